```python
import math
import jax
import jax.numpy as jnp
from jax import lax
import numpy as np

D_MODEL = 1024
BATCH = 8
SEQ = 2048
DEPTH = 2

GRID_W = 64
CTX_LEN = 256
N_MIXERS = 2
N_RWKV = (DEPTH + 1) // 2
N_GDN = DEPTH // 2
NORM_EPS = 1e-6
POS_BASE = 10000.0

RW_HEAD = 64
RW_HEADS = D_MODEL // RW_HEAD
RW_DECAY_LORA = 64
RW_AAA_LORA = 64
RW_GATE_LORA = 128
RW_GN_EPS = 64e-5

GD_DK = 128
GD_DV = 128
GD_KH = D_MODEL // GD_DK
GD_VH = 2 * GD_KH
GD_KD = GD_KH * GD_DK
GD_VD = GD_VH * GD_DV
GD_QKV = 2 * GD_KD + GD_VD
GD_CONV_W = 5
GD_CHUNK = 64

MOE_GROUPS = 4
MOE_EXPERTS = 8
MOE_TOPK = 2
MOE_FF = 3 * D_MODEL // 8

kernel_name = "hybrid_rwkv7_gdn_hmoe_prefix_dit"


def _rmsnorm(t, g):
    tf = t.astype(jnp.float32)
    y = tf * lax.rsqrt(jnp.mean(tf * tf, axis=-1, keepdims=True) + NORM_EPS)
    return (y * g.astype(jnp.float32)).astype(t.dtype)


def _modulate(t, shift, scale):
    return t * (1.0 + scale) + shift


def _l2norm(t):
    return t * lax.rsqrt(jnp.sum(t * t, axis=-1, keepdims=True) + NORM_EPS)


def _pos_embed_2d(rows, d):
    quarter = d // 4
    omega = 1.0 / (POS_BASE ** (jnp.arange(quarter, dtype=jnp.float32) / quarter))
    def axis_emb(n):
        ang = jnp.arange(n, dtype=jnp.float32)[:, None] * omega[None, :]
        return jnp.concatenate([jnp.sin(ang), jnp.cos(ang)], axis=-1)
    e_row = jnp.broadcast_to(axis_emb(rows)[:, None, :], (rows, GRID_W, d // 2))
    e_col = jnp.broadcast_to(axis_emb(GRID_W)[None, :, :], (rows, GRID_W, d // 2))
    return jnp.concatenate([e_row, e_col], axis=-1).reshape(rows * GRID_W, d)


def _centred_shift(h):
    z = jnp.zeros_like(h[:, :1])
    prev = jnp.concatenate([z, h[:, :-1]], axis=1)
    nxt = jnp.concatenate([h[:, 1:], z], axis=1)
    return 0.5 * (prev + nxt) - h


def _depthwise_conv(t, w):
    pad = (w.shape[0] - 1) // 2
    return lax.conv_general_dilated(t, w[:, None, :].astype(t.dtype), window_strides=(1,),
                                    padding=[(pad, pad)], dimension_numbers=("NWC", "WIO", "NWC"),
                                    feature_group_count=t.shape[-1])


def _rwkv7_features(h, mix, w_rkv, w0, w1, w2, a0, a1, a2, g1, g2, k_k, k_a):
    xx = _centred_shift(h)
    xr, xw, xk, xv, xa, xg = [h + xx * mix[m] for m in range(6)]
    rkv = jnp.einsum("zbtd,zde->zbte", jnp.stack([xr, xk, xv]), w_rkv)
    r, k, v = rkv[0], rkv[1], rkv[2]
    w_lora = jnp.einsum("zbtr,zrd->zbtd", jnp.tanh(jnp.einsum("btd,zdr->zbtr", xw, w1)), w2)
    w_log = -jax.nn.softplus(-(w0[:, None, None, :] + w_lora).astype(jnp.float32)) - 0.5
    decay = jnp.exp(-jnp.exp(w_log))
    a_lora = jnp.einsum("zbtr,zrd->zbtd", jnp.einsum("btd,zdr->zbtr", xa, a1), a2)
    a_rate = jax.nn.sigmoid((a0[:, None, None, :] + a_lora).astype(jnp.float32))
    gate = jax.nn.sigmoid(xg @ g1) @ g2
    heads = lambda t: t.reshape(t.shape[:-1] + (RW_HEADS, RW_HEAD)).astype(jnp.float32)
    kk = _l2norm(heads(k * k_k))
    k_dir = heads(k)[None] * (1.0 + (heads(a_rate) - 1.0) * heads(k_a))
    return {"r": heads(r), "v": heads(v), "k": k_dir, "decay": heads(decay),
            "a": heads(a_rate), "kk": kk, "gate": gate}


def _rwkv7_scan(f, z, s0, reverse):
    kk = f["kk"]
    seq = (f["r"], f["decay"][z], f["k"][z], f["v"], -kk, kk * f["a"][z])
    def step(s, inp):
        r_t, w_t, k_t, v_t, a_t, b_t = inp
        sa = jnp.einsum("bhvk,bhk->bhv", s, a_t)
        s = s * w_t[:, :, None, :] + sa[..., None] * b_t[:, :, None, :] + v_t[..., None] * k_t[:, :, None, :]
        return s, jnp.einsum("bhvk,bhk->bhv", s, r_t)
    s_fin, y = lax.scan(step, s0, tuple(jnp.moveaxis(t, 1, 0) for t in seq), reverse=reverse)
    return jnp.moveaxis(y, 0, 1), s_fin


def _rwkv7_readout(f, y, r_k, ln_g, ln_b, w_o, dtype):
    bsz, t_len, n_h, n_d = y.shape
    mu = jnp.mean(y, axis=-1, keepdims=True)
    var = jnp.mean(jnp.square(y - mu), axis=-1, keepdims=True)
    yn = ((y - mu) * lax.rsqrt(var + RW_GN_EPS)).reshape(bsz, t_len, n_h * n_d)
    yn = yn * ln_g.astype(jnp.float32) + ln_b.astype(jnp.float32)
    k_bonus = 0.5 * (f["k"][0] + f["k"][1])
    bonus = jnp.sum(f["r"] * k_bonus * r_k.astype(jnp.float32), axis=-1, keepdims=True) * f["v"]
    out = (yn + bonus.reshape(bsz, t_len, n_h * n_d)).astype(dtype) * f["gate"]
    return out @ w_o


def _rwkv7_mixer(h_c, h_l, mix, w_rkv, w0, w1, w2, a0, a1, a2, g1, g2, k_k, k_a, r_k, ln_g, ln_b, w_o,
                 need_ctx_out):
    f_c = _rwkv7_features(h_c, mix, w_rkv, w0, w1, w2, a0, a1, a2, g1, g2, k_k, k_a)
    f_l = _rwkv7_features(h_l, mix, w_rkv, w0, w1, w2, a0, a1, a2, g1, g2, k_k, k_a)
    s0 = jnp.zeros((h_l.shape[0], RW_HEADS, RW_HEAD, RW_HEAD), jnp.float32)
    y_cf, s_f = _rwkv7_scan(f_c, 0, s0, False)
    y_cb, s_b = _rwkv7_scan(f_c, 1, s0, True)
    y_lf, _ = _rwkv7_scan(f_l, 0, s_f, False)
    y_lb, _ = _rwkv7_scan(f_l, 1, s_b, True)
    y_l = _rwkv7_readout(f_l, y_lf + y_lb, r_k, ln_g, ln_b, w_o, h_l.dtype)
    y_c = _rwkv7_readout(f_c, y_cf + y_cb, r_k, ln_g, ln_b, w_o, h_c.dtype) if need_ctx_out else None
    return y_c, y_l


def _gdn_features(h, w_in, conv_w, w_ab, a_log, dt_bias):
    bsz, t_len, _ = h.shape
    proj = h @ w_in
    qkv = jax.nn.silu(_depthwise_conv(proj[..., :GD_QKV], conv_w))
    z = proj[..., GD_QKV:]
    q = qkv[..., :GD_KD].reshape(bsz, t_len, GD_KH, GD_DK).astype(jnp.float32)
    k = qkv[..., GD_KD:2 * GD_KD].reshape(bsz, t_len, GD_KH, GD_DK).astype(jnp.float32)
    v = qkv[..., 2 * GD_KD:].reshape(bsz, t_len, GD_VH, GD_DV).astype(jnp.float32)
    rep = GD_VH // GD_KH
    q = jnp.repeat(_l2norm(q) * (GD_DK ** -0.5), rep, axis=2)
    k = jnp.repeat(_l2norm(k), rep, axis=2)
    ab = jnp.einsum("btd,zdh->zbth", h, w_ab).astype(jnp.float32)
    a_in, b_in = ab[..., :GD_VH], ab[..., GD_VH:]
    g = -jnp.exp(a_log.astype(jnp.float32))[:, None, None, :] * jax.nn.softplus(
        a_in + dt_bias.astype(jnp.float32)[:, None, None, :])
    beta = jax.nn.sigmoid(b_in)
    return {"q": q, "k": k, "v": v, "g": g, "beta": beta, "z": z}


def _gated_delta_chunked(q, k, v, g, beta, s0, reverse):
    if reverse:
        q, k, v, g, beta = [jnp.flip(t, axis=1) for t in (q, k, v, g, beta)]
    bsz, t_len, n_h, _ = q.shape
    n_blk = t_len // GD_CHUNK
    blk = lambda t: jnp.moveaxis(t.reshape((bsz, n_blk, GD_CHUNK, n_h) + t.shape[3:]), 3, 1)
    q, k, v, g, beta = blk(q), blk(k), blk(v), blk(g), blk(beta)
    G = jnp.cumsum(g, axis=-1)
    idx = jnp.arange(GD_CHUNK)
    lower = idx[:, None] >= idx[None, :]
    strict = idx[:, None] > idx[None, :]
    gamma = jnp.exp(jnp.where(lower, G[..., :, None] - G[..., None, :], -jnp.inf))
    kb = k * beta[..., None]
    A = jnp.where(strict, jnp.einsum("bhnik,bhnjk->bhnij", kb, k) * gamma, 0.0)
    M = A + jnp.eye(GD_CHUNK, dtype=A.dtype)
    u = lax.linalg.triangular_solve(M, v * beta[..., None], left_side=True, lower=True)
    w = lax.linalg.triangular_solve(M, kb * jnp.exp(G)[..., None], left_side=True, lower=True)
    aqk = jnp.einsum("bhnik,bhnjk->bhnij", q, k) * gamma
    qg = q * jnp.exp(G)[..., None]
    kg = k * jnp.exp(G[..., -1:] - G)[..., None]
    gl = jnp.exp(G[..., -1])
    def step(s, inp):
        u_n, w_n, qg_n, kg_n, aqk_n, gl_n = inp
        v_new = u_n - jnp.einsum("bhck,bhkv->bhcv", w_n, s)
        o = jnp.einsum("bhck,bhkv->bhcv", qg_n, s) + jnp.einsum("bhij,bhjv->bhiv", aqk_n, v_new)
        s = s * gl_n[..., None, None] + jnp.einsum("bhck,bhcv->bhkv", kg_n, v_new)
        return s, o
    xs = tuple(jnp.moveaxis(t, 2, 0) for t in (u, w, qg, kg, aqk, gl))
    s_fin, o = lax.scan(step, s0, xs)
    o = jnp.moveaxis(jnp.moveaxis(o, 0, 2), 1, 3).reshape(bsz, t_len, n_h, o.shape[-1])
    if reverse:
        o = jnp.flip(o, axis=1)
    return o, s_fin


def _gdn_dir(f, z, s0, reverse):
    return _gated_delta_chunked(f["q"], f["k"], f["v"], f["g"][z], f["beta"][z], s0, reverse)


def _gdn_readout(o, z, norm_g, w_o):
    bsz, t_len = o.shape[:2]
    on = o * lax.rsqrt(jnp.mean(o * o, axis=-1, keepdims=True) + NORM_EPS) * norm_g.astype(jnp.float32)
    on = on.reshape(bsz, t_len, GD_VD).astype(z.dtype) * jax.nn.silu(z)
    return on @ w_o


def _gdn_mixer(h_c, h_l, w_in, conv_w, w_ab, a_log, dt_bias, norm_g, w_o, need_ctx_out):
    f_c = _gdn_features(h_c, w_in, conv_w, w_ab, a_log, dt_bias)
    f_l = _gdn_features(h_l, w_in, conv_w, w_ab, a_log, dt_bias)
    s0 = jnp.zeros((h_l.shape[0], GD_VH, GD_DK, GD_DV), jnp.float32)
    o_cf, s_f = _gdn_dir(f_c, 0, s0, False)
    o_cb, s_b = _gdn_dir(f_c, 1, s0, True)
    o_lf, _ = _gdn_dir(f_l, 0, s_f, False)
    o_lb, _ = _gdn_dir(f_l, 1, s_b, True)
    y_l = _gdn_readout(o_lf + o_lb, f_l["z"], norm_g, w_o)
    y_c = _gdn_readout(o_cf + o_cb, f_c["z"], norm_g, w_o) if need_ctx_out else None
    return y_c, y_l


def _hier_moe(t, w_rg, b_rg, w_re, b_re, w_gate, w_up, w_down):
    n_tok = t.shape[0]
    tf = t.astype(jnp.float32)
    p_group = jax.nn.softmax(tf @ w_rg.astype(jnp.float32) + b_rg.astype(jnp.float32), axis=-1)
    gp, gi = lax.top_k(p_group, 1)
    g_onehot = jax.nn.one_hot(gi[:, 0], MOE_GROUPS, dtype=jnp.float32)
    e_logits = (tf @ w_re.astype(jnp.float32) + b_re.astype(jnp.float32)).reshape(n_tok, MOE_GROUPS, MOE_EXPERTS)
    e_sel = jnp.sum(e_logits * g_onehot[:, :, None], axis=1)
    top_l, top_i = lax.top_k(e_sel, MOE_TOPK)
    top_p = jax.nn.softmax(top_l, axis=-1) * gp
    w_exp = jnp.sum(jax.nn.one_hot(top_i, MOE_EXPERTS, dtype=jnp.float32) * top_p[..., None], axis=1)
    combine = (g_onehot[:, :, None] * w_exp[:, None, :]).astype(t.dtype)
    out = jnp.zeros_like(t)
    for grp in range(MOE_GROUPS):
        hg = jnp.einsum("td,edf->tef", t, w_gate[grp])
        hu = jnp.einsum("td,edf->tef", t, w_up[grp])
        act = jax.nn.silu(hg) * hu * combine[:, grp, :, None]
        out = out + jnp.einsum("tef,efd->td", act, w_down[grp])
    return out


def setup_inputs(seed: int = 0) -> dict:
    key = jax.random.key(seed)
    keys = iter(jax.random.split(key, 64))
    def nrm(shape, scale):
        return jax.random.normal(next(keys), shape, jnp.float32) * scale
    def uni(shape, lo, hi):
        return jax.random.uniform(next(keys), shape, jnp.float32, lo, hi)
    D, L, LA, LB = D_MODEL, DEPTH, N_RWKV, N_GDN
    dt = jnp.exp(uni((LB, 2, GD_VH), math.log(1e-3), math.log(1e-1)))
    return {
        "x": nrm((BATCH, SEQ, D), 1.0),
        "c": nrm((BATCH, D), 1.0),
        "ctx": nrm((BATCH, CTX_LEN, D), 1.0),
        "c_ctx": nrm((D,), 1.0),
        "ada_w": nrm((L, D, 6 * D), D ** -0.5),
        "ada_b": nrm((L, 6 * D), 0.02),
        "norm1_g": 1.0 + nrm((L, D), 0.02),
        "norm2_g": 1.0 + nrm((L, D), 0.02),
        "rw_mix": uni((LA, 6, D), 0.0, 1.0),
        "rw_w_rkv": nrm((LA, 3, D, D), D ** -0.5),
        "rw_w0": uni((LA, 2, D), -6.0, -1.0),
        "rw_w1": nrm((LA, 2, D, RW_DECAY_LORA), D ** -0.5),
        "rw_w2": nrm((LA, 2, RW_DECAY_LORA, D), 0.5 * RW_DECAY_LORA ** -0.5),
        "rw_a0": nrm((LA, 2, D), 0.1),
        "rw_a1": nrm((LA, 2, D, RW_AAA_LORA), D ** -0.5),
        "rw_a2": nrm((LA, 2, RW_AAA_LORA, D), 0.5 * RW_AAA_LORA ** -0.5),
        "rw_g1": nrm((LA, D, RW_GATE_LORA), D ** -0.5),
        "rw_g2": nrm((LA, RW_GATE_LORA, D), RW_GATE_LORA ** -0.5),
        "rw_k_k": 0.85 + nrm((LA, D), 0.05),
        "rw_k_a": 1.0 + nrm((LA, D), 0.05),
        "rw_r_k": nrm((LA, RW_HEADS, RW_HEAD), 0.1),
        "rw_ln_g": 1.0 + nrm((LA, D), 0.02),
        "rw_ln_b": nrm((LA, D), 0.02),
        "rw_w_o": nrm((LA, D, D), D ** -0.5),
        "gd_w_in": nrm((LB, D, GD_QKV + GD_VD), D ** -0.5),
        "gd_conv": nrm((LB, GD_CONV_W, GD_QKV), GD_CONV_W ** -0.5),
        "gd_w_ab": nrm((LB, 2, D, 2 * GD_VH), D ** -0.5),
        "gd_a_log": jnp.log(uni((LB, 2, GD_VH), 1.0, 16.0)),
        "gd_dt_bias": dt + jnp.log(-jnp.expm1(-dt)),
        "gd_norm_g": 1.0 + nrm((LB, GD_DV), 0.02),
        "gd_w_o": nrm((LB, GD_VD, D), GD_VD ** -0.5),
        "moe_w_rg": nrm((L, D, MOE_GROUPS), D ** -0.5),
        "moe_b_rg": nrm((L, MOE_GROUPS), 0.01),
        "moe_w_re": nrm((L, D, MOE_GROUPS * MOE_EXPERTS), D ** -0.5),
        "moe_b_re": nrm((L, MOE_GROUPS * MOE_EXPERTS), 0.01),
        "moe_w_gate": nrm((L, MOE_GROUPS, MOE_EXPERTS, D, MOE_FF), D ** -0.5),
        "moe_w_up": nrm((L, MOE_GROUPS, MOE_EXPERTS, D, MOE_FF), D ** -0.5),
        "moe_w_down": nrm((L, MOE_GROUPS, MOE_EXPERTS, MOE_FF, D), MOE_FF ** -0.5),
        "final_g": 1.0 + nrm((D,), 0.02),
    }


def reference(x, c, ctx, c_ctx, ada_w, ada_b, norm1_g, norm2_g,
              rw_mix, rw_w_rkv, rw_w0, rw_w1, rw_w2, rw_a0, rw_a1, rw_a2, rw_g1, rw_g2,
              rw_k_k, rw_k_a, rw_r_k, rw_ln_g, rw_ln_b, rw_w_o,
              gd_w_in, gd_conv, gd_w_ab, gd_a_log, gd_dt_bias, gd_norm_g, gd_w_o,
              moe_w_rg, moe_b_rg, moe_w_re, moe_b_re, moe_w_gate, moe_w_up, moe_w_down,
              final_g):
    n_lat, d = x.shape[1], x.shape[2]
    n_ctx = ctx.shape[1]
    rows = n_lat // GRID_W
    x = x + _pos_embed_2d(rows, d).astype(x.dtype)
    silu_c = jax.nn.silu(c)
    silu_cc = jax.nn.silu(c_ctx)
    for i in range(DEPTH):
        last = i == DEPTH - 1
        mod_l = (silu_c @ ada_w[i] + ada_b[i])[:, None, :]
        mod_c = (silu_cc @ ada_w[i] + ada_b[i])[None, None, :]
        sh1_l, sc1_l, g1_l, sh2_l, sc2_l, g2_l = jnp.split(mod_l, 6, axis=-1)
        sh1_c, sc1_c, g1_c, sh2_c, sc2_c, g2_c = jnp.split(mod_c, 6, axis=-1)
        h_l = _modulate(_rmsnorm(x, norm1_g[i]), sh1_l, sc1_l)
        h_c = _modulate(_rmsnorm(ctx, norm1_g[i]), sh1_c, sc1_c)
        j = i // N_MIXERS
        if i % N_MIXERS == 0:
            y_c, y_l = _rwkv7_mixer(h_c, h_l, rw_mix[j], rw_w_rkv[j], rw_w0[j], rw_w1[j], rw_w2[j],
                                    rw_a0[j], rw_a1[j], rw_a2[j], rw_g1[j], rw_g2[j], rw_k_k[j], rw_k_a[j],
                                    rw_r_k[j], rw_ln_g[j], rw_ln_b[j], rw_w_o[j], not last)
        else:
            y_c, y_l = _gdn_mixer(h_c, h_l, gd_w_in[j], gd_conv[j], gd_w_ab[j], gd_a_log[j],
                                  gd_dt_bias[j], gd_norm_g[j], gd_w_o[j], not last)
        x = x + g1_l * y_l
        h_l = _modulate(_rmsnorm(x, norm2_g[i]), sh2_l, sc2_l)
        moe_args = (moe_w_rg[i], moe_b_rg[i], moe_w_re[i], moe_b_re[i], moe_w_gate[i], moe_w_up[i], moe_w_down[i])
        if last:
            x = x + g2_l * _hier_moe(h_l.reshape(-1, d), *moe_args).reshape(x.shape)
        else:
            ctx = ctx + g1_c * y_c
            h_c = _modulate(_rmsnorm(ctx, norm2_g[i]), sh2_c, sc2_c)
            tok = jnp.concatenate([h_c, h_l], axis=1)
            m = _hier_moe(tok.reshape(-1, d), *moe_args).reshape(tok.shape)
            ctx = ctx + g2_c * m[:, :n_ctx]
            x = x + g2_l * m[:, n_ctx:]
    return _rmsnorm(x, final_g)
```

```python
import functools
import math

import jax
import jax.numpy as jnp
from jax import lax
from jax.experimental import pallas as pl
from jax.experimental.pallas import tpu as pltpu

F32 = jnp.float32
BF16 = jnp.bfloat16

NORM_EPS = 1e-6
RW_GN_EPS = 64e-5
POS_BASE = 10000.0
GRID_W = 64
RW_HEAD = 64
RW_LORA = 64
GD_DK = 128
GD_DV = 128
GD_CONV_W = 5
MOE_GROUPS = 4
MOE_EXPERTS = 8
CHUNK = 64
LANES = 128
HALO = 8
TOKEN_TILE = 256
VMEM_LIMIT = 56 * 1024 * 1024
NEG_BIG = -1e30


def _bf(x):
    return x.astype(BF16)


def _dot(a, b):
    return jnp.dot(_bf(a), _bf(b), preferred_element_type=F32)


def _dot_nt(a, b):
    return lax.dot_general(_bf(a), _bf(b), (((1,), (1,)), ((), ())), preferred_element_type=F32)


def _dot_tn(a, b):
    return lax.dot_general(_bf(a), _bf(b), (((0,), (0,)), ((), ())), preferred_element_type=F32)


def _split2(x):
    hi = x.astype(BF16)
    lo = (x - hi.astype(F32)).astype(BF16)
    return hi, lo


def _split3(x):
    hi = x.astype(BF16)
    r1 = x - hi.astype(F32)
    mid = r1.astype(BF16)
    lo = (r1 - mid.astype(F32)).astype(BF16)
    return hi, mid, lo


def _dot_x2(a, b_exact):
    hi, lo = _split2(a)
    b = _bf(b_exact)
    return (jnp.dot(hi, b, preferred_element_type=F32) + jnp.dot(lo, b, preferred_element_type=F32))


def _dot_x3(a, b_exact):
    hi, mid, lo = _split3(a)
    b = _bf(b_exact)
    return (jnp.dot(hi, b, preferred_element_type=F32) + jnp.dot(mid, b, preferred_element_type=F32)
            + jnp.dot(lo, b, preferred_element_type=F32))


def _dot_ex3(a_exact, b):
    hi, mid, lo = _split3(b)
    a = _bf(a_exact)
    return (jnp.dot(a, hi, preferred_element_type=F32) + jnp.dot(a, mid, preferred_element_type=F32)
            + jnp.dot(a, lo, preferred_element_type=F32))


def _dot_hh(a, b):
    ah, al = _split2(a)
    bh, bl = _split2(b)
    return (jnp.dot(ah, bh, preferred_element_type=F32) + jnp.dot(al, bh, preferred_element_type=F32)
            + jnp.dot(ah, bl, preferred_element_type=F32))


def _sigmoid(x):
    return 1.0 / (1.0 + jnp.exp(-x))


def _silu(x):
    return x * _sigmoid(x)


def _softplus(x):
    return jnp.maximum(x, 0.0) + jnp.log(1.0 + jnp.exp(-jnp.abs(x)))


def _norm_mod(x, g, shift, scale):
    ms = jnp.mean(x * x, axis=-1, keepdims=True)
    return (x * lax.rsqrt(ms + NORM_EPS) * g) * (1.0 + scale) + shift


def _tile_rows(x, n):
    return jnp.concatenate([x] * n, axis=0)


def _tile_lanes(x, n):
    return jnp.concatenate([x] * n, axis=1)


def _order_masks(direction, c, n=1):
    row = lax.broadcasted_iota(jnp.int32, (c, n * c), 0)
    col = lax.broadcasted_iota(jnp.int32, (c, n * c), 1) % c
    diff = (col - row) * (1 - 2 * direction)
    return diff <= 0, diff < 0, jnp.where(diff == 0, 1.0, 0.0)


def _block_mask(rows, cols, rblk, cblk, rdiv=1):
    r = lax.broadcasted_iota(jnp.int32, (rows, cols), 0) // rblk
    c = lax.broadcasted_iota(jnp.int32, (rows, cols), 1) // cblk
    return (r // rdiv) == c if rdiv != 1 else r == c


TRI_BASE = 4


def _tri_inv(a, bd, eye_t):
    c = a.shape[0]
    n = a.shape[1] // c
    row = lax.broadcasted_iota(jnp.int32, a.shape, 0)
    col = lax.broadcasted_iota(jnp.int32, a.shape, 1) % c

    def same_block(m):
        return (row // m) == (col // m)

    def mm(x, y):
        return jnp.dot(_bf(x), _bf(jnp.where(bd, _tile_rows(y, n), 0.0)), preferred_element_type=F32)

    a0 = jnp.where(same_block(TRI_BASE), a, 0.0)
    x = eye_t + a0
    p = a0
    for _ in range(int(math.log2(TRI_BASE)) - 1):
        p = mm(p, p)
        x = x + mm(p, x)
    m = TRI_BASE
    while m < c:
        a_off = jnp.where(jnp.logical_and(same_block(2 * m), jnp.logical_not(same_block(m))), a, 0.0)
        x = x + mm(x, mm(a_off, x))
        m *= 2
    return x


def _chunk_index(direction, j, n_ctx_chunks, n_chunks):
    bwd = jnp.where(j < n_ctx_chunks, n_ctx_chunks - 1 - j, n_chunks + n_ctx_chunks - 1 - j)
    return jnp.where(direction == 0, j, bwd)


def _const_spec(shape):
    nd = len(shape)
    return pl.BlockSpec(shape, lambda *_: (0,) * nd, pipeline_mode=pl.Buffered(1))


def _params(n_axes):
    return pltpu.CompilerParams(dimension_semantics=("arbitrary",) * n_axes, vmem_limit_bytes=VMEM_LIMIT)


def _ada_kernel(cs_ref, w_ref, b_ref, o_ref):
    s = _silu(cs_ref[...])
    o_ref[0] = _dot_hh(s, w_ref[0]) + b_ref[0]


def _ada(cs, ada_w, ada_b):
    n_layers, d, n6 = ada_w.shape
    rows = cs.shape[0]
    tn = 1536
    return pl.pallas_call(
        _ada_kernel,
        out_shape=jax.ShapeDtypeStruct((n_layers, rows, n6), F32),
        grid=(n_layers, n6 // tn),
        in_specs=[pl.BlockSpec((rows, d), lambda i, j: (0, 0)),
                  pl.BlockSpec((1, d, tn), lambda i, j: (i, 0, j)),
                  pl.BlockSpec((1, 1, tn), lambda i, j: (i, 0, j))],
        out_specs=pl.BlockSpec((1, rows, tn), lambda i, j: (i, 0, j)),
        compiler_params=_params(2),
        name="ada",
    )(cs, ada_w, ada_b.reshape(n_layers, 1, n6))


def _shifted_neighbours(h, h_prev, h_next, pos, tm, n_ctx, n_tot):
    prev_ok = jnp.logical_and(pos != 0, pos != n_ctx)
    next_ok = jnp.logical_and(pos + tm != n_ctx, pos + tm != n_tot)
    h_prev = jnp.where(prev_ok, h_prev, 0.0)
    h_next = jnp.where(next_ok, h_next, 0.0)
    row = lax.broadcasted_iota(jnp.int32, (tm, 1), 0)
    prev = jnp.where(row == 0, h_prev, pltpu.roll(h, 1, axis=0))
    nxt = jnp.where(row == tm - 1, h_next, pltpu.roll(h, tm - 1, axis=0))
    return prev, nxt


def _tile_specs(tm, d, n_tiles, lead=()):
    nl = len(lead)
    hb = tm // HALO
    last = n_tiles * hb - 1

    def cur(*g):
        return (g[nl], g[nl + 1], 0)

    def prv(*g):
        return (g[nl], jnp.maximum(g[nl + 1] * hb - 1, 0), 0)

    def nxt(*g):
        return (g[nl], jnp.minimum((g[nl + 1] + 1) * hb, last), 0)

    return [pl.BlockSpec((1, tm, d), cur), pl.BlockSpec((1, HALO, d), prv), pl.BlockSpec((1, HALO, d), nxt)]


def _rw_feat_kernel(x_ref, xp_ref, xn_ref, pos_ref, posp_ref, posn_ref, mod_ref, ng_ref, mix_ref,
                    wrkv_ref, w0_ref, w1_ref, w2_ref, a0_ref, a1_ref, a2_ref, g1_ref, g2_ref,
                    kkp_ref, kap_ref, gsum_ref, gbc_ref,
                    r_o, v_o, kk_o, gate_o, lw_o, kd_o, a_o, *, tm, n_ctx, n_tot):
    d = x_ref.shape[-1]
    pos = pl.program_id(1) * tm
    sh = mod_ref[0, 0, 0:1, :]
    sc = mod_ref[0, 0, 1:2, :]
    g = ng_ref[...]
    h = _norm_mod(x_ref[0] + pos_ref[...], g, sh, sc)
    hp = _norm_mod(xp_ref[0, HALO - 1:HALO, :] + posp_ref[HALO - 1:HALO, :], g, sh, sc)
    hn = _norm_mod(xn_ref[0, 0:1, :] + posn_ref[0:1, :], g, sh, sc)
    prev, nxt = _shifted_neighbours(h, hp, hn, pos, tm, n_ctx, n_tot)
    xx = 0.5 * (prev + nxt) - h
    mix = mix_ref[...]
    xr = h + xx * mix[0:1]
    xw = h + xx * mix[1:2]
    xk = h + xx * mix[2:3]
    xv = h + xx * mix[3:4]
    xa = h + xx * mix[4:5]
    xg = h + xx * mix[5:6]

    r_o[0] = _dot(xr, wrkv_ref[0])
    v_o[0] = _dot(xv, wrkv_ref[2])
    gate_o[0] = _dot(_sigmoid(_dot(xg, g1_ref[...])), g2_ref[...])
    k = _dot(xk, wrkv_ref[1])

    kq = k * kkp_ref[...]
    ssq = _dot_x2(kq * kq, gsum_ref[...])
    kk_o[0] = kq * _dot_x2(lax.rsqrt(ssq + NORM_EPS), gbc_ref[...])

    wl = _dot(jnp.tanh(_dot(xw, w1_ref[...])), w2_ref[...])
    al = _dot(_dot(xa, a1_ref[...]), a2_ref[...])
    ka = kap_ref[...]
    for z in range(2):
        zw = w0_ref[z:z + 1, :] + wl[:, z * d:(z + 1) * d]
        lw_o[z, 0] = (-math.exp(-0.5)) * _sigmoid(zw)
        rate = _sigmoid(a0_ref[z:z + 1, :] + al[:, z * d:(z + 1) * d])
        a_o[z, 0] = rate
        kd_o[z, 0] = k * (1.0 + (rate - 1.0) * ka)


def _rw_feat(xs, pos, mod, ng, p, gsum, gbc, n_ctx):
    b, l, d = xs.shape
    tm = TOKEN_TILE
    nt = l // tm
    nct = n_ctx // tm
    hb = tm // HALO
    kern = functools.partial(_rw_feat_kernel, tm=tm, n_ctx=n_ctx, n_tot=l)
    tok = pl.BlockSpec((1, tm, d), lambda i, t: (i, t, 0))
    tok2 = pl.BlockSpec((2, 1, tm, d), lambda i, t: (0, i, t, 0))
    pos_specs = [pl.BlockSpec((tm, d), lambda i, t: (t, 0)),
                 pl.BlockSpec((HALO, d), lambda i, t: (jnp.maximum(t * hb - 1, 0), 0)),
                 pl.BlockSpec((HALO, d), lambda i, t: (jnp.minimum((t + 1) * hb, nt * hb - 1), 0))]
    in_specs = (_tile_specs(tm, d, nt) + pos_specs
                + [pl.BlockSpec((1, 1, 6, d), lambda i, t: (i, (t >= nct).astype(jnp.int32), 0, 0))]
                + [_const_spec(a.shape) for a in (ng, p["mix"], p["w_rkv"], p["w0"], p["w1"], p["w2"], p["a0"],
                                                  p["a1"], p["a2"], p["g1"], p["g2"], p["k_k"], p["k_a"], gsum, gbc)])
    sd = jax.ShapeDtypeStruct((b, l, d), F32)
    sd2 = jax.ShapeDtypeStruct((2, b, l, d), F32)
    return pl.pallas_call(
        kern,
        out_shape=(sd, sd, sd, sd, sd2, sd2, sd2),
        grid=(b, nt),
        in_specs=in_specs,
        out_specs=(tok, tok, tok, tok, tok2, tok2, tok2),
        compiler_params=_params(2),
        name="rw_feat",
    )(xs, xs, xs, pos, pos, pos, mod, ng, p["mix"], p["w_rkv"], p["w0"], p["w1"], p["w2"], p["a0"], p["a1"], p["a2"],
      p["g1"], p["g2"], p["k_k"], p["k_a"], gsum, gbc)


def _rw_scan_kernel(r_ref, v_ref, kk_ref, lw_ref, kd_ref, a_ref, y_ref, s_ref, *, heads_per_group):
    c = r_ref.shape[1]
    d = r_ref.shape[2]
    gw = heads_per_group * RW_HEAD
    n_groups = d // gw
    direction = pl.program_id(1)
    j = pl.program_id(2)

    @pl.when(j == 0)
    def _():
        s_ref[...] = jnp.zeros_like(s_ref)

    incl, _, _ = _order_masks(direction, c)
    lw = lw_ref[0, 0]
    lam = _dot_ex3(jnp.where(incl, 1.0, 0.0), lw)
    tot = jnp.sum(lw, axis=0, keepdims=True)
    w_in = jnp.exp(lam)
    w_ex = jnp.exp(lam - lw)
    w_inv = jnp.exp(-lam)
    w_end = jnp.exp(tot - lam)
    w_tot = jnp.exp(tot)

    kk = kk_ref[0]
    kd = kd_ref[0, 0]
    bb = kk * a_ref[0, 0]
    a_t = -(kk * w_ex)
    r_t = r_ref[0] * w_in
    b_t = bb * w_inv
    k_t = kd * w_inv
    b_e = bb * w_end
    k_e = kd * w_end
    v = v_ref[0]

    n = heads_per_group
    bd = _block_mask(n * c, gw, c, RW_HEAD)
    incl_t, strict_t, eye_t = _order_masks(direction, c, n)

    def bdiag(x):
        return jnp.where(bd, _tile_rows(x, n), 0.0)

    for gi in range(n_groups):
        sl = slice(gi * gw, (gi + 1) * gw)
        lhs = _bf(jnp.concatenate([a_t[:, sl], r_t[:, sl]], axis=0))
        ab = lax.dot_general(lhs, _bf(bdiag(b_t[:, sl])), (((1,), (1,)), ((), ())), preferred_element_type=F32)
        ak = lax.dot_general(lhs, _bf(bdiag(k_t[:, sl])), (((1,), (1,)), ((), ())), preferred_element_type=F32)
        a_ab = jnp.where(strict_t, ab[:c], 0.0)
        a_rb = jnp.where(incl_t, ab[c:], 0.0)
        a_ak = jnp.where(strict_t, ak[:c], 0.0)
        a_rk = jnp.where(incl_t, ak[c:], 0.0)
        t_inv = _tri_inv(a_ab, bd, eye_t)

        s = s_ref[gi]
        pre = lax.dot_general(lhs, _bf(s), (((1,), (1,)), ((), ())), preferred_element_type=F32)
        vg = v[:, sl]
        av = _dot(jnp.concatenate([a_ak, a_rk], axis=0), bdiag(vg))
        u = _dot(t_inv, bdiag(pre[:c] + av[:c]))
        y_ref[0, 0, :, sl] = pre[c:] + av[c:] + _dot(a_rb, bdiag(u))
        upd = _dot_tn(jnp.concatenate([u, vg], axis=0), jnp.concatenate([b_e[:, sl], k_e[:, sl]], axis=0))
        s_ref[gi] = s * w_tot[:, sl] + jnp.where(bd, upd, 0.0)


def _rw_scan(r, v, kk, lw, kd, a, n_ctx):
    b, l, d = r.shape
    c = CHUNK
    nc = l // c
    ncc = n_ctx // c
    hpg = 4
    gw = hpg * RW_HEAD

    def shared(i, z, j):
        return (i, _chunk_index(z, j, ncc, nc), 0)

    def perdir(i, z, j):
        return (z, i, _chunk_index(z, j, ncc, nc), 0)

    s1 = pl.BlockSpec((1, c, d), shared)
    s2 = pl.BlockSpec((1, 1, c, d), perdir)
    return pl.pallas_call(
        functools.partial(_rw_scan_kernel, heads_per_group=hpg),
        out_shape=jax.ShapeDtypeStruct((2, b, l, d), F32),
        grid=(b, 2, nc),
        in_specs=[s1, s1, s1, s2, s2, s2],
        out_specs=s2,
        scratch_shapes=[pltpu.VMEM((d // gw, gw, gw), F32)],
        compiler_params=_params(3),
        name="rw_scan",
    )(r, v, kk, lw, kd, a)


def _rw_out_kernel(x_ref, pos_ref, y_ref, r_ref, v_ref, kd_ref, gate_ref, mod_ref, rk_ref, lng_ref, lnb_ref,
                   wo_ref, gsum_ref, gbc_ref, xo_ref):
    inv_n = 1.0 / RW_HEAD
    gsum = gsum_ref[...]
    gbc = gbc_ref[...]
    y = y_ref[0, 0] + y_ref[1, 0]
    mu = _dot_x2(_dot_x2(y, gsum) * inv_n, gbc)
    yc = y - mu
    var = _dot_x2(yc * yc, gsum) * inv_n
    yn = yc * _dot_x2(lax.rsqrt(var + RW_GN_EPS), gbc) * lng_ref[...] + lnb_ref[...]
    k_bonus = 0.5 * (kd_ref[0, 0] + kd_ref[1, 0])
    bonus = _dot_x2(_dot_x2(r_ref[0] * k_bonus * rk_ref[...], gsum), gbc) * v_ref[0]
    out = (yn + bonus) * gate_ref[0]
    o = _dot(out, wo_ref[...])
    xo_ref[0] = x_ref[0] + pos_ref[...] + mod_ref[0, 0, 2:3, :] * o


def _rw_out(xs, pos, y, r, v, kd, gate, mod, p, gsum, gbc, n_ctx):
    b, l, d = xs.shape
    tm = TOKEN_TILE
    nt = l // tm
    nct = n_ctx // tm
    tok = pl.BlockSpec((1, tm, d), lambda i, t: (i, t, 0))
    tok2 = pl.BlockSpec((2, 1, tm, d), lambda i, t: (0, i, t, 0))
    consts = (p["r_k"], p["ln_g"], p["ln_b"], p["w_o"], gsum, gbc)
    return pl.pallas_call(
        _rw_out_kernel,
        out_shape=jax.ShapeDtypeStruct((b, l, d), F32),
        grid=(b, nt),
        in_specs=[tok, pl.BlockSpec((tm, d), lambda i, t: (t, 0)), tok2, tok, tok, tok2, tok,
                  pl.BlockSpec((1, 1, 6, d), lambda i, t: (i, (t >= nct).astype(jnp.int32), 0, 0))]
                 + [_const_spec(a.shape) for a in consts],
        out_specs=tok,
        compiler_params=_params(2),
        name="rw_out",
    )(xs, pos, y, r, v, kd, gate, mod, *consts)


def _route(logits):
    ne = MOE_GROUPS * MOE_EXPERTS
    lane = lax.broadcasted_iota(jnp.int32, logits.shape, 1).astype(F32)
    far = float(4 * LANES)
    g_mask = jnp.logical_and(lane >= ne, lane < ne + MOE_GROUPS)
    gl = jnp.where(g_mask, logits, NEG_BIG)
    gmax = jnp.max(gl, axis=1, keepdims=True)
    gsum = jnp.sum(jnp.where(g_mask, jnp.exp(gl - gmax), 0.0), axis=1, keepdims=True)
    gp = 1.0 / gsum
    g_first = jnp.min(jnp.where(gl == gmax, lane, far), axis=1, keepdims=True) - ne
    in_grp = jnp.logical_and(lane >= g_first * MOE_EXPERTS, lane < (g_first + 1.0) * MOE_EXPERTS)
    el = jnp.where(in_grp, logits, NEG_BIG)
    m1 = jnp.max(el, axis=1, keepdims=True)
    i1 = jnp.min(jnp.where(el == m1, lane, far), axis=1, keepdims=True)
    el2 = jnp.where(lane == i1, NEG_BIG, el)
    m2 = jnp.max(el2, axis=1, keepdims=True)
    i2 = jnp.min(jnp.where(el2 == m2, lane, far), axis=1, keepdims=True)
    e2 = jnp.exp(m2 - m1)
    p1 = gp / (1.0 + e2)
    p2 = p1 * e2
    return jnp.where(lane == i1, p1, jnp.where(lane == i2, p2, 0.0))


def _moe_kernel(x_ref, mod_ref, ng_ref, wr_ref, br_ref, wg_ref, wu_ref, wd_ref, fg_ref, xo_ref,
                h_s, comb_s, *, n_ctx, rb, final_norm):
    rows = x_ref.shape[1]
    e = pl.program_id(1)
    n_exp = pl.num_programs(1)
    n_blk = rows // rb
    n_ctx_blk = n_ctx // rb

    def seg_of(i):
        return (i >= n_ctx_blk).astype(jnp.int32) if n_ctx_blk else 1

    @pl.when(e == 0)
    def _():
        def body(i, carry):
            sl = pl.ds(pl.multiple_of(i * rb, rb), rb)
            seg = seg_of(i)
            sh = mod_ref[0, pl.ds(seg, 1), 3, :]
            sc = mod_ref[0, pl.ds(seg, 1), 4, :]
            h = _norm_mod(x_ref[0, sl, :], ng_ref[...], sh, sc)
            h_s[sl, :] = _bf(h)
            comb_s[sl, :] = _route(_dot_hh(h, wr_ref[...]) + br_ref[...])
            xo_ref[0, sl, :] = jnp.zeros((rb, xo_ref.shape[2]), F32)
            return carry
        lax.fori_loop(0, n_blk, body, 0)

    def expert(i, carry):
        sl = pl.ds(pl.multiple_of(i * rb, rb), rb)
        hb = h_s[sl, :]
        hg = jnp.dot(hb, wg_ref[0], preferred_element_type=F32)
        hu = jnp.dot(hb, wu_ref[0], preferred_element_type=F32)
        comb = comb_s[sl, :]
        lane = lax.broadcasted_iota(jnp.int32, comb.shape, 1)
        ce = jnp.sum(jnp.where(lane == e, comb, 0.0), axis=1, keepdims=True)
        act = _silu(hg) * hu * ce
        xo_ref[0, sl, :] += jnp.dot(_bf(act), wd_ref[0], preferred_element_type=F32)
        return carry
    lax.fori_loop(0, n_blk, expert, 0)

    @pl.when(e == n_exp - 1)
    def _():
        def body(i, carry):
            sl = pl.ds(pl.multiple_of(i * rb, rb), rb)
            seg = seg_of(i)
            g2 = mod_ref[0, pl.ds(seg, 1), 5, :]
            xn = x_ref[0, sl, :] + g2 * xo_ref[0, sl, :]
            if final_norm:
                ms = jnp.mean(xn * xn, axis=-1, keepdims=True)
                xn = xn * lax.rsqrt(ms + NORM_EPS) * fg_ref[...]
            xo_ref[0, sl, :] = xn
            return carry
        lax.fori_loop(0, n_blk, body, 0)


def _moe(xs, mod, ng, p, final_g, n_ctx, final_norm):
    b, rows, d = xs.shape
    ne, _, ff = p["w_gate"].shape
    rb = TOKEN_TILE
    kern = functools.partial(_moe_kernel, n_ctx=n_ctx, rb=rb, final_norm=final_norm)
    full = pl.BlockSpec((1, rows, d), lambda i, e: (i, 0, 0))
    full_in = pl.BlockSpec((1, rows, d), lambda i, e: (i, 0, 0), pipeline_mode=pl.Buffered(1))
    return pl.pallas_call(
        kern,
        out_shape=jax.ShapeDtypeStruct((b, rows, d), F32),
        grid=(b, ne),
        in_specs=[full_in, pl.BlockSpec((1, 2, 6, d), lambda i, e: (i, 0, 0, 0)),
                  _const_spec(ng.shape), _const_spec(p["w_r"].shape), _const_spec(p["b_r"].shape),
                  pl.BlockSpec((1, d, ff), lambda i, e: (e, 0, 0)),
                  pl.BlockSpec((1, d, ff), lambda i, e: (e, 0, 0)),
                  pl.BlockSpec((1, ff, d), lambda i, e: (e, 0, 0)),
                  _const_spec(final_g.shape)],
        out_specs=full,
        scratch_shapes=[pltpu.VMEM((rows, d), BF16), pltpu.VMEM((rows, LANES), F32)],
        compiler_params=_params(2),
        name="moe",
    )(xs, mod, ng, p["w_r"], p["b_r"], p["w_gate"], p["w_up"], p["w_down"], final_g)


def _gd_proj_kernel(x_ref, xp_ref, xn_ref, mod_ref, ng_ref, w_ref, conv_ref, o_ref, *, tm, n_ctx, n_tot, n_qk_blk,
                    n_conv_blk):
    jb = pl.program_id(0)
    pos = pl.program_id(2) * tm
    sh = mod_ref[0, 0, 0:1, :]
    sc = mod_ref[0, 0, 1:2, :]
    g = ng_ref[...]
    prev_ok = jnp.logical_and(pos != 0, pos != n_ctx)
    next_ok = jnp.logical_and(pos + tm != n_ctx, pos + tm != n_tot)
    h = _norm_mod(x_ref[0], g, sh, sc)
    hp = jnp.where(prev_ok, _norm_mod(xp_ref[0], g, sh, sc), 0.0)
    hn = jnp.where(next_ok, _norm_mod(xn_ref[0], g, sh, sc), 0.0)
    w = w_ref[...]
    pc = _dot(h, w)

    @pl.when(jb >= n_conv_blk)
    def _():
        o_ref[0] = pc

    @pl.when(jb < n_conv_blk)
    def _():
        ext = jnp.concatenate([_dot(hp, w), pc, _dot(hn, w)], axis=0)
        n_ext = tm + 2 * HALO
        pad = (GD_CONV_W - 1) // 2
        acc = ext[HALO:HALO + tm] * conv_ref[pad:pad + 1, :]
        for wi in range(GD_CONV_W):
            if wi == pad:
                continue
            shifted = pltpu.roll(ext, (pad - wi) % n_ext, axis=0)
            acc = acc + shifted[HALO:HALO + tm] * conv_ref[wi:wi + 1, :]
        act = _silu(acc)

        @pl.when(jb < n_qk_blk)
        def _():
            scale = jnp.where(jb < n_qk_blk // 2, GD_DK ** -0.5, 1.0)
            parts = []
            for hh in range(act.shape[1] // GD_DK):
                seg = act[:, hh * GD_DK:(hh + 1) * GD_DK]
                ss = jnp.sum(seg * seg, axis=-1, keepdims=True)
                parts.append(seg * (lax.rsqrt(ss + NORM_EPS) * scale))
            o_ref[0] = jnp.concatenate(parts, axis=1)

        @pl.when(jb >= n_qk_blk)
        def _():
            o_ref[0] = act


def _gd_proj(xs, mod, ng, w_in, conv_w, n_ctx, kd_total, qkv_total):
    b, l, d = xs.shape
    n_out = w_in.shape[1]
    tm = TOKEN_TILE
    nb = 1024
    nt = l // tm
    nct = n_ctx // tm
    nj = n_out // nb
    n_conv_blk = qkv_total // nb
    kern = functools.partial(_gd_proj_kernel, tm=tm, n_ctx=n_ctx, n_tot=l, n_qk_blk=2 * kd_total // nb,
                             n_conv_blk=n_conv_blk)
    return pl.pallas_call(
        kern,
        out_shape=jax.ShapeDtypeStruct((b, l, n_out), F32),
        grid=(nj, b, nt),
        in_specs=_tile_specs(tm, d, nt, lead=(0,))
                 + [pl.BlockSpec((1, 1, 6, d), lambda j, i, t: (i, (t >= nct).astype(jnp.int32), 0, 0)),
                    _const_spec(ng.shape),
                    pl.BlockSpec((d, nb), lambda j, i, t: (0, j)),
                    pl.BlockSpec((GD_CONV_W, nb), lambda j, i, t: (0, jnp.minimum(j, n_conv_blk - 1)))],
        out_specs=pl.BlockSpec((1, tm, nb), lambda j, i, t: (i, t, j)),
        compiler_params=_params(3),
        name="gd_proj",
    )(xs, xs, xs, mod, ng, w_in, conv_w)


def _gd_ab_kernel(x_ref, mod_ref, ng_ref, wab_ref, alog_ref, dtb_ref, o_ref, *, n_vh):
    h = _norm_mod(x_ref[0], ng_ref[...], mod_ref[0, 0, 0:1, :], mod_ref[0, 0, 1:2, :])
    ab = _dot_hh(h, wab_ref[...])
    lane = lax.broadcasted_iota(jnp.int32, (1, LANES), 1)
    for z in range(2):
        abz = ab if z == 0 else pltpu.roll(ab, LANES - 2 * n_vh, axis=1)
        gdec = -jnp.exp(alog_ref[z:z + 1, :]) * _softplus(abz + dtb_ref[z:z + 1, :])
        beta = _sigmoid(abz)
        o_ref[z, 0] = jnp.where(lane < n_vh, gdec, jnp.where(lane < 2 * n_vh, beta, 0.0))


def _gd_ab(xs, mod, ng, wab, alog, dtb, n_ctx, n_vh):
    b, l, d = xs.shape
    tm = TOKEN_TILE
    nt = l // tm
    nct = n_ctx // tm
    return pl.pallas_call(
        functools.partial(_gd_ab_kernel, n_vh=n_vh),
        out_shape=jax.ShapeDtypeStruct((2, b, l, LANES), F32),
        grid=(b, nt),
        in_specs=[pl.BlockSpec((1, tm, d), lambda i, t: (i, t, 0)),
                  pl.BlockSpec((1, 1, 6, d), lambda i, t: (i, (t >= nct).astype(jnp.int32), 0, 0)),
                  _const_spec(ng.shape), _const_spec(wab.shape), _const_spec(alog.shape), _const_spec(dtb.shape)],
        out_specs=pl.BlockSpec((2, 1, tm, LANES), lambda i, t: (0, i, t, 0)),
        compiler_params=_params(2),
        name="gd_ab",
    )(xs, mod, ng, wab, alog, dtb)


def _gd_chunk_kernel(q_ref, k_ref, v_ref, gb_ref, e64g_ref, e64b_ref, e128g_ref, e128b_ref, o_ref, s_ref, *,
                     n_vh, heads_per_group):
    c = q_ref.shape[1]
    n = heads_per_group
    rep = n_vh // (q_ref.shape[2] // GD_DK)
    direction = pl.program_id(1)
    j = pl.program_id(2)

    @pl.when(j == 0)
    def _():
        s_ref[...] = jnp.zeros_like(s_ref)

    incl, _, _ = _order_masks(direction, c)
    incl_f = jnp.where(incl, 1.0, 0.0)
    gb = gb_ref[0, 0]
    gc = _dot_ex3(incl_f, gb)
    gtot = jnp.sum(gb, axis=0, keepdims=True)

    incl_all, _, eye_all = _order_masks(direction, c, n_vh)
    gt64 = _dot_x3(gc, e64g_ref[...])
    gs64 = _dot_ex3(jnp.ones((c, c), F32), gt64 * eye_all)
    bt64 = _dot_x3(gb, e64b_ref[...])
    gam = jnp.where(incl_all, jnp.exp(jnp.where(incl_all, gt64 - gs64, 0.0)), 0.0)

    e_g = jnp.exp(_dot_x3(gc, e128g_ref[...]))
    e_end = jnp.exp(_dot_x3(gtot - gc, e128g_ref[...]))
    beta = _dot_x3(gb, e128b_ref[...])
    gl = jnp.exp(_dot_x3(jnp.broadcast_to(gtot, (HALO, gtot.shape[1])), e128g_ref[...]))[0:1]

    q = q_ref[0]
    k = k_ref[0]
    v = v_ref[0]
    gw = n * c
    bd = _block_mask(gw, gw, c, c)
    incl_t, strict_t, eye_t = _order_masks(direction, c, n)
    kh_per_group = n // rep
    bd_k = _block_mask(gw, kh_per_group * GD_DK, c, GD_DK, rdiv=rep)
    bd_v = _block_mask(gw, n * GD_DV, c, GD_DV)

    for gi in range(n_vh // n):
        ksl = slice(gi * kh_per_group * GD_DK, (gi + 1) * kh_per_group * GD_DK)
        kg_ = k[:, ksl]
        qg_ = q[:, ksl]
        kexp = jnp.where(bd_k, _tile_rows(kg_, n), 0.0)
        qkk = _dot_nt(jnp.concatenate([kg_, qg_], axis=0), kexp)
        csl = slice(gi * gw, (gi + 1) * gw)
        a_mat = jnp.where(strict_t, qkk[:c] * gam[:, csl] * bt64[:, csl], 0.0)
        aqk = jnp.where(incl_t, qkk[c:] * gam[:, csl], 0.0)
        t_inv = _tri_inv(-a_mat, bd, eye_t)

        vsl = slice(gi * n * GD_DV, (gi + 1) * n * GD_DV)
        k2 = jnp.concatenate([kg_[:, (hh // rep) * GD_DK:(hh // rep + 1) * GD_DK] for hh in range(n)], axis=1)
        q2 = jnp.concatenate([qg_[:, (hh // rep) * GD_DK:(hh // rep + 1) * GD_DK] for hh in range(n)], axis=1)
        bg = beta[:, vsl]
        eg = e_g[:, vsl]
        kb = k2 * bg
        u = _dot(t_inv, jnp.where(bd_v, _tile_rows(v[:, vsl] * bg, n), 0.0))
        w = _dot(t_inv, jnp.where(bd_v, _tile_rows(kb * eg, n), 0.0))
        qe = q2 * eg
        ke = k2 * e_end[:, vsl]

        vnew = []
        pre = []
        for hh in range(n):
            hs = slice(hh * GD_DV, (hh + 1) * GD_DV)
            s = s_ref[gi * n + hh]
            ws = _dot(jnp.concatenate([w[:, hs], qe[:, hs]], axis=0), s)
            vnew.append(u[:, hs] - ws[:c])
            pre.append(ws[c:])
        vn = jnp.concatenate(vnew, axis=1)
        o_ref[0, 0, :, vsl] = jnp.concatenate(pre, axis=1) + _dot(aqk, jnp.where(bd_v, _tile_rows(vn, n), 0.0))
        for hh in range(n):
            hs = slice(hh * GD_DV, (hh + 1) * GD_DV)
            hv = gi * n + hh
            s_ref[hv] = s_ref[hv] * gl[:, hv * GD_DV:(hv + 1) * GD_DV] + _dot_tn(ke[:, hs], vn[:, hs])


def _gd_chunk(proj, gb, consts, n_ctx, kd_total, vd_total):
    b, l, _ = proj.shape
    c = CHUNK
    nc = l // c
    ncc = n_ctx // c
    n_vh = vd_total // GD_DV
    qblk = kd_total
    e64g, e64b, e128g, e128b = consts

    def col(blk):
        return lambda i, z, j: (i, _chunk_index(z, j, ncc, nc), blk)

    def perdir(i, z, j):
        return (z, i, _chunk_index(z, j, ncc, nc), 0)

    return pl.pallas_call(
        functools.partial(_gd_chunk_kernel, n_vh=n_vh, heads_per_group=4),
        out_shape=jax.ShapeDtypeStruct((2, b, l, vd_total), F32),
        grid=(b, 2, nc),
        in_specs=[pl.BlockSpec((1, c, qblk), col(0)),
                  pl.BlockSpec((1, c, qblk), col(1)),
                  pl.BlockSpec((1, c, vd_total), col(2 * kd_total // vd_total)),
                  pl.BlockSpec((1, 1, c, LANES), perdir)]
                 + [_const_spec(a.shape) for a in consts],
        out_specs=pl.BlockSpec((1, 1, c, vd_total), perdir),
        scratch_shapes=[pltpu.VMEM((n_vh, GD_DK, GD_DV), F32)],
        compiler_params=_params(3),
        name="gd_chunk",
    )(proj, proj, proj, gb, e64g, e64b, e128g, e128b)


def _gd_out_kernel(x_ref, o_ref, z_ref, mod_ref, ngd_ref, wo_ref, xo_ref):
    o = o_ref[0, 0] + o_ref[1, 0]
    z = z_ref[0]
    parts = []
    for hh in range(o.shape[1] // GD_DV):
        seg = o[:, hh * GD_DV:(hh + 1) * GD_DV]
        ms = jnp.mean(seg * seg, axis=-1, keepdims=True)
        parts.append(seg * lax.rsqrt(ms + NORM_EPS) * ngd_ref[...])
    on = jnp.concatenate(parts, axis=1) * _silu(z)
    xo_ref[0] = x_ref[0] + mod_ref[0, 0, 2:3, :] * _dot(on, wo_ref[...])


def _gd_out(xs, o, proj, mod, ngd, w_o, n_ctx, z_blk):
    b, l, d = xs.shape
    vd = o.shape[-1]
    tm = TOKEN_TILE
    nct = n_ctx // tm
    nt = (l - n_ctx) // tm
    return pl.pallas_call(
        _gd_out_kernel,
        out_shape=jax.ShapeDtypeStruct((b, l - n_ctx, d), F32),
        grid=(b, nt),
        in_specs=[pl.BlockSpec((1, tm, d), lambda i, t: (i, t + nct, 0)),
                  pl.BlockSpec((2, 1, tm, vd), lambda i, t: (0, i, t + nct, 0)),
                  pl.BlockSpec((1, tm, vd), lambda i, t: (i, t + nct, z_blk)),
                  pl.BlockSpec((1, 1, 6, d), lambda i, t: (i, 1, 0, 0)),
                  _const_spec(ngd.shape), _const_spec(w_o.shape)],
        out_specs=pl.BlockSpec((1, tm, d), lambda i, t: (i, t, 0)),
        compiler_params=_params(2),
        name="gd_out",
    )(xs, o, proj, mod, ngd, w_o)


def _pos_embed_2d(rows, d):
    quarter = d // 4
    omega = 1.0 / (POS_BASE ** (jnp.arange(quarter, dtype=F32) / quarter))

    def axis_emb(n):
        ang = jnp.arange(n, dtype=F32)[:, None] * omega[None, :]
        return jnp.concatenate([jnp.sin(ang), jnp.cos(ang)], axis=-1)

    e_row = jnp.broadcast_to(axis_emb(rows)[:, None, :], (rows, GRID_W, d // 2))
    e_col = jnp.broadcast_to(axis_emb(GRID_W)[None, :, :], (rows, GRID_W, d // 2))
    return jnp.concatenate([e_row, e_col], axis=-1).reshape(rows * GRID_W, d)


def _head_indicator(d, head):
    ch = jnp.arange(d)[:, None] // head
    ind = (ch == jnp.arange(LANES)[None, :]).astype(BF16)
    return ind, ind.T


def _expand_matrix(offset, n_heads, width):
    src = jnp.arange(LANES)[:, None]
    dst = jnp.arange(n_heads * width)[None, :] // width
    return (src == dst + offset).astype(BF16)


def _block_diag2(m):
    z = jnp.zeros_like(m[0])
    return jnp.concatenate([jnp.concatenate([m[0], z], axis=1), jnp.concatenate([z, m[1]], axis=1)], axis=0)


def _moe_params(i, moe_w_rg, moe_b_rg, moe_w_re, moe_b_re, moe_w_gate, moe_w_up, moe_w_down):
    d = moe_w_rg.shape[1]
    ne = MOE_GROUPS * MOE_EXPERTS
    pad = LANES - ne - MOE_GROUPS
    w_r = jnp.concatenate([moe_w_re[i], moe_w_rg[i], jnp.zeros((d, pad), F32)], axis=1)
    b_r = jnp.concatenate([moe_b_re[i], moe_b_rg[i], jnp.zeros((pad,), F32)])[None, :]
    ff = moe_w_gate.shape[-1]
    return {"w_r": w_r, "b_r": b_r,
            "w_gate": _bf(moe_w_gate[i]).reshape(ne, d, ff),
            "w_up": _bf(moe_w_up[i]).reshape(ne, d, ff),
            "w_down": _bf(moe_w_down[i]).reshape(ne, ff, d)}


def kernel(x, c, ctx, c_ctx, ada_w, ada_b, norm1_g, norm2_g, rw_mix, rw_w_rkv, rw_w0, rw_w1, rw_w2, rw_a0, rw_a1, rw_a2, rw_g1, rw_g2, rw_k_k, rw_k_a, rw_r_k, rw_ln_g, rw_ln_b, rw_w_o, gd_w_in, gd_conv, gd_w_ab, gd_a_log, gd_dt_bias, gd_norm_g, gd_w_o, moe_w_rg, moe_b_rg, moe_w_re, moe_b_re, moe_w_gate, moe_w_up, moe_w_down, final_g):
    bsz, n_lat, d = x.shape
    n_ctx = ctx.shape[1]
    assert n_ctx % TOKEN_TILE == 0 and n_lat % TOKEN_TILE == 0 and n_ctx % CHUNK == 0
    assert ada_w.shape[0] == 2 and d % (4 * RW_HEAD) == 0

    rows = -(-(bsz + 1) // HALO) * HALO
    cs = jnp.zeros((rows, d), F32).at[:bsz].set(c).at[bsz].set(c_ctx)
    mod_all = _ada(cs, ada_w, ada_b)

    def mod_of(i):
        lat = mod_all[i, :bsz].reshape(bsz, 1, 6, d)
        cx = jnp.broadcast_to(mod_all[i, bsz].reshape(1, 1, 6, d), (bsz, 1, 6, d))
        return jnp.concatenate([cx, lat], axis=1)

    xs = jnp.concatenate([ctx, x], axis=1)
    pos = jnp.concatenate([jnp.zeros((n_ctx, d), F32), _pos_embed_2d(n_lat // GRID_W, d)], axis=0)
    gsum, gbc = _head_indicator(d, RW_HEAD)

    mod0 = mod_of(0)
    ng1 = norm1_g[0][None, :]
    rw = {"mix": rw_mix[0], "w_rkv": _bf(rw_w_rkv[0]), "w0": rw_w0[0],
          "w1": _bf(jnp.concatenate([rw_w1[0, 0], rw_w1[0, 1]], axis=1)), "w2": _bf(_block_diag2(rw_w2[0])),
          "a0": rw_a0[0],
          "a1": _bf(jnp.concatenate([rw_a1[0, 0], rw_a1[0, 1]], axis=1)), "a2": _bf(_block_diag2(rw_a2[0])),
          "g1": _bf(rw_g1[0]), "g2": _bf(rw_g2[0]), "k_k": rw_k_k[0][None, :], "k_a": rw_k_a[0][None, :],
          "r_k": rw_r_k[0].reshape(1, d), "ln_g": rw_ln_g[0][None, :], "ln_b": rw_ln_b[0][None, :],
          "w_o": _bf(rw_w_o[0])}
    r, v, kk, gate, lw, kd, ar = _rw_feat(xs, pos, mod0, ng1, rw, gsum, gbc, n_ctx)
    y = _rw_scan(r, v, kk, lw, kd, ar, n_ctx)
    xs = _rw_out(xs, pos, y, r, v, kd, gate, mod0, rw, gsum, gbc, n_ctx)
    moe_args = (moe_w_rg, moe_b_rg, moe_w_re, moe_b_re, moe_w_gate, moe_w_up, moe_w_down)
    fg = final_g[None, :]
    xs = _moe(xs, mod0, norm2_g[0][None, :], _moe_params(0, *moe_args), fg, n_ctx, False)

    mod1 = mod_of(1)
    n_vh = gd_a_log.shape[-1]
    vd_total = n_vh * GD_DV
    kd_total = (gd_w_in.shape[-1] - 2 * vd_total) // 2
    qkv_total = 2 * kd_total + vd_total
    ng1 = norm1_g[1][None, :]
    proj = _gd_proj(xs, mod1, ng1, _bf(gd_w_in[0]), gd_conv[0], n_ctx, kd_total, qkv_total)
    wab = jnp.concatenate([gd_w_ab[0, 0], gd_w_ab[0, 1], jnp.zeros((d, LANES - 4 * n_vh), F32)], axis=1)
    lane_pad = jnp.zeros((2, LANES - n_vh), F32)
    alog = jnp.concatenate([gd_a_log[0], lane_pad], axis=1)
    dtb = jnp.concatenate([gd_dt_bias[0], lane_pad], axis=1)
    gb = _gd_ab(xs, mod1, ng1, wab, alog, dtb, n_ctx, n_vh)
    consts = (_expand_matrix(0, n_vh, CHUNK), _expand_matrix(n_vh, n_vh, CHUNK),
              _expand_matrix(0, n_vh, GD_DV), _expand_matrix(n_vh, n_vh, GD_DV))
    o = _gd_chunk(proj, gb, consts, n_ctx, kd_total, vd_total)
    ngd = gd_norm_g[0][None, :]
    x_lat = _gd_out(xs, o, proj, mod1, ngd, _bf(gd_w_o[0]), n_ctx, qkv_total // vd_total)
    return _moe(x_lat, mod1, norm2_g[1][None, :], _moe_params(1, *moe_args), fg, 0, True)
```

```python
import functools
import math

import jax
import jax.numpy as jnp
from jax import lax
from jax.experimental import pallas as pl
from jax.experimental.pallas import tpu as pltpu

F32 = jnp.float32
BF16 = jnp.bfloat16

NORM_EPS = 1e-6
RW_GN_EPS = 64e-5
POS_BASE = 10000.0
GRID_W = 64
RW_HEAD = 64
RW_LORA = 64
GD_DK = 128
GD_DV = 128
GD_CONV_W = 5
MOE_GROUPS = 4
MOE_EXPERTS = 8
CHUNK = 64
LANES = 128
HALO = 8
TOKEN_TILE = 256
VMEM_LIMIT = 56 * 1024 * 1024
NEG_BIG = -1e30


def _bf(x):
    return x.astype(BF16)


def _dot(a, b):
    return jnp.dot(_bf(a), _bf(b), preferred_element_type=F32)


def _dot_nt(a, b):
    return lax.dot_general(_bf(a), _bf(b), (((1,), (1,)), ((), ())), preferred_element_type=F32)


def _dot_tn(a, b):
    return lax.dot_general(_bf(a), _bf(b), (((0,), (0,)), ((), ())), preferred_element_type=F32)


def _split2(x):
    hi = x.astype(BF16)
    lo = (x - hi.astype(F32)).astype(BF16)
    return hi, lo


def _split3(x):
    hi = x.astype(BF16)
    r1 = x - hi.astype(F32)
    mid = r1.astype(BF16)
    lo = (r1 - mid.astype(F32)).astype(BF16)
    return hi, mid, lo


def _dot_x2(a, b_exact):
    hi, lo = _split2(a)
    b = _bf(b_exact)
    return (jnp.dot(hi, b, preferred_element_type=F32) + jnp.dot(lo, b, preferred_element_type=F32))


def _dot_x3(a, b_exact):
    hi, mid, lo = _split3(a)
    b = _bf(b_exact)
    return (jnp.dot(hi, b, preferred_element_type=F32) + jnp.dot(mid, b, preferred_element_type=F32)
            + jnp.dot(lo, b, preferred_element_type=F32))


def _dot_ex3(a_exact, b):
    hi, mid, lo = _split3(b)
    a = _bf(a_exact)
    return (jnp.dot(a, hi, preferred_element_type=F32) + jnp.dot(a, mid, preferred_element_type=F32)
            + jnp.dot(a, lo, preferred_element_type=F32))


def _dot_hh(a, b):
    ah, al = _split2(a)
    bh, bl = _split2(b)
    return (jnp.dot(ah, bh, preferred_element_type=F32) + jnp.dot(al, bh, preferred_element_type=F32)
            + jnp.dot(ah, bl, preferred_element_type=F32))


def _sigmoid(x):
    return 1.0 / (1.0 + jnp.exp(-x))


def _silu(x):
    return x * _sigmoid(x)


def _softplus(x):
    return jnp.maximum(x, 0.0) + jnp.log(1.0 + jnp.exp(-jnp.abs(x)))


def _norm_mod(x, g, shift, scale):
    ms = jnp.mean(x * x, axis=-1, keepdims=True)
    return (x * lax.rsqrt(ms + NORM_EPS) * g) * (1.0 + scale) + shift


def _tile_rows(x, n):
    return jnp.concatenate([x] * n, axis=0)


def _tile_lanes(x, n):
    return jnp.concatenate([x] * n, axis=1)


def _order_masks(direction, c, n=1):
    row = lax.broadcasted_iota(jnp.int32, (c, n * c), 0)
    col = lax.broadcasted_iota(jnp.int32, (c, n * c), 1) % c
    diff = (col - row) * (1 - 2 * direction)
    return diff <= 0, diff < 0, jnp.where(diff == 0, 1.0, 0.0)


def _block_mask(rows, cols, rblk, cblk, rdiv=1):
    r = lax.broadcasted_iota(jnp.int32, (rows, cols), 0) // rblk
    c = lax.broadcasted_iota(jnp.int32, (rows, cols), 1) // cblk
    return (r // rdiv) == c if rdiv != 1 else r == c


TRI_BASE = 4


def _tri_inv(a_list, bd, eye_list):
    c = a_list[0].shape[0]
    n = a_list[0].shape[1] // c
    row = lax.broadcasted_iota(jnp.int32, a_list[0].shape, 0)
    col = lax.broadcasted_iota(jnp.int32, a_list[0].shape, 1) % c

    def same_block(m):
        return (row // m) == (col // m)

    def mm(x, y):
        return jnp.dot(_bf(x), _bf(jnp.where(bd, _tile_rows(y, n), 0.0)), preferred_element_type=F32)

    base = same_block(TRI_BASE)
    a0 = [jnp.where(base, a, 0.0) for a in a_list]
    x = [e + a for e, a in zip(eye_list, a0)]
    p = [mm(a, a) for a in a0]
    base_levels = int(math.log2(TRI_BASE)) - 1
    for lvl in range(base_levels):
        x = [xi + mm(pi, xi) for xi, pi in zip(x, p)]
        if lvl + 1 < base_levels:
            p = [mm(pi, pi) for pi in p]
    m = TRI_BASE
    while m < c:
        off = jnp.logical_and(same_block(2 * m), jnp.logical_not(same_block(m)))
        t = [mm(jnp.where(off, a, 0.0), xi) for a, xi in zip(a_list, x)]
        x = [xi + mm(xi, ti) for xi, ti in zip(x, t)]
        m *= 2
    return x


def _chunk_index(direction, j, n_ctx_chunks, n_chunks):
    bwd = jnp.where(j < n_ctx_chunks, n_ctx_chunks - 1 - j, n_chunks + n_ctx_chunks - 1 - j)
    return jnp.where(direction == 0, j, bwd)


def _const_spec(shape):
    nd = len(shape)
    return pl.BlockSpec(shape, lambda *_: (0,) * nd, pipeline_mode=pl.Buffered(1))


def _params(n_axes):
    return pltpu.CompilerParams(dimension_semantics=("arbitrary",) * n_axes, vmem_limit_bytes=VMEM_LIMIT)


def _ada_kernel(cs_ref, w_ref, b_ref, o_ref):
    s = _silu(cs_ref[...])
    o_ref[0] = _dot_hh(s, w_ref[0]) + b_ref[0]


def _ada(cs, ada_w, ada_b):
    n_layers, d, n6 = ada_w.shape
    rows = cs.shape[0]
    tn = 1536
    return pl.pallas_call(
        _ada_kernel,
        out_shape=jax.ShapeDtypeStruct((n_layers, rows, n6), F32),
        grid=(n_layers, n6 // tn),
        in_specs=[pl.BlockSpec((rows, d), lambda i, j: (0, 0)),
                  pl.BlockSpec((1, d, tn), lambda i, j: (i, 0, j)),
                  pl.BlockSpec((1, 1, tn), lambda i, j: (i, 0, j))],
        out_specs=pl.BlockSpec((1, rows, tn), lambda i, j: (i, 0, j)),
        compiler_params=_params(2),
        name="ada",
    )(cs, ada_w, ada_b.reshape(n_layers, 1, n6))


def _shifted_neighbours(h, h_prev, h_next, pos, tm, n_ctx, n_tot):
    prev_ok = jnp.logical_and(pos != 0, pos != n_ctx)
    next_ok = jnp.logical_and(pos + tm != n_ctx, pos + tm != n_tot)
    h_prev = jnp.where(prev_ok, h_prev, 0.0)
    h_next = jnp.where(next_ok, h_next, 0.0)
    row = lax.broadcasted_iota(jnp.int32, (tm, 1), 0)
    prev = jnp.where(row == 0, h_prev, pltpu.roll(h, 1, axis=0))
    nxt = jnp.where(row == tm - 1, h_next, pltpu.roll(h, tm - 1, axis=0))
    return prev, nxt


def _tile_specs(tm, d, n_tiles, lead=()):
    nl = len(lead)
    hb = tm // HALO
    last = n_tiles * hb - 1

    def cur(*g):
        return (g[nl], g[nl + 1], 0)

    def prv(*g):
        return (g[nl], jnp.maximum(g[nl + 1] * hb - 1, 0), 0)

    def nxt(*g):
        return (g[nl], jnp.minimum((g[nl + 1] + 1) * hb, last), 0)

    return [pl.BlockSpec((1, tm, d), cur), pl.BlockSpec((1, HALO, d), prv), pl.BlockSpec((1, HALO, d), nxt)]


def _rw_feat_kernel(x_ref, xp_ref, xn_ref, pos_ref, posp_ref, posn_ref, mod_ref, ng_ref, mix_ref,
                    wrkv_ref, w0_ref, w1_ref, w2_ref, a0_ref, a1_ref, a2_ref, g1_ref, g2_ref,
                    kkp_ref, kap_ref, gsum_ref, gbc_ref,
                    r_o, v_o, kk_o, gate_o, lw_o, kd_o, a_o, *, tm, n_ctx, n_tot):
    d = x_ref.shape[-1]
    pos = pl.program_id(1) * tm
    sh = mod_ref[0, 0, 0:1, :]
    sc = mod_ref[0, 0, 1:2, :]
    g = ng_ref[...]
    h = _norm_mod(x_ref[0] + pos_ref[...], g, sh, sc)
    hp = _norm_mod(xp_ref[0, HALO - 1:HALO, :] + posp_ref[HALO - 1:HALO, :], g, sh, sc)
    hn = _norm_mod(xn_ref[0, 0:1, :] + posn_ref[0:1, :], g, sh, sc)
    prev, nxt = _shifted_neighbours(h, hp, hn, pos, tm, n_ctx, n_tot)
    xx = 0.5 * (prev + nxt) - h
    mix = mix_ref[...]
    xr = h + xx * mix[0:1]
    xw = h + xx * mix[1:2]
    xk = h + xx * mix[2:3]
    xv = h + xx * mix[3:4]
    xa = h + xx * mix[4:5]
    xg = h + xx * mix[5:6]

    r_o[0] = _dot(xr, wrkv_ref[0])
    v_o[0] = _dot(xv, wrkv_ref[2])
    gate_o[0] = _dot(_sigmoid(_dot(xg, g1_ref[...])), g2_ref[...])
    k = _dot(xk, wrkv_ref[1])

    kq = k * kkp_ref[...]
    ssq = _dot_x2(kq * kq, gsum_ref[...])
    kk_o[0] = kq * _dot_x2(lax.rsqrt(ssq + NORM_EPS), gbc_ref[...])

    wl = _dot(jnp.tanh(_dot(xw, w1_ref[...])), w2_ref[...])
    al = _dot(_dot(xa, a1_ref[...]), a2_ref[...])
    ka = kap_ref[...]
    for z in range(2):
        zw = w0_ref[z:z + 1, :] + wl[:, z * d:(z + 1) * d]
        lw_o[z, 0] = (-math.exp(-0.5)) * _sigmoid(zw)
        rate = _sigmoid(a0_ref[z:z + 1, :] + al[:, z * d:(z + 1) * d])
        a_o[z, 0] = rate
        kd_o[z, 0] = k * (1.0 + (rate - 1.0) * ka)


def _rw_feat(xs, pos, mod, ng, p, gsum, gbc, n_ctx):
    b, l, d = xs.shape
    tm = TOKEN_TILE
    nt = l // tm
    nct = n_ctx // tm
    hb = tm // HALO
    kern = functools.partial(_rw_feat_kernel, tm=tm, n_ctx=n_ctx, n_tot=l)
    tok = pl.BlockSpec((1, tm, d), lambda i, t: (i, t, 0))
    tok2 = pl.BlockSpec((2, 1, tm, d), lambda i, t: (0, i, t, 0))
    pos_specs = [pl.BlockSpec((tm, d), lambda i, t: (t, 0)),
                 pl.BlockSpec((HALO, d), lambda i, t: (jnp.maximum(t * hb - 1, 0), 0)),
                 pl.BlockSpec((HALO, d), lambda i, t: (jnp.minimum((t + 1) * hb, nt * hb - 1), 0))]
    in_specs = (_tile_specs(tm, d, nt) + pos_specs
                + [pl.BlockSpec((1, 1, 6, d), lambda i, t: (i, (t >= nct).astype(jnp.int32), 0, 0))]
                + [_const_spec(a.shape) for a in (ng, p["mix"], p["w_rkv"], p["w0"], p["w1"], p["w2"], p["a0"],
                                                  p["a1"], p["a2"], p["g1"], p["g2"], p["k_k"], p["k_a"], gsum, gbc)])
    sd = jax.ShapeDtypeStruct((b, l, d), F32)
    sd2 = jax.ShapeDtypeStruct((2, b, l, d), F32)
    return pl.pallas_call(
        kern,
        out_shape=(sd, sd, sd, sd, sd2, sd2, sd2),
        grid=(b, nt),
        in_specs=in_specs,
        out_specs=(tok, tok, tok, tok, tok2, tok2, tok2),
        compiler_params=_params(2),
        name="rw_feat",
    )(xs, xs, xs, pos, pos, pos, mod, ng, p["mix"], p["w_rkv"], p["w0"], p["w1"], p["w2"], p["a0"], p["a1"], p["a2"],
      p["g1"], p["g2"], p["k_k"], p["k_a"], gsum, gbc)


def _rw_scan_kernel(r0_ref, v0_ref, kk0_ref, lw0_ref, kd0_ref, a0_ref,
                    r1_ref, v1_ref, kk1_ref, lw1_ref, kd1_ref, a1_ref,
                    y0_ref, y1_ref, s_ref, *, heads_per_group):
    c = r0_ref.shape[1]
    d = r0_ref.shape[2]
    n = heads_per_group
    gw = n * RW_HEAD
    n_groups = d // gw
    nt = (((1,), (1,)), ((), ()))

    @pl.when(pl.program_id(1) == 0)
    def _():
        s_ref[...] = jnp.zeros_like(s_ref)

    bd = _block_mask(n * c, gw, c, RW_HEAD)

    def bdiag(x):
        return jnp.where(bd, _tile_rows(x, n), 0.0)

    refs = ((r0_ref, v0_ref, kk0_ref, lw0_ref, kd0_ref, a0_ref, y0_ref),
            (r1_ref, v1_ref, kk1_ref, lw1_ref, kd1_ref, a1_ref, y1_ref))
    prep = []
    for z, (r_ref, v_ref, kk_ref, lw_ref, kd_ref, a_ref, _) in enumerate(refs):
        incl, _, _ = _order_masks(z, c)
        lw = lw_ref[0, 0]
        lam = _dot_ex3(jnp.where(incl, 1.0, 0.0), lw)
        tot = jnp.sum(lw, axis=0, keepdims=True)
        w_inv = jnp.exp(-lam)
        w_end = jnp.exp(tot - lam)
        kk = kk_ref[0]
        kd = kd_ref[0, 0]
        bb = kk * a_ref[0, 0]
        prep.append({"a_t": -(kk * jnp.exp(lam - lw)), "r_t": r_ref[0] * jnp.exp(lam), "b_t": bb * w_inv,
                     "k_t": kd * w_inv, "b_e": bb * w_end, "k_e": kd * w_end, "v": v_ref[0],
                     "w_tot": jnp.exp(tot), "masks": _order_masks(z, c, n)})

    chains = [(z, gi) for z in range(2) for gi in range(n_groups)]

    def sl(gi):
        return slice(gi * gw, (gi + 1) * gw)

    lhs = [_bf(jnp.concatenate([prep[z]["a_t"][:, sl(gi)], prep[z]["r_t"][:, sl(gi)]], axis=0)) for z, gi in chains]
    ab = [lax.dot_general(lh, _bf(bdiag(prep[z]["b_t"][:, sl(gi)])), nt, preferred_element_type=F32)
          for lh, (z, gi) in zip(lhs, chains)]
    ak = [lax.dot_general(lh, _bf(bdiag(prep[z]["k_t"][:, sl(gi)])), nt, preferred_element_type=F32)
          for lh, (z, gi) in zip(lhs, chains)]
    a_ab = [jnp.where(prep[z]["masks"][1], x[:c], 0.0) for x, (z, _) in zip(ab, chains)]
    a_rb = [jnp.where(prep[z]["masks"][0], x[c:], 0.0) for x, (z, _) in zip(ab, chains)]
    a_kk = [jnp.concatenate([jnp.where(prep[z]["masks"][1], x[:c], 0.0), jnp.where(prep[z]["masks"][0], x[c:], 0.0)],
                            axis=0) for x, (z, _) in zip(ak, chains)]
    st = [s_ref[z, gi] for z, gi in chains]
    pre = [lax.dot_general(lh, _bf(s), nt, preferred_element_type=F32) for lh, s in zip(lhs, st)]
    vg = [prep[z]["v"][:, sl(gi)] for z, gi in chains]
    av = [_dot(x, bdiag(v)) for x, v in zip(a_kk, vg)]
    t_inv = _tri_inv(a_ab, bd, [prep[z]["masks"][2] for z, _ in chains])
    u = [_dot(t, bdiag(p[:c] + q[:c])) for t, p, q in zip(t_inv, pre, av)]
    yv = [p[c:] + q[c:] + _dot(x, bdiag(ui)) for p, q, x, ui in zip(pre, av, a_rb, u)]
    upd = [_dot_tn(jnp.concatenate([ui, v], axis=0),
                   jnp.concatenate([prep[z]["b_e"][:, sl(gi)], prep[z]["k_e"][:, sl(gi)]], axis=0))
           for ui, v, (z, gi) in zip(u, vg, chains)]
    for (z, gi), y_val, s, up in zip(chains, yv, st, upd):
        refs[z][6][0, :, sl(gi)] = y_val
        s_ref[z, gi] = s * prep[z]["w_tot"][:, sl(gi)] + jnp.where(bd, up, 0.0)


def _rw_scan(r, v, kk, lw, kd, a, n_ctx):
    b, l, d = r.shape
    c = CHUNK
    nc = l // c
    ncc = n_ctx // c
    hpg = 4
    gw = hpg * RW_HEAD

    def shared(z):
        return pl.BlockSpec((1, c, d), lambda i, j: (i, _chunk_index(z, j, ncc, nc), 0))

    def perdir(z):
        return pl.BlockSpec((1, 1, c, d), lambda i, j: (z, i, _chunk_index(z, j, ncc, nc), 0))

    sd = jax.ShapeDtypeStruct((b, l, d), F32)
    return pl.pallas_call(
        functools.partial(_rw_scan_kernel, heads_per_group=hpg),
        out_shape=(sd, sd),
        grid=(b, nc),
        in_specs=[shared(0)] * 3 + [perdir(0)] * 3 + [shared(1)] * 3 + [perdir(1)] * 3,
        out_specs=(shared(0), shared(1)),
        scratch_shapes=[pltpu.VMEM((2, d // gw, gw, gw), F32)],
        compiler_params=_params(2),
        name="rw_scan",
    )(r, v, kk, lw, kd, a, r, v, kk, lw, kd, a)


def _rw_out_kernel(x_ref, pos_ref, y0_ref, y1_ref, r_ref, v_ref, kd_ref, gate_ref, mod_ref, rk_ref, lng_ref, lnb_ref,
                   wo_ref, gsum_ref, gbc_ref, xo_ref):
    inv_n = 1.0 / RW_HEAD
    gsum = gsum_ref[...]
    gbc = gbc_ref[...]
    y = y0_ref[0] + y1_ref[0]
    mu = _dot_x2(_dot_x2(y, gsum) * inv_n, gbc)
    yc = y - mu
    var = _dot_x2(yc * yc, gsum) * inv_n
    yn = yc * _dot_x2(lax.rsqrt(var + RW_GN_EPS), gbc) * lng_ref[...] + lnb_ref[...]
    k_bonus = 0.5 * (kd_ref[0, 0] + kd_ref[1, 0])
    bonus = _dot_x2(_dot_x2(r_ref[0] * k_bonus * rk_ref[...], gsum), gbc) * v_ref[0]
    out = (yn + bonus) * gate_ref[0]
    o = _dot(out, wo_ref[...])
    xo_ref[0] = x_ref[0] + pos_ref[...] + mod_ref[0, 0, 2:3, :] * o


def _rw_out(xs, pos, y, r, v, kd, gate, mod, p, gsum, gbc, n_ctx):
    b, l, d = xs.shape
    tm = TOKEN_TILE
    nt = l // tm
    nct = n_ctx // tm
    tok = pl.BlockSpec((1, tm, d), lambda i, t: (i, t, 0))
    tok2 = pl.BlockSpec((2, 1, tm, d), lambda i, t: (0, i, t, 0))
    consts = (p["r_k"], p["ln_g"], p["ln_b"], p["w_o"], gsum, gbc)
    return pl.pallas_call(
        _rw_out_kernel,
        out_shape=jax.ShapeDtypeStruct((b, l, d), F32),
        grid=(b, nt),
        in_specs=[tok, pl.BlockSpec((tm, d), lambda i, t: (t, 0)), tok, tok, tok, tok, tok2, tok,
                  pl.BlockSpec((1, 1, 6, d), lambda i, t: (i, (t >= nct).astype(jnp.int32), 0, 0))]
                 + [_const_spec(a.shape) for a in consts],
        out_specs=tok,
        compiler_params=_params(2),
        name="rw_out",
    )(xs, pos, y[0], y[1], r, v, kd, gate, mod, *consts)


def _route(logits):
    ne = MOE_GROUPS * MOE_EXPERTS
    lane = lax.broadcasted_iota(jnp.int32, logits.shape, 1).astype(F32)
    far = float(4 * LANES)
    g_mask = jnp.logical_and(lane >= ne, lane < ne + MOE_GROUPS)
    gl = jnp.where(g_mask, logits, NEG_BIG)
    gmax = jnp.max(gl, axis=1, keepdims=True)
    gsum = jnp.sum(jnp.where(g_mask, jnp.exp(gl - gmax), 0.0), axis=1, keepdims=True)
    gp = 1.0 / gsum
    g_first = jnp.min(jnp.where(gl == gmax, lane, far), axis=1, keepdims=True) - ne
    in_grp = jnp.logical_and(lane >= g_first * MOE_EXPERTS, lane < (g_first + 1.0) * MOE_EXPERTS)
    el = jnp.where(in_grp, logits, NEG_BIG)
    m1 = jnp.max(el, axis=1, keepdims=True)
    i1 = jnp.min(jnp.where(el == m1, lane, far), axis=1, keepdims=True)
    el2 = jnp.where(lane == i1, NEG_BIG, el)
    m2 = jnp.max(el2, axis=1, keepdims=True)
    i2 = jnp.min(jnp.where(el2 == m2, lane, far), axis=1, keepdims=True)
    e2 = jnp.exp(m2 - m1)
    p1 = gp / (1.0 + e2)
    p2 = p1 * e2
    return jnp.where(lane == i1, p1, jnp.where(lane == i2, p2, 0.0))


def _moe_kernel(x_ref, mod_ref, ng_ref, wr_ref, br_ref, wg_ref, wu_ref, wd_ref, fg_ref, xo_ref,
                h_s, comb_s, *, n_ctx, rb, final_norm):
    rows = x_ref.shape[1]
    e = pl.program_id(1)
    n_exp = pl.num_programs(1)
    n_blk = rows // rb
    n_ctx_blk = n_ctx // rb

    def seg_of(i):
        return (i >= n_ctx_blk).astype(jnp.int32) if n_ctx_blk else 1

    @pl.when(e == 0)
    def _():
        def body(i, carry):
            sl = pl.ds(pl.multiple_of(i * rb, rb), rb)
            seg = seg_of(i)
            sh = mod_ref[0, pl.ds(seg, 1), 3, :]
            sc = mod_ref[0, pl.ds(seg, 1), 4, :]
            h = _norm_mod(x_ref[0, sl, :], ng_ref[...], sh, sc)
            h_s[sl, :] = _bf(h)
            comb_s[sl, :] = _route(_dot_hh(h, wr_ref[...]) + br_ref[...])
            xo_ref[0, sl, :] = jnp.zeros((rb, xo_ref.shape[2]), F32)
            return carry
        lax.fori_loop(0, n_blk, body, 0)

    def expert(i, carry):
        sl = pl.ds(pl.multiple_of(i * rb, rb), rb)
        hb = h_s[sl, :]
        hg = jnp.dot(hb, wg_ref[0], preferred_element_type=F32)
        hu = jnp.dot(hb, wu_ref[0], preferred_element_type=F32)
        comb = comb_s[sl, :]
        lane = lax.broadcasted_iota(jnp.int32, comb.shape, 1)
        ce = jnp.sum(jnp.where(lane == e, comb, 0.0), axis=1, keepdims=True)
        act = _silu(hg) * hu * ce
        xo_ref[0, sl, :] += jnp.dot(_bf(act), wd_ref[0], preferred_element_type=F32)
        return carry
    lax.fori_loop(0, n_blk, expert, 0)

    @pl.when(e == n_exp - 1)
    def _():
        def body(i, carry):
            sl = pl.ds(pl.multiple_of(i * rb, rb), rb)
            seg = seg_of(i)
            g2 = mod_ref[0, pl.ds(seg, 1), 5, :]
            xn = x_ref[0, sl, :] + g2 * xo_ref[0, sl, :]
            if final_norm:
                ms = jnp.mean(xn * xn, axis=-1, keepdims=True)
                xn = xn * lax.rsqrt(ms + NORM_EPS) * fg_ref[...]
            xo_ref[0, sl, :] = xn
            return carry
        lax.fori_loop(0, n_blk, body, 0)


def _moe(xs, mod, ng, p, final_g, n_ctx, final_norm):
    b, rows, d = xs.shape
    ne, _, ff = p["w_gate"].shape
    rb = TOKEN_TILE
    kern = functools.partial(_moe_kernel, n_ctx=n_ctx, rb=rb, final_norm=final_norm)
    full = pl.BlockSpec((1, rows, d), lambda i, e: (i, 0, 0))
    full_in = pl.BlockSpec((1, rows, d), lambda i, e: (i, 0, 0), pipeline_mode=pl.Buffered(1))
    return pl.pallas_call(
        kern,
        out_shape=jax.ShapeDtypeStruct((b, rows, d), F32),
        grid=(b, ne),
        in_specs=[full_in, pl.BlockSpec((1, 2, 6, d), lambda i, e: (i, 0, 0, 0)),
                  _const_spec(ng.shape), _const_spec(p["w_r"].shape), _const_spec(p["b_r"].shape),
                  pl.BlockSpec((1, d, ff), lambda i, e: (e, 0, 0)),
                  pl.BlockSpec((1, d, ff), lambda i, e: (e, 0, 0)),
                  pl.BlockSpec((1, ff, d), lambda i, e: (e, 0, 0)),
                  _const_spec(final_g.shape)],
        out_specs=full,
        scratch_shapes=[pltpu.VMEM((rows, d), BF16), pltpu.VMEM((rows, LANES), F32)],
        compiler_params=_params(2),
        name="moe",
    )(xs, mod, ng, p["w_r"], p["b_r"], p["w_gate"], p["w_up"], p["w_down"], final_g)


def _gd_proj_kernel(x_ref, xp_ref, xn_ref, mod_ref, ng_ref, w_ref, conv_ref, o_ref, *, tm, n_ctx, n_tot, n_qk_blk,
                    n_conv_blk):
    jb = pl.program_id(0)
    pos = pl.program_id(2) * tm
    sh = mod_ref[0, 0, 0:1, :]
    sc = mod_ref[0, 0, 1:2, :]
    g = ng_ref[...]
    prev_ok = jnp.logical_and(pos != 0, pos != n_ctx)
    next_ok = jnp.logical_and(pos + tm != n_ctx, pos + tm != n_tot)
    h = _norm_mod(x_ref[0], g, sh, sc)
    hp = jnp.where(prev_ok, _norm_mod(xp_ref[0], g, sh, sc), 0.0)
    hn = jnp.where(next_ok, _norm_mod(xn_ref[0], g, sh, sc), 0.0)
    w = w_ref[...]
    pc = _dot(h, w)

    @pl.when(jb >= n_conv_blk)
    def _():
        o_ref[0] = pc

    @pl.when(jb < n_conv_blk)
    def _():
        ext = jnp.concatenate([_dot(hp, w), pc, _dot(hn, w)], axis=0)
        n_ext = tm + 2 * HALO
        pad = (GD_CONV_W - 1) // 2
        acc = ext[HALO:HALO + tm] * conv_ref[pad:pad + 1, :]
        for wi in range(GD_CONV_W):
            if wi == pad:
                continue
            shifted = pltpu.roll(ext, (pad - wi) % n_ext, axis=0)
            acc = acc + shifted[HALO:HALO + tm] * conv_ref[wi:wi + 1, :]
        act = _silu(acc)

        @pl.when(jb < n_qk_blk)
        def _():
            scale = jnp.where(jb < n_qk_blk // 2, GD_DK ** -0.5, 1.0)
            parts = []
            for hh in range(act.shape[1] // GD_DK):
                seg = act[:, hh * GD_DK:(hh + 1) * GD_DK]
                ss = jnp.sum(seg * seg, axis=-1, keepdims=True)
                parts.append(seg * (lax.rsqrt(ss + NORM_EPS) * scale))
            o_ref[0] = jnp.concatenate(parts, axis=1)

        @pl.when(jb >= n_qk_blk)
        def _():
            o_ref[0] = act


def _gd_proj(xs, mod, ng, w_in, conv_w, n_ctx, kd_total, qkv_total):
    b, l, d = xs.shape
    n_out = w_in.shape[1]
    tm = TOKEN_TILE
    nb = 1024
    nt = l // tm
    nct = n_ctx // tm
    nj = n_out // nb
    n_conv_blk = qkv_total // nb
    kern = functools.partial(_gd_proj_kernel, tm=tm, n_ctx=n_ctx, n_tot=l, n_qk_blk=2 * kd_total // nb,
                             n_conv_blk=n_conv_blk)
    return pl.pallas_call(
        kern,
        out_shape=jax.ShapeDtypeStruct((b, l, n_out), F32),
        grid=(nj, b, nt),
        in_specs=_tile_specs(tm, d, nt, lead=(0,))
                 + [pl.BlockSpec((1, 1, 6, d), lambda j, i, t: (i, (t >= nct).astype(jnp.int32), 0, 0)),
                    _const_spec(ng.shape),
                    pl.BlockSpec((d, nb), lambda j, i, t: (0, j)),
                    pl.BlockSpec((GD_CONV_W, nb), lambda j, i, t: (0, jnp.minimum(j, n_conv_blk - 1)))],
        out_specs=pl.BlockSpec((1, tm, nb), lambda j, i, t: (i, t, j)),
        compiler_params=_params(3),
        name="gd_proj",
    )(xs, xs, xs, mod, ng, w_in, conv_w)


def _gd_ab_kernel(x_ref, mod_ref, ng_ref, wab_ref, alog_ref, dtb_ref, o_ref, *, n_vh):
    h = _norm_mod(x_ref[0], ng_ref[...], mod_ref[0, 0, 0:1, :], mod_ref[0, 0, 1:2, :])
    ab = _dot_hh(h, wab_ref[...])
    lane = lax.broadcasted_iota(jnp.int32, (1, LANES), 1)
    for z in range(2):
        abz = ab if z == 0 else pltpu.roll(ab, LANES - 2 * n_vh, axis=1)
        gdec = -jnp.exp(alog_ref[z:z + 1, :]) * _softplus(abz + dtb_ref[z:z + 1, :])
        beta = _sigmoid(abz)
        o_ref[z, 0] = jnp.where(lane < n_vh, gdec, jnp.where(lane < 2 * n_vh, beta, 0.0))


def _gd_ab(xs, mod, ng, wab, alog, dtb, n_ctx, n_vh):
    b, l, d = xs.shape
    tm = TOKEN_TILE
    nt = l // tm
    nct = n_ctx // tm
    return pl.pallas_call(
        functools.partial(_gd_ab_kernel, n_vh=n_vh),
        out_shape=jax.ShapeDtypeStruct((2, b, l, LANES), F32),
        grid=(b, nt),
        in_specs=[pl.BlockSpec((1, tm, d), lambda i, t: (i, t, 0)),
                  pl.BlockSpec((1, 1, 6, d), lambda i, t: (i, (t >= nct).astype(jnp.int32), 0, 0)),
                  _const_spec(ng.shape), _const_spec(wab.shape), _const_spec(alog.shape), _const_spec(dtb.shape)],
        out_specs=pl.BlockSpec((2, 1, tm, LANES), lambda i, t: (0, i, t, 0)),
        compiler_params=_params(2),
        name="gd_ab",
    )(xs, mod, ng, wab, alog, dtb)


def _gd_chunk_kernel(q0_ref, k0_ref, v0_ref, gb0_ref, q1_ref, k1_ref, v1_ref, gb1_ref,
                     e64g_ref, e64b_ref, e128g_ref, e128b_ref, o0_ref, o1_ref, s_ref, *, n_vh, heads_per_group):
    c = q0_ref.shape[1]
    n = heads_per_group
    rep = n_vh // (q0_ref.shape[2] // GD_DK)
    n_groups = n_vh // n
    gw = n * c
    kh_per_group = n // rep

    @pl.when(pl.program_id(1) == 0)
    def _():
        s_ref[...] = jnp.zeros_like(s_ref)

    bd = _block_mask(gw, gw, c, c)
    bd_k = _block_mask(gw, kh_per_group * GD_DK, c, GD_DK, rdiv=rep)
    bd_v = _block_mask(gw, n * GD_DV, c, GD_DV)

    def bdiag_v(x):
        return jnp.where(bd_v, _tile_rows(x, n), 0.0)

    refs = ((q0_ref, k0_ref, v0_ref, gb0_ref, o0_ref), (q1_ref, k1_ref, v1_ref, gb1_ref, o1_ref))
    prep = []
    for z, (q_ref, k_ref, v_ref, gb_ref, _) in enumerate(refs):
        incl, _, _ = _order_masks(z, c)
        gb = gb_ref[0, 0]
        gc = _dot_ex3(jnp.where(incl, 1.0, 0.0), gb)
        gtot = jnp.sum(gb, axis=0, keepdims=True)
        incl_all, _, eye_all = _order_masks(z, c, n_vh)
        gt64 = _dot_x3(gc, e64g_ref[...])
        gs64 = _dot_ex3(jnp.ones((c, c), F32), gt64 * eye_all)
        prep.append({
            "gam": jnp.where(incl_all, jnp.exp(jnp.where(incl_all, gt64 - gs64, 0.0)), 0.0),
            "bt64": _dot_x3(gb, e64b_ref[...]),
            "e_g": jnp.exp(_dot_x3(gc, e128g_ref[...])),
            "e_end": jnp.exp(_dot_x3(gtot - gc, e128g_ref[...])),
            "beta": _dot_x3(gb, e128b_ref[...]),
            "gl": jnp.exp(_dot_x3(jnp.broadcast_to(gtot, (HALO, gtot.shape[1])), e128g_ref[...]))[0:1],
            "q": q_ref[0], "k": k_ref[0], "v": v_ref[0], "masks": _order_masks(z, c, n)})

    chains = [(z, gi) for z in range(2) for gi in range(n_groups)]

    def ksl(gi):
        return slice(gi * kh_per_group * GD_DK, (gi + 1) * kh_per_group * GD_DK)

    def csl(gi):
        return slice(gi * gw, (gi + 1) * gw)

    def vsl(gi):
        return slice(gi * n * GD_DV, (gi + 1) * n * GD_DV)

    def per_vhead(x):
        return jnp.concatenate([x[:, (hh // rep) * GD_DK:(hh // rep + 1) * GD_DK] for hh in range(n)], axis=1)

    kg_ = [prep[z]["k"][:, ksl(gi)] for z, gi in chains]
    qg_ = [prep[z]["q"][:, ksl(gi)] for z, gi in chains]
    qkk = [_dot_nt(jnp.concatenate([kx, qx], axis=0), jnp.where(bd_k, _tile_rows(kx, n), 0.0))
           for kx, qx in zip(kg_, qg_)]
    a_mat = [jnp.where(prep[z]["masks"][1], x[:c] * prep[z]["gam"][:, csl(gi)] * prep[z]["bt64"][:, csl(gi)], 0.0)
             for x, (z, gi) in zip(qkk, chains)]
    aqk = [jnp.where(prep[z]["masks"][0], x[c:] * prep[z]["gam"][:, csl(gi)], 0.0) for x, (z, gi) in zip(qkk, chains)]
    t_inv = _tri_inv([-a for a in a_mat], bd, [prep[z]["masks"][2] for z, _ in chains])

    k2 = [per_vhead(x) for x in kg_]
    q2 = [per_vhead(x) for x in qg_]
    bg = [prep[z]["beta"][:, vsl(gi)] for z, gi in chains]
    eg = [prep[z]["e_g"][:, vsl(gi)] for z, gi in chains]
    u = [_dot(t, bdiag_v(prep[z]["v"][:, vsl(gi)] * b_)) for t, b_, (z, gi) in zip(t_inv, bg, chains)]
    w = [_dot(t, bdiag_v(kx * b_ * e_)) for t, kx, b_, e_ in zip(t_inv, k2, bg, eg)]
    qe = [qx * e_ for qx, e_ in zip(q2, eg)]
    ke = [kx * prep[z]["e_end"][:, vsl(gi)] for kx, (z, gi) in zip(k2, chains)]

    heads = [(ci, hh) for ci in range(len(chains)) for hh in range(n)]

    def hs(hh):
        return slice(hh * GD_DV, (hh + 1) * GD_DV)

    def state_index(ci, hh):
        z, gi = chains[ci]
        return z, gi * n + hh

    ws = {(ci, hh): _dot(jnp.concatenate([w[ci][:, hs(hh)], qe[ci][:, hs(hh)]], axis=0), s_ref[state_index(ci, hh)])
          for ci, hh in heads}
    vn = [jnp.concatenate([u[ci][:, hs(hh)] - ws[ci, hh][:c] for hh in range(n)], axis=1) for ci in range(len(chains))]
    for ci, (z, gi) in enumerate(chains):
        pre = jnp.concatenate([ws[ci, hh][c:] for hh in range(n)], axis=1)
        refs[z][4][0, :, vsl(gi)] = pre + _dot(aqk[ci], bdiag_v(vn[ci]))
    for ci, hh in heads:
        z, hv = state_index(ci, hh)
        gl = prep[z]["gl"][:, hv * GD_DV:(hv + 1) * GD_DV]
        s_ref[z, hv] = s_ref[z, hv] * gl + _dot_tn(ke[ci][:, hs(hh)], vn[ci][:, hs(hh)])


def _gd_chunk(proj, gb, consts, n_ctx, kd_total, vd_total):
    b, l, _ = proj.shape
    c = CHUNK
    nc = l // c
    ncc = n_ctx // c
    n_vh = vd_total // GD_DV

    def col(z, width, blk):
        return pl.BlockSpec((1, c, width), lambda i, j: (i, _chunk_index(z, j, ncc, nc), blk))

    def perdir(z):
        return pl.BlockSpec((1, 1, c, LANES), lambda i, j: (z, i, _chunk_index(z, j, ncc, nc), 0))

    def dir_specs(z):
        return [col(z, kd_total, 0), col(z, kd_total, 1), col(z, vd_total, 2 * kd_total // vd_total), perdir(z)]

    sd = jax.ShapeDtypeStruct((b, l, vd_total), F32)
    return pl.pallas_call(
        functools.partial(_gd_chunk_kernel, n_vh=n_vh, heads_per_group=4),
        out_shape=(sd, sd),
        grid=(b, nc),
        in_specs=dir_specs(0) + dir_specs(1) + [_const_spec(a.shape) for a in consts],
        out_specs=(col(0, vd_total, 0), col(1, vd_total, 0)),
        scratch_shapes=[pltpu.VMEM((2, n_vh, GD_DK, GD_DV), F32)],
        compiler_params=_params(2),
        name="gd_chunk",
    )(proj, proj, proj, gb, proj, proj, proj, gb, *consts)


def _gd_out_kernel(x_ref, o0_ref, o1_ref, z_ref, mod_ref, ngd_ref, wo_ref, xo_ref):
    o = o0_ref[0] + o1_ref[0]
    z = z_ref[0]
    parts = []
    for hh in range(o.shape[1] // GD_DV):
        seg = o[:, hh * GD_DV:(hh + 1) * GD_DV]
        ms = jnp.mean(seg * seg, axis=-1, keepdims=True)
        parts.append(seg * lax.rsqrt(ms + NORM_EPS) * ngd_ref[...])
    on = jnp.concatenate(parts, axis=1) * _silu(z)
    xo_ref[0] = x_ref[0] + mod_ref[0, 0, 2:3, :] * _dot(on, wo_ref[...])


def _gd_out(xs, o, proj, mod, ngd, w_o, n_ctx, z_blk):
    b, l, d = xs.shape
    vd = o[0].shape[-1]
    tm = TOKEN_TILE
    nct = n_ctx // tm
    nt = (l - n_ctx) // tm
    return pl.pallas_call(
        _gd_out_kernel,
        out_shape=jax.ShapeDtypeStruct((b, l - n_ctx, d), F32),
        grid=(b, nt),
        in_specs=[pl.BlockSpec((1, tm, d), lambda i, t: (i, t + nct, 0)),
                  pl.BlockSpec((1, tm, vd), lambda i, t: (i, t + nct, 0)),
                  pl.BlockSpec((1, tm, vd), lambda i, t: (i, t + nct, 0)),
                  pl.BlockSpec((1, tm, vd), lambda i, t: (i, t + nct, z_blk)),
                  pl.BlockSpec((1, 1, 6, d), lambda i, t: (i, 1, 0, 0)),
                  _const_spec(ngd.shape), _const_spec(w_o.shape)],
        out_specs=pl.BlockSpec((1, tm, d), lambda i, t: (i, t, 0)),
        compiler_params=_params(2),
        name="gd_out",
    )(xs, o[0], o[1], proj, mod, ngd, w_o)


def _pos_embed_2d(rows, d):
    quarter = d // 4
    omega = 1.0 / (POS_BASE ** (jnp.arange(quarter, dtype=F32) / quarter))

    def axis_emb(n):
        ang = jnp.arange(n, dtype=F32)[:, None] * omega[None, :]
        return jnp.concatenate([jnp.sin(ang), jnp.cos(ang)], axis=-1)

    e_row = jnp.broadcast_to(axis_emb(rows)[:, None, :], (rows, GRID_W, d // 2))
    e_col = jnp.broadcast_to(axis_emb(GRID_W)[None, :, :], (rows, GRID_W, d // 2))
    return jnp.concatenate([e_row, e_col], axis=-1).reshape(rows * GRID_W, d)


def _head_indicator(d, head):
    ch = jnp.arange(d)[:, None] // head
    ind = (ch == jnp.arange(LANES)[None, :]).astype(BF16)
    return ind, ind.T


def _expand_matrix(offset, n_heads, width):
    src = jnp.arange(LANES)[:, None]
    dst = jnp.arange(n_heads * width)[None, :] // width
    return (src == dst + offset).astype(BF16)


def _block_diag2(m):
    z = jnp.zeros_like(m[0])
    return jnp.concatenate([jnp.concatenate([m[0], z], axis=1), jnp.concatenate([z, m[1]], axis=1)], axis=0)


def _moe_params(i, moe_w_rg, moe_b_rg, moe_w_re, moe_b_re, moe_w_gate, moe_w_up, moe_w_down):
    d = moe_w_rg.shape[1]
    ne = MOE_GROUPS * MOE_EXPERTS
    pad = LANES - ne - MOE_GROUPS
    w_r = jnp.concatenate([moe_w_re[i], moe_w_rg[i], jnp.zeros((d, pad), F32)], axis=1)
    b_r = jnp.concatenate([moe_b_re[i], moe_b_rg[i], jnp.zeros((pad,), F32)])[None, :]
    ff = moe_w_gate.shape[-1]
    return {"w_r": w_r, "b_r": b_r,
            "w_gate": _bf(moe_w_gate[i]).reshape(ne, d, ff),
            "w_up": _bf(moe_w_up[i]).reshape(ne, d, ff),
            "w_down": _bf(moe_w_down[i]).reshape(ne, ff, d)}


def kernel(x, c, ctx, c_ctx, ada_w, ada_b, norm1_g, norm2_g, rw_mix, rw_w_rkv, rw_w0, rw_w1, rw_w2, rw_a0, rw_a1, rw_a2, rw_g1, rw_g2, rw_k_k, rw_k_a, rw_r_k, rw_ln_g, rw_ln_b, rw_w_o, gd_w_in, gd_conv, gd_w_ab, gd_a_log, gd_dt_bias, gd_norm_g, gd_w_o, moe_w_rg, moe_b_rg, moe_w_re, moe_b_re, moe_w_gate, moe_w_up, moe_w_down, final_g):
    bsz, n_lat, d = x.shape
    n_ctx = ctx.shape[1]
    assert n_ctx % TOKEN_TILE == 0 and n_lat % TOKEN_TILE == 0 and n_ctx % CHUNK == 0
    assert ada_w.shape[0] == 2 and d % (4 * RW_HEAD) == 0

    rows = -(-(bsz + 1) // HALO) * HALO
    cs = jnp.zeros((rows, d), F32).at[:bsz].set(c).at[bsz].set(c_ctx)
    mod_all = _ada(cs, ada_w, ada_b)

    def mod_of(i):
        lat = mod_all[i, :bsz].reshape(bsz, 1, 6, d)
        cx = jnp.broadcast_to(mod_all[i, bsz].reshape(1, 1, 6, d), (bsz, 1, 6, d))
        return jnp.concatenate([cx, lat], axis=1)

    xs = jnp.concatenate([ctx, x], axis=1)
    pos = jnp.concatenate([jnp.zeros((n_ctx, d), F32), _pos_embed_2d(n_lat // GRID_W, d)], axis=0)
    gsum, gbc = _head_indicator(d, RW_HEAD)

    mod0 = mod_of(0)
    ng1 = norm1_g[0][None, :]
    rw = {"mix": rw_mix[0], "w_rkv": _bf(rw_w_rkv[0]), "w0": rw_w0[0],
          "w1": _bf(jnp.concatenate([rw_w1[0, 0], rw_w1[0, 1]], axis=1)), "w2": _bf(_block_diag2(rw_w2[0])),
          "a0": rw_a0[0],
          "a1": _bf(jnp.concatenate([rw_a1[0, 0], rw_a1[0, 1]], axis=1)), "a2": _bf(_block_diag2(rw_a2[0])),
          "g1": _bf(rw_g1[0]), "g2": _bf(rw_g2[0]), "k_k": rw_k_k[0][None, :], "k_a": rw_k_a[0][None, :],
          "r_k": rw_r_k[0].reshape(1, d), "ln_g": rw_ln_g[0][None, :], "ln_b": rw_ln_b[0][None, :],
          "w_o": _bf(rw_w_o[0])}
    r, v, kk, gate, lw, kd, ar = _rw_feat(xs, pos, mod0, ng1, rw, gsum, gbc, n_ctx)
    y = _rw_scan(r, v, kk, lw, kd, ar, n_ctx)
    xs = _rw_out(xs, pos, y, r, v, kd, gate, mod0, rw, gsum, gbc, n_ctx)
    moe_args = (moe_w_rg, moe_b_rg, moe_w_re, moe_b_re, moe_w_gate, moe_w_up, moe_w_down)
    fg = final_g[None, :]
    xs = _moe(xs, mod0, norm2_g[0][None, :], _moe_params(0, *moe_args), fg, n_ctx, False)

    mod1 = mod_of(1)
    n_vh = gd_a_log.shape[-1]
    vd_total = n_vh * GD_DV
    kd_total = (gd_w_in.shape[-1] - 2 * vd_total) // 2
    qkv_total = 2 * kd_total + vd_total
    ng1 = norm1_g[1][None, :]
    proj = _gd_proj(xs, mod1, ng1, _bf(gd_w_in[0]), gd_conv[0], n_ctx, kd_total, qkv_total)
    wab = jnp.concatenate([gd_w_ab[0, 0], gd_w_ab[0, 1], jnp.zeros((d, LANES - 4 * n_vh), F32)], axis=1)
    lane_pad = jnp.zeros((2, LANES - n_vh), F32)
    alog = jnp.concatenate([gd_a_log[0], lane_pad], axis=1)
    dtb = jnp.concatenate([gd_dt_bias[0], lane_pad], axis=1)
    gb = _gd_ab(xs, mod1, ng1, wab, alog, dtb, n_ctx, n_vh)
    consts = (_expand_matrix(0, n_vh, CHUNK), _expand_matrix(n_vh, n_vh, CHUNK),
              _expand_matrix(0, n_vh, GD_DV), _expand_matrix(n_vh, n_vh, GD_DV))
    o = _gd_chunk(proj, gb, consts, n_ctx, kd_total, vd_total)
    ngd = gd_norm_g[0][None, :]
    x_lat = _gd_out(xs, o, proj, mod1, ngd, _bf(gd_w_o[0]), n_ctx, qkv_total // vd_total)
    return _moe(x_lat, mod1, norm2_g[1][None, :], _moe_params(1, *moe_args), fg, 0, True)
```

```python
import functools
import math

import jax
import jax.numpy as jnp
from jax import lax
from jax.experimental import pallas as pl
from jax.experimental.pallas import tpu as pltpu

F32 = jnp.float32
BF16 = jnp.bfloat16

NORM_EPS = 1e-6
RW_GN_EPS = 64e-5
POS_BASE = 10000.0
GRID_W = 64
RW_HEAD = 64
RW_LORA = 64
GD_DK = 128
GD_DV = 128
GD_CONV_W = 5
MOE_GROUPS = 4
MOE_EXPERTS = 8
CHUNK = 64
LANES = 128
HALO = 8
TOKEN_TILE = 256
VMEM_LIMIT = 56 * 1024 * 1024
NEG_BIG = -1e30


def _bf(x):
    return x.astype(BF16)


def _dot(a, b):
    return jnp.dot(_bf(a), _bf(b), preferred_element_type=F32)


def _dot_nt(a, b):
    return lax.dot_general(_bf(a), _bf(b), (((1,), (1,)), ((), ())), preferred_element_type=F32)


def _dot_tn(a, b):
    return lax.dot_general(_bf(a), _bf(b), (((0,), (0,)), ((), ())), preferred_element_type=F32)


def _split2(x):
    hi = x.astype(BF16)
    lo = (x - hi.astype(F32)).astype(BF16)
    return hi, lo


def _split3(x):
    hi = x.astype(BF16)
    r1 = x - hi.astype(F32)
    mid = r1.astype(BF16)
    lo = (r1 - mid.astype(F32)).astype(BF16)
    return hi, mid, lo


def _dot_x2(a, b_exact):
    hi, lo = _split2(a)
    b = _bf(b_exact)
    return (jnp.dot(hi, b, preferred_element_type=F32) + jnp.dot(lo, b, preferred_element_type=F32))


def _dot_x3(a, b_exact):
    hi, mid, lo = _split3(a)
    b = _bf(b_exact)
    return (jnp.dot(hi, b, preferred_element_type=F32) + jnp.dot(mid, b, preferred_element_type=F32)
            + jnp.dot(lo, b, preferred_element_type=F32))


def _dot_ex3(a_exact, b):
    hi, mid, lo = _split3(b)
    a = _bf(a_exact)
    return (jnp.dot(a, hi, preferred_element_type=F32) + jnp.dot(a, mid, preferred_element_type=F32)
            + jnp.dot(a, lo, preferred_element_type=F32))


def _dot_hh(a, b):
    ah, al = _split2(a)
    bh, bl = _split2(b)
    return (jnp.dot(ah, bh, preferred_element_type=F32) + jnp.dot(al, bh, preferred_element_type=F32)
            + jnp.dot(ah, bl, preferred_element_type=F32))


def _sigmoid(x):
    return 1.0 / (1.0 + jnp.exp(-x))


def _silu(x):
    return x * _sigmoid(x)


def _softplus(x):
    return jnp.maximum(x, 0.0) + jnp.log(1.0 + jnp.exp(-jnp.abs(x)))


def _norm_mod(x, g, shift, scale):
    ms = jnp.mean(x * x, axis=-1, keepdims=True)
    return (x * lax.rsqrt(ms + NORM_EPS) * g) * (1.0 + scale) + shift


def _tile_rows(x, n):
    return jnp.concatenate([x] * n, axis=0)


def _tile_lanes(x, n):
    return jnp.concatenate([x] * n, axis=1)


def _order_masks(direction, c, n=1):
    row = lax.broadcasted_iota(jnp.int32, (c, n * c), 0)
    col = lax.broadcasted_iota(jnp.int32, (c, n * c), 1) % c
    diff = (col - row) * (1 - 2 * direction)
    return diff <= 0, diff < 0, jnp.where(diff == 0, 1.0, 0.0)


def _block_mask(rows, cols, rblk, cblk, rdiv=1):
    r = lax.broadcasted_iota(jnp.int32, (rows, cols), 0) // rblk
    c = lax.broadcasted_iota(jnp.int32, (rows, cols), 1) // cblk
    return (r // rdiv) == c if rdiv != 1 else r == c


TRI_BASE = 4


def _tri_inv(a_list, bd, eye_list):
    c = a_list[0].shape[0]
    n = a_list[0].shape[1] // c
    row = lax.broadcasted_iota(jnp.int32, a_list[0].shape, 0)
    col = lax.broadcasted_iota(jnp.int32, a_list[0].shape, 1) % c

    def same_block(m):
        return (row // m) == (col // m)

    def mm(x, y):
        return jnp.dot(_bf(x), _bf(jnp.where(bd, _tile_rows(y, n), 0.0)), preferred_element_type=F32)

    base = same_block(TRI_BASE)
    a0 = [jnp.where(base, a, 0.0) for a in a_list]
    x = [e + a for e, a in zip(eye_list, a0)]
    p = [mm(a, a) for a in a0]
    base_levels = int(math.log2(TRI_BASE)) - 1
    for lvl in range(base_levels):
        x = [xi + mm(pi, xi) for xi, pi in zip(x, p)]
        if lvl + 1 < base_levels:
            p = [mm(pi, pi) for pi in p]
    m = TRI_BASE
    while m < c:
        off = jnp.logical_and(same_block(2 * m), jnp.logical_not(same_block(m)))
        t = [mm(jnp.where(off, a, 0.0), xi) for a, xi in zip(a_list, x)]
        x = [xi + mm(xi, ti) for xi, ti in zip(x, t)]
        m *= 2
    return x


def _chunk_index(direction, j, n_ctx_chunks, n_chunks):
    bwd = jnp.where(j < n_ctx_chunks, n_ctx_chunks - 1 - j, n_chunks + n_ctx_chunks - 1 - j)
    return jnp.where(direction == 0, j, bwd)


def _const_spec(shape):
    nd = len(shape)
    return pl.BlockSpec(shape, lambda *_: (0,) * nd, pipeline_mode=pl.Buffered(1))


def _params(n_axes):
    return pltpu.CompilerParams(dimension_semantics=("arbitrary",) * n_axes, vmem_limit_bytes=VMEM_LIMIT)


def _ada_kernel(cs_ref, w_ref, b_ref, o_ref):
    s = _silu(cs_ref[...])
    o_ref[0] = _dot_hh(s, w_ref[0]) + b_ref[0]


def _ada(cs, ada_w, ada_b):
    n_layers, d, n6 = ada_w.shape
    rows = cs.shape[0]
    tn = 1536
    return pl.pallas_call(
        _ada_kernel,
        out_shape=jax.ShapeDtypeStruct((n_layers, rows, n6), F32),
        grid=(n_layers, n6 // tn),
        in_specs=[pl.BlockSpec((rows, d), lambda i, j: (0, 0)),
                  pl.BlockSpec((1, d, tn), lambda i, j: (i, 0, j)),
                  pl.BlockSpec((1, 1, tn), lambda i, j: (i, 0, j))],
        out_specs=pl.BlockSpec((1, rows, tn), lambda i, j: (i, 0, j)),
        compiler_params=_params(2),
        name="ada",
    )(cs, ada_w, ada_b.reshape(n_layers, 1, n6))


def _shifted_neighbours(h, h_prev, h_next, pos, tm, n_ctx, n_tot):
    prev_ok = jnp.logical_and(pos != 0, pos != n_ctx)
    next_ok = jnp.logical_and(pos + tm != n_ctx, pos + tm != n_tot)
    h_prev = jnp.where(prev_ok, h_prev, 0.0)
    h_next = jnp.where(next_ok, h_next, 0.0)
    row = lax.broadcasted_iota(jnp.int32, (tm, 1), 0)
    prev = jnp.where(row == 0, h_prev, pltpu.roll(h, 1, axis=0))
    nxt = jnp.where(row == tm - 1, h_next, pltpu.roll(h, tm - 1, axis=0))
    return prev, nxt


def _tile_specs(tm, d, n_tiles, lead=()):
    nl = len(lead)
    hb = tm // HALO
    last = n_tiles * hb - 1

    def cur(*g):
        return (g[nl], g[nl + 1], 0)

    def prv(*g):
        return (g[nl], jnp.maximum(g[nl + 1] * hb - 1, 0), 0)

    def nxt(*g):
        return (g[nl], jnp.minimum((g[nl + 1] + 1) * hb, last), 0)

    return [pl.BlockSpec((1, tm, d), cur), pl.BlockSpec((1, HALO, d), prv), pl.BlockSpec((1, HALO, d), nxt)]


def _rw_feat_kernel(x_ref, xp_ref, xn_ref, pos_ref, posp_ref, posn_ref, mod_ref, ng_ref, mix_ref,
                    wrkv_ref, w0_ref, w1_ref, w2_ref, a0_ref, a1_ref, a2_ref, g1_ref, g2_ref,
                    kkp_ref, kap_ref, gsum_ref, gbc_ref,
                    r_o, v_o, kk_o, gate_o, lw_o, kd_o, a_o, *, tm, n_ctx, n_tot):
    d = x_ref.shape[-1]
    pos = pl.program_id(1) * tm
    sh = mod_ref[0, 0, 0:1, :]
    sc = mod_ref[0, 0, 1:2, :]
    g = ng_ref[...]
    h = _norm_mod(x_ref[0] + pos_ref[...], g, sh, sc)
    hp = _norm_mod(xp_ref[0, HALO - 1:HALO, :] + posp_ref[HALO - 1:HALO, :], g, sh, sc)
    hn = _norm_mod(xn_ref[0, 0:1, :] + posn_ref[0:1, :], g, sh, sc)
    prev, nxt = _shifted_neighbours(h, hp, hn, pos, tm, n_ctx, n_tot)
    xx = 0.5 * (prev + nxt) - h
    mix = mix_ref[...]
    xr = h + xx * mix[0:1]
    xw = h + xx * mix[1:2]
    xk = h + xx * mix[2:3]
    xv = h + xx * mix[3:4]
    xa = h + xx * mix[4:5]
    xg = h + xx * mix[5:6]

    r_o[0] = _dot(xr, wrkv_ref[0])
    v_o[0] = _dot(xv, wrkv_ref[2])
    gate_o[0] = _dot(_sigmoid(_dot(xg, g1_ref[...])), g2_ref[...])
    k = _dot(xk, wrkv_ref[1])

    kq = k * kkp_ref[...]
    ssq = _dot_x2(kq * kq, gsum_ref[...])
    kk_o[0] = kq * _dot_x2(lax.rsqrt(ssq + NORM_EPS), gbc_ref[...])

    wl = _dot(jnp.tanh(_dot(xw, w1_ref[...])), w2_ref[...])
    al = _dot(_dot(xa, a1_ref[...]), a2_ref[...])
    ka = kap_ref[...]
    for z in range(2):
        zw = w0_ref[z:z + 1, :] + wl[:, z * d:(z + 1) * d]
        lw_o[z, 0] = (-math.exp(-0.5)) * _sigmoid(zw)
        rate = _sigmoid(a0_ref[z:z + 1, :] + al[:, z * d:(z + 1) * d])
        a_o[z, 0] = rate
        kd_o[z, 0] = k * (1.0 + (rate - 1.0) * ka)


def _rw_feat(xs, pos, mod, ng, p, gsum, gbc, n_ctx):
    b, l, d = xs.shape
    tm = TOKEN_TILE
    nt = l // tm
    nct = n_ctx // tm
    hb = tm // HALO
    kern = functools.partial(_rw_feat_kernel, tm=tm, n_ctx=n_ctx, n_tot=l)
    tok = pl.BlockSpec((1, tm, d), lambda i, t: (i, t, 0))
    tok2 = pl.BlockSpec((2, 1, tm, d), lambda i, t: (0, i, t, 0))
    pos_specs = [pl.BlockSpec((tm, d), lambda i, t: (t, 0)),
                 pl.BlockSpec((HALO, d), lambda i, t: (jnp.maximum(t * hb - 1, 0), 0)),
                 pl.BlockSpec((HALO, d), lambda i, t: (jnp.minimum((t + 1) * hb, nt * hb - 1), 0))]
    in_specs = (_tile_specs(tm, d, nt) + pos_specs
                + [pl.BlockSpec((1, 1, 6, d), lambda i, t: (i, _seg_index(t, nct), 0, 0))]
                + [_const_spec(a.shape) for a in (ng, p["mix"], p["w_rkv"], p["w0"], p["w1"], p["w2"], p["a0"],
                                                  p["a1"], p["a2"], p["g1"], p["g2"], p["k_k"], p["k_a"], gsum, gbc)])
    sd = jax.ShapeDtypeStruct((b, l, d), F32)
    sd2 = jax.ShapeDtypeStruct((2, b, l, d), F32)
    return pl.pallas_call(
        kern,
        out_shape=(sd, sd, sd, sd, sd2, sd2, sd2),
        grid=(b, nt),
        in_specs=in_specs,
        out_specs=(tok, tok, tok, tok, tok2, tok2, tok2),
        compiler_params=_params(2),
        name="rw_feat",
    )(xs, xs, xs, pos, pos, pos, mod, ng, p["mix"], p["w_rkv"], p["w0"], p["w1"], p["w2"], p["a0"], p["a1"], p["a2"],
      p["g1"], p["g2"], p["k_k"], p["k_a"], gsum, gbc)


def _rw_scan_kernel(r0_ref, v0_ref, kk0_ref, lw0_ref, kd0_ref, a0_ref,
                    r1_ref, v1_ref, kk1_ref, lw1_ref, kd1_ref, a1_ref,
                    y0_ref, y1_ref, s_ref, *, heads_per_group):
    c = r0_ref.shape[1]
    d = r0_ref.shape[2]
    n = heads_per_group
    gw = n * RW_HEAD
    n_groups = d // gw
    nt = (((1,), (1,)), ((), ()))

    @pl.when(pl.program_id(1) == 0)
    def _():
        s_ref[...] = jnp.zeros_like(s_ref)

    bd = _block_mask(n * c, gw, c, RW_HEAD)

    def bdiag(x):
        return jnp.where(bd, _tile_rows(x, n), 0.0)

    refs = ((r0_ref, v0_ref, kk0_ref, lw0_ref, kd0_ref, a0_ref, y0_ref),
            (r1_ref, v1_ref, kk1_ref, lw1_ref, kd1_ref, a1_ref, y1_ref))
    prep = []
    for z, (r_ref, v_ref, kk_ref, lw_ref, kd_ref, a_ref, _) in enumerate(refs):
        incl, _, _ = _order_masks(z, c)
        lw = lw_ref[0, 0]
        lam = _dot_ex3(jnp.where(incl, 1.0, 0.0), lw)
        tot = jnp.sum(lw, axis=0, keepdims=True)
        w_inv = jnp.exp(-lam)
        w_end = jnp.exp(tot - lam)
        kk = kk_ref[0]
        kd = kd_ref[0, 0]
        bb = kk * a_ref[0, 0]
        prep.append({"a_t": -(kk * jnp.exp(lam - lw)), "r_t": r_ref[0] * jnp.exp(lam), "b_t": bb * w_inv,
                     "k_t": kd * w_inv, "b_e": bb * w_end, "k_e": kd * w_end, "v": v_ref[0],
                     "w_tot": jnp.exp(tot), "masks": _order_masks(z, c, n)})

    chains = [(z, gi) for z in range(2) for gi in range(n_groups)]

    def sl(gi):
        return slice(gi * gw, (gi + 1) * gw)

    lhs = [_bf(jnp.concatenate([prep[z]["a_t"][:, sl(gi)], prep[z]["r_t"][:, sl(gi)]], axis=0)) for z, gi in chains]
    ab = [lax.dot_general(lh, _bf(bdiag(prep[z]["b_t"][:, sl(gi)])), nt, preferred_element_type=F32)
          for lh, (z, gi) in zip(lhs, chains)]
    ak = [lax.dot_general(lh, _bf(bdiag(prep[z]["k_t"][:, sl(gi)])), nt, preferred_element_type=F32)
          for lh, (z, gi) in zip(lhs, chains)]
    a_ab = [jnp.where(prep[z]["masks"][1], x[:c], 0.0) for x, (z, _) in zip(ab, chains)]
    a_rb = [jnp.where(prep[z]["masks"][0], x[c:], 0.0) for x, (z, _) in zip(ab, chains)]
    a_kk = [jnp.concatenate([jnp.where(prep[z]["masks"][1], x[:c], 0.0), jnp.where(prep[z]["masks"][0], x[c:], 0.0)],
                            axis=0) for x, (z, _) in zip(ak, chains)]
    st = [s_ref[z, gi] for z, gi in chains]
    pre = [lax.dot_general(lh, _bf(s), nt, preferred_element_type=F32) for lh, s in zip(lhs, st)]
    vg = [prep[z]["v"][:, sl(gi)] for z, gi in chains]
    av = [_dot(x, bdiag(v)) for x, v in zip(a_kk, vg)]
    t_inv = _tri_inv(a_ab, bd, [prep[z]["masks"][2] for z, _ in chains])
    u = [_dot(t, bdiag(p[:c] + q[:c])) for t, p, q in zip(t_inv, pre, av)]
    yv = [p[c:] + q[c:] + _dot(x, bdiag(ui)) for p, q, x, ui in zip(pre, av, a_rb, u)]
    upd = [_dot_tn(jnp.concatenate([ui, v], axis=0),
                   jnp.concatenate([prep[z]["b_e"][:, sl(gi)], prep[z]["k_e"][:, sl(gi)]], axis=0))
           for ui, v, (z, gi) in zip(u, vg, chains)]
    for (z, gi), y_val, s, up in zip(chains, yv, st, upd):
        refs[z][6][0, :, sl(gi)] = y_val
        s_ref[z, gi] = s * prep[z]["w_tot"][:, sl(gi)] + jnp.where(bd, up, 0.0)


def _rw_scan(r, v, kk, lw, kd, a, n_ctx):
    b, l, d = r.shape
    c = CHUNK
    nc = l // c
    ncc = n_ctx // c
    hpg = 4
    gw = hpg * RW_HEAD

    def shared(z):
        return pl.BlockSpec((1, c, d), lambda i, j: (i, _chunk_index(z, j, ncc, nc), 0))

    def perdir(z):
        return pl.BlockSpec((1, 1, c, d), lambda i, j: (z, i, _chunk_index(z, j, ncc, nc), 0))

    sd = jax.ShapeDtypeStruct((b, l, d), F32)
    return pl.pallas_call(
        functools.partial(_rw_scan_kernel, heads_per_group=hpg),
        out_shape=(sd, sd),
        grid=(b, nc),
        in_specs=[shared(0)] * 3 + [perdir(0)] * 3 + [shared(1)] * 3 + [perdir(1)] * 3,
        out_specs=(shared(0), shared(1)),
        scratch_shapes=[pltpu.VMEM((2, d // gw, gw, gw), F32)],
        compiler_params=_params(2),
        name="rw_scan",
    )(r, v, kk, lw, kd, a, r, v, kk, lw, kd, a)


def _rw_out_kernel(x_ref, pos_ref, y0_ref, y1_ref, r_ref, v_ref, kd_ref, gate_ref, mod_ref, rk_ref, lng_ref, lnb_ref,
                   wo_ref, gsum_ref, gbc_ref, xo_ref):
    inv_n = 1.0 / RW_HEAD
    gsum = gsum_ref[...]
    gbc = gbc_ref[...]
    y = y0_ref[0] + y1_ref[0]
    mu = _dot_x2(_dot_x2(y, gsum) * inv_n, gbc)
    yc = y - mu
    var = _dot_x2(yc * yc, gsum) * inv_n
    yn = yc * _dot_x2(lax.rsqrt(var + RW_GN_EPS), gbc) * lng_ref[...] + lnb_ref[...]
    k_bonus = 0.5 * (kd_ref[0, 0] + kd_ref[1, 0])
    bonus = _dot_x2(_dot_x2(r_ref[0] * k_bonus * rk_ref[...], gsum), gbc) * v_ref[0]
    out = (yn + bonus) * gate_ref[0]
    o = _dot(out, wo_ref[...])
    xo_ref[0] = x_ref[0] + pos_ref[...] + mod_ref[0, 0, 2:3, :] * o


def _rw_out(xs, pos, y, r, v, kd, gate, mod, p, gsum, gbc, n_ctx):
    b, l, d = xs.shape
    tm = TOKEN_TILE
    nt = l // tm
    nct = n_ctx // tm
    tok = pl.BlockSpec((1, tm, d), lambda i, t: (i, t, 0))
    tok2 = pl.BlockSpec((2, 1, tm, d), lambda i, t: (0, i, t, 0))
    consts = (p["r_k"], p["ln_g"], p["ln_b"], p["w_o"], gsum, gbc)
    return pl.pallas_call(
        _rw_out_kernel,
        out_shape=jax.ShapeDtypeStruct((b, l, d), F32),
        grid=(b, nt),
        in_specs=[tok, pl.BlockSpec((tm, d), lambda i, t: (t, 0)), tok, tok, tok, tok, tok2, tok,
                  pl.BlockSpec((1, 1, 6, d), lambda i, t: (i, _seg_index(t, nct), 0, 0))]
                 + [_const_spec(a.shape) for a in consts],
        out_specs=tok,
        compiler_params=_params(2),
        name="rw_out",
    )(xs, pos, y[0], y[1], r, v, kd, gate, mod, *consts)


def _route(logits):
    ne = MOE_GROUPS * MOE_EXPERTS
    lane = lax.broadcasted_iota(jnp.int32, logits.shape, 1).astype(F32)
    far = float(4 * LANES)
    g_mask = jnp.logical_and(lane >= ne, lane < ne + MOE_GROUPS)
    gl = jnp.where(g_mask, logits, NEG_BIG)
    gmax = jnp.max(gl, axis=1, keepdims=True)
    gsum = jnp.sum(jnp.where(g_mask, jnp.exp(gl - gmax), 0.0), axis=1, keepdims=True)
    gp = 1.0 / gsum
    g_first = jnp.min(jnp.where(gl == gmax, lane, far), axis=1, keepdims=True) - ne
    in_grp = jnp.logical_and(lane >= g_first * MOE_EXPERTS, lane < (g_first + 1.0) * MOE_EXPERTS)
    el = jnp.where(in_grp, logits, NEG_BIG)
    m1 = jnp.max(el, axis=1, keepdims=True)
    i1 = jnp.min(jnp.where(el == m1, lane, far), axis=1, keepdims=True)
    el2 = jnp.where(lane == i1, NEG_BIG, el)
    m2 = jnp.max(el2, axis=1, keepdims=True)
    i2 = jnp.min(jnp.where(el2 == m2, lane, far), axis=1, keepdims=True)
    e2 = jnp.exp(m2 - m1)
    p1 = gp / (1.0 + e2)
    p2 = p1 * e2
    comb = jnp.where(lane == i1, p1, jnp.where(lane == i2, p2, 0.0))
    return jnp.where(lane == g_first + ne, 1.0, comb)


def _seg_index(t, n_ctx_tiles):
    return jnp.where(t >= n_ctx_tiles, 1, 0)


def _moe_route_kernel(x_ref, mod_ref, ng_ref, wr_ref, br_ref, h_ref, comb_ref):
    h = _norm_mod(x_ref[0], ng_ref[...], mod_ref[0, 0, 3:4, :], mod_ref[0, 0, 4:5, :])
    h_ref[0] = _bf(h)
    comb_ref[0] = _route(_dot_hh(h, wr_ref[...]) + br_ref[...])


def _moe_route(xs, mod, ng, p, n_ctx):
    b, rows, d = xs.shape
    tm = TOKEN_TILE
    nct = n_ctx // tm
    return pl.pallas_call(
        _moe_route_kernel,
        out_shape=(jax.ShapeDtypeStruct((b, rows, d), BF16), jax.ShapeDtypeStruct((b, rows, LANES), F32)),
        grid=(b, rows // tm),
        in_specs=[pl.BlockSpec((1, tm, d), lambda i, t: (i, t, 0)),
                  pl.BlockSpec((1, 1, 6, d), lambda i, t: (i, _seg_index(t, nct), 0, 0)),
                  _const_spec(ng.shape), _const_spec(p["w_r"].shape), _const_spec(p["b_r"].shape)],
        out_specs=(pl.BlockSpec((1, tm, d), lambda i, t: (i, t, 0)),
                   pl.BlockSpec((1, tm, LANES), lambda i, t: (i, t, 0))),
        compiler_params=_params(2),
        name="moe_route",
    )(xs, mod, ng, p["w_r"], p["b_r"])


MOE_ROW_BLOCK = 128


def _moe_expert_kernel(h_ref, comb_ref, tri_ref, upper_ref, wg_ref, wu_ref, wd_ref, m_ref,
                       xs_s, cs_s, acc_s, dest_s, seg_s, *, tb):
    rows = h_ref.shape[1]
    slots = xs_s.shape[0]
    ne = MOE_GROUPS * MOE_EXPERTS
    rb = MOE_ROW_BLOCK
    e = pl.program_id(1)
    n_tb = rows // tb

    @pl.when(e == 0)
    def _():
        lane = lax.broadcasted_iota(jnp.int32, (1, LANES), 1)
        g_lanes = jnp.logical_and(lane >= ne, lane < ne + MOE_GROUPS)
        carry = jnp.zeros((1, LANES), F32)
        ranks = []
        for i in range(n_tb):
            gind = jnp.where(g_lanes, comb_ref[0, i * tb:(i + 1) * tb, :], 0.0)
            r = jnp.dot(tri_ref[...], _bf(gind), preferred_element_type=F32) + carry
            carry = r[tb - 1:tb, :]
            ranks.append((gind, r))
        padded = jnp.ceil(carry * (1.0 / rb)) * rb
        start = _dot_x3(jnp.broadcast_to(padded, (HALO, LANES)), upper_ref[...])[0:1]
        for g in range(MOE_GROUPS):
            pick = lane == ne + g
            seg_s[g] = jnp.sum(jnp.where(pick, start, 0.0)).astype(jnp.int32)
            seg_s[MOE_GROUPS + g] = jnp.sum(jnp.where(pick, padded, 0.0)).astype(jnp.int32)
        dest_rows = []
        for i, (gind, r) in enumerate(ranks):
            dest = jnp.sum(gind * (start + r - 1.0), axis=1, keepdims=True)
            dest_b = jnp.broadcast_to(dest, (tb, LANES))
            dest_s[i * tb:(i + 1) * tb, :] = dest_b
            dest_rows.append(dest_b.T[0:1, :])
        dest_row = jnp.concatenate(dest_rows, axis=1)
        comb = comb_ref[0]
        hb = h_ref[0]

        def permute(sb, c_):
            base = pl.multiple_of(sb * tb, tb)
            slot = (lax.broadcasted_iota(jnp.int32, (tb, rows), 0) + base).astype(F32)
            perm = jnp.where(slot == dest_row, 1.0, 0.0)
            xs_s[pl.ds(base, tb), :] = _bf(jnp.dot(_bf(perm), hb, preferred_element_type=F32))
            cs_s[pl.ds(base, tb), :] = _dot_ex3(perm, comb)
            acc_s[pl.ds(base, tb), :] = jnp.zeros((tb, acc_s.shape[1]), F32)
            return c_
        lax.fori_loop(0, slots // tb, permute, 0)

    grp = e // MOE_EXPERTS
    start = seg_s[grp]
    n_blk = seg_s[MOE_GROUPS + grp] // rb

    def expert(i, c_):
        sl = pl.ds(pl.multiple_of(start + i * rb, rb), rb)
        xb = xs_s[sl, :]
        hg = jnp.dot(xb, wg_ref[0], preferred_element_type=F32)
        hu = jnp.dot(xb, wu_ref[0], preferred_element_type=F32)
        cs = cs_s[sl, :]
        lane = lax.broadcasted_iota(jnp.int32, cs.shape, 1)
        ce = jnp.sum(jnp.where(lane == e, cs, 0.0), axis=1, keepdims=True)
        act = _silu(hg) * hu * ce
        acc_s[sl, :] += jnp.dot(_bf(act), wd_ref[0], preferred_element_type=F32)
        return c_
    lax.fori_loop(0, n_blk, expert, 0)

    @pl.when(e == pl.num_programs(1) - 1)
    def _():
        acc = _bf(acc_s[...])
        for i in range(n_tb):
            slot = lax.broadcasted_iota(jnp.int32, (tb, slots), 1).astype(F32)
            unperm = jnp.where(slot == dest_s[i * tb:(i + 1) * tb, 0:1], 1.0, 0.0)
            m_ref[0, i * tb:(i + 1) * tb, :] = _bf(jnp.dot(_bf(unperm), acc, preferred_element_type=F32))


def _moe_experts(h, comb, p):
    b, rows, d = h.shape
    ne, _, ff = p["w_gate"].shape
    tb = TOKEN_TILE
    slots = rows + tb * (-(-(MOE_GROUPS * MOE_ROW_BLOCK) // tb))
    tri = jnp.tril(jnp.ones((tb, tb), BF16))
    upper = jnp.triu(jnp.ones((LANES, LANES), BF16), 1)
    return pl.pallas_call(
        functools.partial(_moe_expert_kernel, tb=tb),
        out_shape=jax.ShapeDtypeStruct((b, rows, d), BF16),
        grid=(b, ne),
        in_specs=[pl.BlockSpec((1, rows, d), lambda i, e: (i, 0, 0), pipeline_mode=pl.Buffered(1)),
                  pl.BlockSpec((1, rows, LANES), lambda i, e: (i, 0, 0), pipeline_mode=pl.Buffered(1)),
                  _const_spec(tri.shape), _const_spec(upper.shape),
                  pl.BlockSpec((1, d, ff), lambda i, e: (e, 0, 0)),
                  pl.BlockSpec((1, d, ff), lambda i, e: (e, 0, 0)),
                  pl.BlockSpec((1, ff, d), lambda i, e: (e, 0, 0))],
        out_specs=pl.BlockSpec((1, rows, d), lambda i, e: (i, 0, 0)),
        scratch_shapes=[pltpu.VMEM((slots, d), BF16), pltpu.VMEM((slots, LANES), F32), pltpu.VMEM((slots, d), F32),
                        pltpu.VMEM((rows, LANES), F32), pltpu.SMEM((2 * MOE_GROUPS,), jnp.int32)],
        compiler_params=_params(2),
        name="moe_experts",
    )(h, comb, tri, upper, p["w_gate"], p["w_up"], p["w_down"])


def _moe_resid_kernel(x_ref, m_ref, mod_ref, fg_ref, xo_ref, *, final_norm):
    xn = x_ref[0] + mod_ref[0, 0, 5:6, :] * m_ref[0].astype(F32)
    if final_norm:
        ms = jnp.mean(xn * xn, axis=-1, keepdims=True)
        xn = xn * lax.rsqrt(ms + NORM_EPS) * fg_ref[...]
    xo_ref[0] = xn


def _moe_resid(xs, m, mod, final_g, n_ctx, final_norm):
    b, rows, d = xs.shape
    tm = TOKEN_TILE
    nct = n_ctx // tm
    tok = pl.BlockSpec((1, tm, d), lambda i, t: (i, t, 0))
    return pl.pallas_call(
        functools.partial(_moe_resid_kernel, final_norm=final_norm),
        out_shape=jax.ShapeDtypeStruct((b, rows, d), F32),
        grid=(b, rows // tm),
        in_specs=[tok, tok, pl.BlockSpec((1, 1, 6, d), lambda i, t: (i, _seg_index(t, nct), 0, 0)),
                  _const_spec(final_g.shape)],
        out_specs=tok,
        compiler_params=_params(2),
        name="moe_resid",
    )(xs, m, mod, final_g)


def _moe(xs, mod, ng, p, final_g, n_ctx, final_norm):
    h, comb = _moe_route(xs, mod, ng, p, n_ctx)
    m = _moe_experts(h, comb, p)
    return _moe_resid(xs, m, mod, final_g, n_ctx, final_norm)


def _gd_proj_kernel(x_ref, xp_ref, xn_ref, mod_ref, ng_ref, w_ref, conv_ref, o_ref, *, tm, n_ctx, n_tot, n_qk_blk,
                    n_conv_blk):
    jb = pl.program_id(0)
    pos = pl.program_id(2) * tm
    sh = mod_ref[0, 0, 0:1, :]
    sc = mod_ref[0, 0, 1:2, :]
    g = ng_ref[...]
    prev_ok = jnp.logical_and(pos != 0, pos != n_ctx)
    next_ok = jnp.logical_and(pos + tm != n_ctx, pos + tm != n_tot)
    h = _norm_mod(x_ref[0], g, sh, sc)
    hp = jnp.where(prev_ok, _norm_mod(xp_ref[0], g, sh, sc), 0.0)
    hn = jnp.where(next_ok, _norm_mod(xn_ref[0], g, sh, sc), 0.0)
    ext = _dot(jnp.concatenate([hp, h, hn], axis=0), w_ref[...])
    n_ext = tm + 2 * HALO

    @pl.when(jb >= n_conv_blk)
    def _():
        o_ref[0] = ext[HALO:HALO + tm]

    @pl.when(jb < n_conv_blk)
    def _():
        pad = (GD_CONV_W - 1) // 2
        acc = ext[HALO:HALO + tm] * conv_ref[pad:pad + 1, :]
        for wi in range(GD_CONV_W):
            if wi == pad:
                continue
            shifted = pltpu.roll(ext, (pad - wi) % n_ext, axis=0)
            acc = acc + shifted[HALO:HALO + tm] * conv_ref[wi:wi + 1, :]
        act = _silu(acc)

        @pl.when(jb < n_qk_blk)
        def _():
            scale = jnp.where(jb < n_qk_blk // 2, GD_DK ** -0.5, 1.0)
            parts = []
            for hh in range(act.shape[1] // GD_DK):
                seg = act[:, hh * GD_DK:(hh + 1) * GD_DK]
                ss = jnp.sum(seg * seg, axis=-1, keepdims=True)
                parts.append(seg * (lax.rsqrt(ss + NORM_EPS) * scale))
            o_ref[0] = jnp.concatenate(parts, axis=1)

        @pl.when(jb >= n_qk_blk)
        def _():
            o_ref[0] = act


def _gd_proj(xs, mod, ng, w_in, conv_w, n_ctx, kd_total, qkv_total):
    b, l, d = xs.shape
    n_out = w_in.shape[1]
    tm = TOKEN_TILE
    nb = 1024
    nt = l // tm
    nct = n_ctx // tm
    nj = n_out // nb
    n_conv_blk = qkv_total // nb
    kern = functools.partial(_gd_proj_kernel, tm=tm, n_ctx=n_ctx, n_tot=l, n_qk_blk=2 * kd_total // nb,
                             n_conv_blk=n_conv_blk)
    return pl.pallas_call(
        kern,
        out_shape=jax.ShapeDtypeStruct((b, l, n_out), F32),
        grid=(nj, b, nt),
        in_specs=_tile_specs(tm, d, nt, lead=(0,))
                 + [pl.BlockSpec((1, 1, 6, d), lambda j, i, t: (i, _seg_index(t, nct), 0, 0)),
                    _const_spec(ng.shape),
                    pl.BlockSpec((d, nb), lambda j, i, t: (0, j)),
                    pl.BlockSpec((GD_CONV_W, nb), lambda j, i, t: (0, jnp.minimum(j, n_conv_blk - 1)))],
        out_specs=pl.BlockSpec((1, tm, nb), lambda j, i, t: (i, t, j)),
        compiler_params=_params(3),
        name="gd_proj",
    )(xs, xs, xs, mod, ng, w_in, conv_w)


def _gd_ab_kernel(x_ref, mod_ref, ng_ref, wab_ref, alog_ref, dtb_ref, o_ref, *, n_vh):
    h = _norm_mod(x_ref[0], ng_ref[...], mod_ref[0, 0, 0:1, :], mod_ref[0, 0, 1:2, :])
    ab = _dot_hh(h, wab_ref[...])
    lane = lax.broadcasted_iota(jnp.int32, (1, LANES), 1)
    for z in range(2):
        abz = ab if z == 0 else pltpu.roll(ab, LANES - 2 * n_vh, axis=1)
        gdec = -jnp.exp(alog_ref[z:z + 1, :]) * _softplus(abz + dtb_ref[z:z + 1, :])
        beta = _sigmoid(abz)
        o_ref[z, 0] = jnp.where(lane < n_vh, gdec, jnp.where(lane < 2 * n_vh, beta, 0.0))


def _gd_ab(xs, mod, ng, wab, alog, dtb, n_ctx, n_vh):
    b, l, d = xs.shape
    tm = TOKEN_TILE
    nt = l // tm
    nct = n_ctx // tm
    return pl.pallas_call(
        functools.partial(_gd_ab_kernel, n_vh=n_vh),
        out_shape=jax.ShapeDtypeStruct((2, b, l, LANES), F32),
        grid=(b, nt),
        in_specs=[pl.BlockSpec((1, tm, d), lambda i, t: (i, t, 0)),
                  pl.BlockSpec((1, 1, 6, d), lambda i, t: (i, _seg_index(t, nct), 0, 0)),
                  _const_spec(ng.shape), _const_spec(wab.shape), _const_spec(alog.shape), _const_spec(dtb.shape)],
        out_specs=pl.BlockSpec((2, 1, tm, LANES), lambda i, t: (0, i, t, 0)),
        compiler_params=_params(2),
        name="gd_ab",
    )(xs, mod, ng, wab, alog, dtb)


def _gd_chunk_kernel(q0_ref, k0_ref, v0_ref, gb0_ref, q1_ref, k1_ref, v1_ref, gb1_ref,
                     e64g_ref, e64b_ref, e128g_ref, e128b_ref, o0_ref, o1_ref, s_ref, *, n_vh, heads_per_group):
    c = q0_ref.shape[1]
    n = heads_per_group
    rep = n_vh // (q0_ref.shape[2] // GD_DK)
    n_groups = n_vh // n
    gw = n * c
    kh_per_group = n // rep

    @pl.when(pl.program_id(1) == 0)
    def _():
        s_ref[...] = jnp.zeros_like(s_ref)

    bd = _block_mask(gw, gw, c, c)
    bd_k = _block_mask(gw, kh_per_group * GD_DK, c, GD_DK, rdiv=rep)
    bd_v = _block_mask(gw, n * GD_DV, c, GD_DV)

    def bdiag_v(x):
        return jnp.where(bd_v, _tile_rows(x, n), 0.0)

    refs = ((q0_ref, k0_ref, v0_ref, gb0_ref, o0_ref), (q1_ref, k1_ref, v1_ref, gb1_ref, o1_ref))
    prep = []
    for z, (q_ref, k_ref, v_ref, gb_ref, _) in enumerate(refs):
        incl, _, _ = _order_masks(z, c)
        gb = gb_ref[0, 0]
        gc = _dot_ex3(jnp.where(incl, 1.0, 0.0), gb)
        gtot = jnp.sum(gb, axis=0, keepdims=True)
        incl_all, _, eye_all = _order_masks(z, c, n_vh)
        gt64 = _dot_x3(gc, e64g_ref[...])
        gs64 = _dot_ex3(jnp.ones((c, c), F32), gt64 * eye_all)
        prep.append({
            "gam": jnp.where(incl_all, jnp.exp(jnp.where(incl_all, gt64 - gs64, 0.0)), 0.0),
            "bt64": _dot(gb, e64b_ref[...]),
            "e_g": _dot(jnp.exp(gc), e128g_ref[...]),
            "e_end": _dot(jnp.exp(gtot - gc), e128g_ref[...]),
            "beta": _dot(gb, e128b_ref[...]),
            "gl": jnp.exp(gtot),
            "q": q_ref[0], "k": k_ref[0], "v": v_ref[0], "masks": _order_masks(z, c, n)})

    chains = [(z, gi) for z in range(2) for gi in range(n_groups)]

    def ksl(gi):
        return slice(gi * kh_per_group * GD_DK, (gi + 1) * kh_per_group * GD_DK)

    def csl(gi):
        return slice(gi * gw, (gi + 1) * gw)

    def vsl(gi):
        return slice(gi * n * GD_DV, (gi + 1) * n * GD_DV)

    def per_vhead(x):
        return jnp.concatenate([x[:, (hh // rep) * GD_DK:(hh // rep + 1) * GD_DK] for hh in range(n)], axis=1)

    kg_ = [prep[z]["k"][:, ksl(gi)] for z, gi in chains]
    qg_ = [prep[z]["q"][:, ksl(gi)] for z, gi in chains]
    qkk = [_dot_nt(jnp.concatenate([kx, qx], axis=0), jnp.where(bd_k, _tile_rows(kx, n), 0.0))
           for kx, qx in zip(kg_, qg_)]
    a_mat = [jnp.where(prep[z]["masks"][1], x[:c] * prep[z]["gam"][:, csl(gi)] * prep[z]["bt64"][:, csl(gi)], 0.0)
             for x, (z, gi) in zip(qkk, chains)]
    aqk = [jnp.where(prep[z]["masks"][0], x[c:] * prep[z]["gam"][:, csl(gi)], 0.0) for x, (z, gi) in zip(qkk, chains)]
    t_inv = _tri_inv([-a for a in a_mat], bd, [prep[z]["masks"][2] for z, _ in chains])

    k2 = [per_vhead(x) for x in kg_]
    q2 = [per_vhead(x) for x in qg_]
    bg = [prep[z]["beta"][:, vsl(gi)] for z, gi in chains]
    eg = [prep[z]["e_g"][:, vsl(gi)] for z, gi in chains]
    u = [_dot(t, bdiag_v(prep[z]["v"][:, vsl(gi)] * b_)) for t, b_, (z, gi) in zip(t_inv, bg, chains)]
    w = [_dot(t, bdiag_v(kx * b_ * e_)) for t, kx, b_, e_ in zip(t_inv, k2, bg, eg)]
    qe = [qx * e_ for qx, e_ in zip(q2, eg)]
    ke = [kx * prep[z]["e_end"][:, vsl(gi)] for kx, (z, gi) in zip(k2, chains)]

    heads = [(ci, hh) for ci in range(len(chains)) for hh in range(n)]

    def hs(hh):
        return slice(hh * GD_DV, (hh + 1) * GD_DV)

    def state_index(ci, hh):
        z, gi = chains[ci]
        return z, gi * n + hh

    ws = {(ci, hh): _dot(jnp.concatenate([w[ci][:, hs(hh)], qe[ci][:, hs(hh)]], axis=0), s_ref[state_index(ci, hh)])
          for ci, hh in heads}
    vn = [jnp.concatenate([u[ci][:, hs(hh)] - ws[ci, hh][:c] for hh in range(n)], axis=1) for ci in range(len(chains))]
    for ci, (z, gi) in enumerate(chains):
        pre = jnp.concatenate([ws[ci, hh][c:] for hh in range(n)], axis=1)
        refs[z][4][0, :, vsl(gi)] = pre + _dot(aqk[ci], bdiag_v(vn[ci]))
    for ci, hh in heads:
        z, hv = state_index(ci, hh)
        gl = jnp.broadcast_to(prep[z]["gl"][:, hv:hv + 1], (1, GD_DV))
        s_ref[z, hv] = s_ref[z, hv] * gl + _dot_tn(ke[ci][:, hs(hh)], vn[ci][:, hs(hh)])


def _gd_chunk(proj, gb, consts, n_ctx, kd_total, vd_total):
    b, l, _ = proj.shape
    c = CHUNK
    nc = l // c
    ncc = n_ctx // c
    n_vh = vd_total // GD_DV

    def col(z, width, blk):
        return pl.BlockSpec((1, c, width), lambda i, j: (i, _chunk_index(z, j, ncc, nc), blk))

    def perdir(z):
        return pl.BlockSpec((1, 1, c, LANES), lambda i, j: (z, i, _chunk_index(z, j, ncc, nc), 0))

    def dir_specs(z):
        return [col(z, kd_total, 0), col(z, kd_total, 1), col(z, vd_total, 2 * kd_total // vd_total), perdir(z)]

    sd = jax.ShapeDtypeStruct((b, l, vd_total), F32)
    return pl.pallas_call(
        functools.partial(_gd_chunk_kernel, n_vh=n_vh, heads_per_group=4),
        out_shape=(sd, sd),
        grid=(b, nc),
        in_specs=dir_specs(0) + dir_specs(1) + [_const_spec(a.shape) for a in consts],
        out_specs=(col(0, vd_total, 0), col(1, vd_total, 0)),
        scratch_shapes=[pltpu.VMEM((2, n_vh, GD_DK, GD_DV), F32)],
        compiler_params=_params(2),
        name="gd_chunk",
    )(proj, proj, proj, gb, proj, proj, proj, gb, *consts)


def _gd_out_kernel(x_ref, o0_ref, o1_ref, z_ref, mod_ref, ngd_ref, wo_ref, xo_ref):
    o = o0_ref[0] + o1_ref[0]
    z = z_ref[0]
    parts = []
    for hh in range(o.shape[1] // GD_DV):
        seg = o[:, hh * GD_DV:(hh + 1) * GD_DV]
        ms = jnp.mean(seg * seg, axis=-1, keepdims=True)
        parts.append(seg * lax.rsqrt(ms + NORM_EPS) * ngd_ref[...])
    on = jnp.concatenate(parts, axis=1) * _silu(z)
    xo_ref[0] = x_ref[0] + mod_ref[0, 0, 2:3, :] * _dot(on, wo_ref[...])


def _gd_out(xs, o, proj, mod, ngd, w_o, n_ctx, z_blk):
    b, l, d = xs.shape
    vd = o[0].shape[-1]
    tm = TOKEN_TILE
    nct = n_ctx // tm
    nt = (l - n_ctx) // tm
    return pl.pallas_call(
        _gd_out_kernel,
        out_shape=jax.ShapeDtypeStruct((b, l - n_ctx, d), F32),
        grid=(b, nt),
        in_specs=[pl.BlockSpec((1, tm, d), lambda i, t: (i, t + nct, 0)),
                  pl.BlockSpec((1, tm, vd), lambda i, t: (i, t + nct, 0)),
                  pl.BlockSpec((1, tm, vd), lambda i, t: (i, t + nct, 0)),
                  pl.BlockSpec((1, tm, vd), lambda i, t: (i, t + nct, z_blk)),
                  pl.BlockSpec((1, 1, 6, d), lambda i, t: (i, 1, 0, 0)),
                  _const_spec(ngd.shape), _const_spec(w_o.shape)],
        out_specs=pl.BlockSpec((1, tm, d), lambda i, t: (i, t, 0)),
        compiler_params=_params(2),
        name="gd_out",
    )(xs, o[0], o[1], proj, mod, ngd, w_o)


def _pos_embed_2d(rows, d):
    quarter = d // 4
    omega = 1.0 / (POS_BASE ** (jnp.arange(quarter, dtype=F32) / quarter))

    def axis_emb(n):
        ang = jnp.arange(n, dtype=F32)[:, None] * omega[None, :]
        return jnp.concatenate([jnp.sin(ang), jnp.cos(ang)], axis=-1)

    e_row = jnp.broadcast_to(axis_emb(rows)[:, None, :], (rows, GRID_W, d // 2))
    e_col = jnp.broadcast_to(axis_emb(GRID_W)[None, :, :], (rows, GRID_W, d // 2))
    return jnp.concatenate([e_row, e_col], axis=-1).reshape(rows * GRID_W, d)


def _head_indicator(d, head):
    ch = jnp.arange(d)[:, None] // head
    ind = (ch == jnp.arange(LANES)[None, :]).astype(BF16)
    return ind, ind.T


def _expand_matrix(offset, n_heads, width):
    src = jnp.arange(LANES)[:, None]
    dst = jnp.arange(n_heads * width)[None, :] // width
    return (src == dst + offset).astype(BF16)


def _block_diag2(m):
    z = jnp.zeros_like(m[0])
    return jnp.concatenate([jnp.concatenate([m[0], z], axis=1), jnp.concatenate([z, m[1]], axis=1)], axis=0)


def _moe_params(i, moe_w_rg, moe_b_rg, moe_w_re, moe_b_re, moe_w_gate, moe_w_up, moe_w_down):
    d = moe_w_rg.shape[1]
    ne = MOE_GROUPS * MOE_EXPERTS
    pad = LANES - ne - MOE_GROUPS
    w_r = jnp.concatenate([moe_w_re[i], moe_w_rg[i], jnp.zeros((d, pad), F32)], axis=1)
    b_r = jnp.concatenate([moe_b_re[i], moe_b_rg[i], jnp.zeros((pad,), F32)])[None, :]
    ff = moe_w_gate.shape[-1]
    return {"w_r": w_r, "b_r": b_r,
            "w_gate": _bf(moe_w_gate[i]).reshape(ne, d, ff),
            "w_up": _bf(moe_w_up[i]).reshape(ne, d, ff),
            "w_down": _bf(moe_w_down[i]).reshape(ne, ff, d)}


def kernel(x, c, ctx, c_ctx, ada_w, ada_b, norm1_g, norm2_g, rw_mix, rw_w_rkv, rw_w0, rw_w1, rw_w2, rw_a0, rw_a1, rw_a2, rw_g1, rw_g2, rw_k_k, rw_k_a, rw_r_k, rw_ln_g, rw_ln_b, rw_w_o, gd_w_in, gd_conv, gd_w_ab, gd_a_log, gd_dt_bias, gd_norm_g, gd_w_o, moe_w_rg, moe_b_rg, moe_w_re, moe_b_re, moe_w_gate, moe_w_up, moe_w_down, final_g):
    bsz, n_lat, d = x.shape
    n_ctx = ctx.shape[1]
    assert n_ctx % TOKEN_TILE == 0 and n_lat % TOKEN_TILE == 0 and n_ctx % CHUNK == 0
    assert ada_w.shape[0] == 2 and d % (4 * RW_HEAD) == 0

    rows = -(-(bsz + 1) // HALO) * HALO
    cs = jnp.zeros((rows, d), F32).at[:bsz].set(c).at[bsz].set(c_ctx)
    mod_all = _ada(cs, ada_w, ada_b)

    def mod_of(i):
        lat = mod_all[i, :bsz].reshape(bsz, 1, 6, d)
        cx = jnp.broadcast_to(mod_all[i, bsz].reshape(1, 1, 6, d), (bsz, 1, 6, d))
        return jnp.concatenate([cx, lat], axis=1)

    xs = jnp.concatenate([ctx, x], axis=1)
    pos = jnp.concatenate([jnp.zeros((n_ctx, d), F32), _pos_embed_2d(n_lat // GRID_W, d)], axis=0)
    gsum, gbc = _head_indicator(d, RW_HEAD)

    mod0 = mod_of(0)
    ng1 = norm1_g[0][None, :]
    rw = {"mix": rw_mix[0], "w_rkv": _bf(rw_w_rkv[0]), "w0": rw_w0[0],
          "w1": _bf(jnp.concatenate([rw_w1[0, 0], rw_w1[0, 1]], axis=1)), "w2": _bf(_block_diag2(rw_w2[0])),
          "a0": rw_a0[0],
          "a1": _bf(jnp.concatenate([rw_a1[0, 0], rw_a1[0, 1]], axis=1)), "a2": _bf(_block_diag2(rw_a2[0])),
          "g1": _bf(rw_g1[0]), "g2": _bf(rw_g2[0]), "k_k": rw_k_k[0][None, :], "k_a": rw_k_a[0][None, :],
          "r_k": rw_r_k[0].reshape(1, d), "ln_g": rw_ln_g[0][None, :], "ln_b": rw_ln_b[0][None, :],
          "w_o": _bf(rw_w_o[0])}
    r, v, kk, gate, lw, kd, ar = _rw_feat(xs, pos, mod0, ng1, rw, gsum, gbc, n_ctx)
    y = _rw_scan(r, v, kk, lw, kd, ar, n_ctx)
    xs = _rw_out(xs, pos, y, r, v, kd, gate, mod0, rw, gsum, gbc, n_ctx)
    moe_args = (moe_w_rg, moe_b_rg, moe_w_re, moe_b_re, moe_w_gate, moe_w_up, moe_w_down)
    fg = final_g[None, :]
    xs = _moe(xs, mod0, norm2_g[0][None, :], _moe_params(0, *moe_args), fg, n_ctx, False)

    mod1 = mod_of(1)
    n_vh = gd_a_log.shape[-1]
    vd_total = n_vh * GD_DV
    kd_total = (gd_w_in.shape[-1] - 2 * vd_total) // 2
    qkv_total = 2 * kd_total + vd_total
    ng1 = norm1_g[1][None, :]
    proj = _gd_proj(xs, mod1, ng1, _bf(gd_w_in[0]), gd_conv[0], n_ctx, kd_total, qkv_total)
    wab = jnp.concatenate([gd_w_ab[0, 0], gd_w_ab[0, 1], jnp.zeros((d, LANES - 4 * n_vh), F32)], axis=1)
    lane_pad = jnp.zeros((2, LANES - n_vh), F32)
    alog = jnp.concatenate([gd_a_log[0], lane_pad], axis=1)
    dtb = jnp.concatenate([gd_dt_bias[0], lane_pad], axis=1)
    gb = _gd_ab(xs, mod1, ng1, wab, alog, dtb, n_ctx, n_vh)
    consts = (_expand_matrix(0, n_vh, CHUNK), _expand_matrix(n_vh, n_vh, CHUNK),
              _expand_matrix(0, n_vh, GD_DV), _expand_matrix(n_vh, n_vh, GD_DV))
    o = _gd_chunk(proj, gb, consts, n_ctx, kd_total, vd_total)
    ngd = gd_norm_g[0][None, :]
    x_lat = _gd_out(xs, o, proj, mod1, ngd, _bf(gd_w_o[0]), n_ctx, qkv_total // vd_total)
    return _moe(x_lat, mod1, norm2_g[1][None, :], _moe_params(1, *moe_args), fg, 0, True)
```

```python
import functools
import math

import jax
import jax.numpy as jnp
from jax import lax
from jax.experimental import pallas as pl
from jax.experimental.pallas import tpu as pltpu

F32 = jnp.float32
BF16 = jnp.bfloat16

NORM_EPS = 1e-6
RW_GN_EPS = 64e-5
POS_BASE = 10000.0
GRID_W = 64
RW_HEAD = 64
RW_LORA = 64
GD_DK = 128
GD_DV = 128
GD_CONV_W = 5
MOE_GROUPS = 4
MOE_EXPERTS = 8
CHUNK = 64
LANES = 128
HALO = 8
TOKEN_TILE = 256
VMEM_LIMIT = 56 * 1024 * 1024
NEG_BIG = -1e30


def _bf(x):
    return x.astype(BF16)


def _dot(a, b):
    return jnp.dot(_bf(a), _bf(b), preferred_element_type=F32)


def _dot_nt(a, b):
    return lax.dot_general(_bf(a), _bf(b), (((1,), (1,)), ((), ())), preferred_element_type=F32)


def _dot_tn(a, b):
    return lax.dot_general(_bf(a), _bf(b), (((0,), (0,)), ((), ())), preferred_element_type=F32)


def _split2(x):
    hi = x.astype(BF16)
    lo = (x - hi.astype(F32)).astype(BF16)
    return hi, lo


def _split3(x):
    hi = x.astype(BF16)
    r1 = x - hi.astype(F32)
    mid = r1.astype(BF16)
    lo = (r1 - mid.astype(F32)).astype(BF16)
    return hi, mid, lo


def _dot_x2(a, b_exact):
    hi, lo = _split2(a)
    b = _bf(b_exact)
    return (jnp.dot(hi, b, preferred_element_type=F32) + jnp.dot(lo, b, preferred_element_type=F32))


def _dot_x3(a, b_exact):
    hi, mid, lo = _split3(a)
    b = _bf(b_exact)
    return (jnp.dot(hi, b, preferred_element_type=F32) + jnp.dot(mid, b, preferred_element_type=F32)
            + jnp.dot(lo, b, preferred_element_type=F32))


def _dot_ex3(a_exact, b):
    hi, mid, lo = _split3(b)
    a = _bf(a_exact)
    return (jnp.dot(a, hi, preferred_element_type=F32) + jnp.dot(a, mid, preferred_element_type=F32)
            + jnp.dot(a, lo, preferred_element_type=F32))


def _dot_hh(a, b):
    ah, al = _split2(a)
    bh, bl = _split2(b)
    return (jnp.dot(ah, bh, preferred_element_type=F32) + jnp.dot(al, bh, preferred_element_type=F32)
            + jnp.dot(ah, bl, preferred_element_type=F32))


def _sigmoid(x):
    return 1.0 / (1.0 + jnp.exp(-x))


def _silu(x):
    return x * _sigmoid(x)


def _softplus(x):
    return jnp.maximum(x, 0.0) + jnp.log(1.0 + jnp.exp(-jnp.abs(x)))


def _norm_mod(x, g, shift, scale):
    ms = jnp.mean(x * x, axis=-1, keepdims=True)
    return (x * lax.rsqrt(ms + NORM_EPS) * g) * (1.0 + scale) + shift


def _tile_rows(x, n):
    return jnp.concatenate([x] * n, axis=0)


def _tile_lanes(x, n):
    return jnp.concatenate([x] * n, axis=1)


def _order_masks(direction, c, n=1):
    row = lax.broadcasted_iota(jnp.int32, (c, n * c), 0)
    col = lax.broadcasted_iota(jnp.int32, (c, n * c), 1) % c
    diff = (col - row) * (1 - 2 * direction)
    return diff <= 0, diff < 0, jnp.where(diff == 0, 1.0, 0.0)


def _cumsum_rows(x, reverse):
    n = x.shape[0]
    row = lax.broadcasted_iota(jnp.int32, (n, 1), 0)
    s = 1
    while s < n:
        if reverse:
            shifted = jnp.where(row < n - s, pltpu.roll(x, n - s, axis=0), 0.0)
        else:
            shifted = jnp.where(row >= s, pltpu.roll(x, s, axis=0), 0.0)
        x = x + shifted
        s *= 2
    return x


def _block_mask(rows, cols, rblk, cblk, rdiv=1):
    r = lax.broadcasted_iota(jnp.int32, (rows, cols), 0) // rblk
    c = lax.broadcasted_iota(jnp.int32, (rows, cols), 1) // cblk
    return (r // rdiv) == c if rdiv != 1 else r == c


TRI_BASE = 4
SCAN_BATCH = 2


def _tri_inv(a_list, bd, eye_list):
    c = a_list[0].shape[0]
    n = a_list[0].shape[1] // c
    row = lax.broadcasted_iota(jnp.int32, a_list[0].shape, 0)
    col = lax.broadcasted_iota(jnp.int32, a_list[0].shape, 1) % c

    def same_block(m):
        return (row // m) == (col // m)

    def mm(x, y):
        return jnp.dot(_bf(x), _bf(jnp.where(bd, _tile_rows(y, n), 0.0)), preferred_element_type=F32)

    base = same_block(TRI_BASE)
    a0 = [jnp.where(base, a, 0.0) for a in a_list]
    x = [e + a for e, a in zip(eye_list, a0)]
    p = [mm(a, a) for a in a0]
    base_levels = int(math.log2(TRI_BASE)) - 1
    for lvl in range(base_levels):
        x = [xi + mm(pi, xi) for xi, pi in zip(x, p)]
        if lvl + 1 < base_levels:
            p = [mm(pi, pi) for pi in p]
    m = TRI_BASE
    while m < c:
        off = jnp.logical_and(same_block(2 * m), jnp.logical_not(same_block(m)))
        t = [mm(jnp.where(off, a, 0.0), xi) for a, xi in zip(a_list, x)]
        x = [xi + mm(xi, ti) for xi, ti in zip(x, t)]
        m *= 2
    return x


def _chunk_index(direction, j, n_ctx_chunks, n_chunks):
    bwd = jnp.where(j < n_ctx_chunks, n_ctx_chunks - 1 - j, n_chunks + n_ctx_chunks - 1 - j)
    return jnp.where(direction == 0, j, bwd)


def _const_spec(shape):
    nd = len(shape)
    return pl.BlockSpec(shape, lambda *_: (0,) * nd, pipeline_mode=pl.Buffered(1))


def _params(n_axes):
    return pltpu.CompilerParams(dimension_semantics=("arbitrary",) * n_axes, vmem_limit_bytes=VMEM_LIMIT)


def _ada_kernel(cs_ref, w_ref, b_ref, o_ref):
    s = _silu(cs_ref[...])
    o_ref[0] = _dot_hh(s, w_ref[0]) + b_ref[0]


def _ada(cs, ada_w, ada_b):
    n_layers, d, n6 = ada_w.shape
    rows = cs.shape[0]
    tn = 1536
    return pl.pallas_call(
        _ada_kernel,
        out_shape=jax.ShapeDtypeStruct((n_layers, rows, n6), F32),
        grid=(n_layers, n6 // tn),
        in_specs=[pl.BlockSpec((rows, d), lambda i, j: (0, 0)),
                  pl.BlockSpec((1, d, tn), lambda i, j: (i, 0, j)),
                  pl.BlockSpec((1, 1, tn), lambda i, j: (i, 0, j))],
        out_specs=pl.BlockSpec((1, rows, tn), lambda i, j: (i, 0, j)),
        compiler_params=_params(2),
        name="ada",
    )(cs, ada_w, ada_b.reshape(n_layers, 1, n6))


def _shifted_neighbours(h, h_prev, h_next, pos, tm, n_ctx, n_tot):
    prev_ok = jnp.logical_and(pos != 0, pos != n_ctx)
    next_ok = jnp.logical_and(pos + tm != n_ctx, pos + tm != n_tot)
    h_prev = jnp.where(prev_ok, h_prev, 0.0)
    h_next = jnp.where(next_ok, h_next, 0.0)
    row = lax.broadcasted_iota(jnp.int32, (tm, 1), 0)
    prev = jnp.where(row == 0, h_prev, pltpu.roll(h, 1, axis=0))
    nxt = jnp.where(row == tm - 1, h_next, pltpu.roll(h, tm - 1, axis=0))
    return prev, nxt


def _tile_specs(tm, d, n_tiles, lead=()):
    nl = len(lead)
    hb = tm // HALO
    last = n_tiles * hb - 1

    def cur(*g):
        return (g[nl], g[nl + 1], 0)

    def prv(*g):
        return (g[nl], jnp.maximum(g[nl + 1] * hb - 1, 0), 0)

    def nxt(*g):
        return (g[nl], jnp.minimum((g[nl + 1] + 1) * hb, last), 0)

    return [pl.BlockSpec((1, tm, d), cur), pl.BlockSpec((1, HALO, d), prv), pl.BlockSpec((1, HALO, d), nxt)]


def _rw_feat_kernel(x_ref, xp_ref, xn_ref, pos_ref, posp_ref, posn_ref, mod_ref, ng_ref, mix_ref,
                    wrkv_ref, w0_ref, w1_ref, w2_ref, a0_ref, a1_ref, a2_ref, g1_ref, g2_ref,
                    kkp_ref, kap_ref, gsum_ref, gbc_ref,
                    r_o, v_o, kk_o, gate_o, lw_o, kd_o, a_o, *, tm, n_ctx, n_tot):
    d = x_ref.shape[-1]
    pos = pl.program_id(1) * tm
    sh = mod_ref[0, 0, 0:1, :]
    sc = mod_ref[0, 0, 1:2, :]
    g = ng_ref[...]
    h = _norm_mod(x_ref[0] + pos_ref[...], g, sh, sc)
    hp = _norm_mod(xp_ref[0, HALO - 1:HALO, :] + posp_ref[HALO - 1:HALO, :], g, sh, sc)
    hn = _norm_mod(xn_ref[0, 0:1, :] + posn_ref[0:1, :], g, sh, sc)
    prev, nxt = _shifted_neighbours(h, hp, hn, pos, tm, n_ctx, n_tot)
    xx = 0.5 * (prev + nxt) - h
    mix = mix_ref[...]
    xr = h + xx * mix[0:1]
    xw = h + xx * mix[1:2]
    xk = h + xx * mix[2:3]
    xv = h + xx * mix[3:4]
    xa = h + xx * mix[4:5]
    xg = h + xx * mix[5:6]

    r_o[0] = _dot(xr, wrkv_ref[0])
    v_o[0] = _dot(xv, wrkv_ref[2])
    gate_o[0] = _dot(_sigmoid(_dot(xg, g1_ref[...])), g2_ref[...])
    k = _dot(xk, wrkv_ref[1])

    kq = k * kkp_ref[...]
    ssq = _dot_x2(kq * kq, gsum_ref[...])
    kk_o[0] = kq * _dot_x2(lax.rsqrt(ssq + NORM_EPS), gbc_ref[...])

    wl = _dot(jnp.tanh(_dot(xw, w1_ref[...])), w2_ref[...])
    al = _dot(_dot(xa, a1_ref[...]), a2_ref[...])
    ka = kap_ref[...]
    for z in range(2):
        zw = w0_ref[z:z + 1, :] + wl[:, z * d:(z + 1) * d]
        lw_o[z, 0] = (-math.exp(-0.5)) * _sigmoid(zw)
        rate = _sigmoid(a0_ref[z:z + 1, :] + al[:, z * d:(z + 1) * d])
        a_o[z, 0] = rate
        kd_o[z, 0] = k * (1.0 + (rate - 1.0) * ka)


def _rw_feat(xs, pos, mod, ng, p, gsum, gbc, n_ctx):
    b, l, d = xs.shape
    tm = TOKEN_TILE
    nt = l // tm
    nct = n_ctx // tm
    hb = tm // HALO
    kern = functools.partial(_rw_feat_kernel, tm=tm, n_ctx=n_ctx, n_tot=l)
    tok = pl.BlockSpec((1, tm, d), lambda i, t: (i, t, 0))
    tok2 = pl.BlockSpec((2, 1, tm, d), lambda i, t: (0, i, t, 0))
    pos_specs = [pl.BlockSpec((tm, d), lambda i, t: (t, 0)),
                 pl.BlockSpec((HALO, d), lambda i, t: (jnp.maximum(t * hb - 1, 0), 0)),
                 pl.BlockSpec((HALO, d), lambda i, t: (jnp.minimum((t + 1) * hb, nt * hb - 1), 0))]
    in_specs = (_tile_specs(tm, d, nt) + pos_specs
                + [pl.BlockSpec((1, 1, 6, d), lambda i, t: (i, _seg_index(t, nct), 0, 0))]
                + [_const_spec(a.shape) for a in (ng, p["mix"], p["w_rkv"], p["w0"], p["w1"], p["w2"], p["a0"],
                                                  p["a1"], p["a2"], p["g1"], p["g2"], p["k_k"], p["k_a"], gsum, gbc)])
    sd = jax.ShapeDtypeStruct((b, l, d), F32)
    sd2 = jax.ShapeDtypeStruct((2, b, l, d), F32)
    return pl.pallas_call(
        kern,
        out_shape=(sd, sd, sd, sd, sd2, sd2, sd2),
        grid=(b, nt),
        in_specs=in_specs,
        out_specs=(tok, tok, tok, tok, tok2, tok2, tok2),
        compiler_params=_params(2),
        name="rw_feat",
    )(xs, xs, xs, pos, pos, pos, mod, ng, p["mix"], p["w_rkv"], p["w0"], p["w1"], p["w2"], p["a0"], p["a1"], p["a2"],
      p["g1"], p["g2"], p["k_k"], p["k_a"], gsum, gbc)


def _rw_scan_kernel(r0_ref, v0_ref, kk0_ref, lw0_ref, kd0_ref, a0_ref,
                    r1_ref, v1_ref, kk1_ref, lw1_ref, kd1_ref, a1_ref,
                    y0_ref, y1_ref, s_ref, *, heads_per_group):
    c = r0_ref.shape[1]
    d = r0_ref.shape[2]
    n = heads_per_group
    gw = n * RW_HEAD
    n_groups = d // gw
    nt = (((1,), (1,)), ((), ()))

    @pl.when(pl.program_id(1) == 0)
    def _():
        s_ref[...] = jnp.zeros_like(s_ref)

    bd = _block_mask(n * c, gw, c, RW_HEAD)

    def bdiag(x):
        return jnp.where(bd, _tile_rows(x, n), 0.0)

    refs = ((r0_ref, v0_ref, kk0_ref, lw0_ref, kd0_ref, a0_ref, y0_ref),
            (r1_ref, v1_ref, kk1_ref, lw1_ref, kd1_ref, a1_ref, y1_ref))
    masks = [_order_masks(z, c, n) for z in range(2)]
    prep = {}
    for bi in range(r0_ref.shape[0]):
        for z, (r_ref, v_ref, kk_ref, lw_ref, kd_ref, a_ref, _) in enumerate(refs):
            lw = lw_ref[0, bi]
            lam = _cumsum_rows(lw, z == 1)
            tot = jnp.sum(lw, axis=0, keepdims=True)
            w_inv = jnp.exp(-lam)
            w_end = jnp.exp(tot - lam)
            kk = kk_ref[bi]
            kd = kd_ref[0, bi]
            bb = kk * a_ref[0, bi]
            prep[bi, z] = {"a_t": -(kk * jnp.exp(lam - lw)), "r_t": r_ref[bi] * jnp.exp(lam), "b_t": bb * w_inv,
                           "k_t": kd * w_inv, "b_e": bb * w_end, "k_e": kd * w_end, "v": v_ref[bi],
                           "w_tot": jnp.exp(tot)}

    chains = [(bi, z, gi) for bi in range(r0_ref.shape[0]) for z in range(2) for gi in range(n_groups)]

    def sl(gi):
        return slice(gi * gw, (gi + 1) * gw)

    def part(name, ch):
        bi, z, gi = ch
        return prep[bi, z][name][:, sl(gi)]

    lhs = [_bf(jnp.concatenate([part("a_t", ch), part("r_t", ch)], axis=0)) for ch in chains]
    ab = [lax.dot_general(lh, _bf(bdiag(part("b_t", ch))), nt, preferred_element_type=F32)
          for lh, ch in zip(lhs, chains)]
    ak = [lax.dot_general(lh, _bf(bdiag(part("k_t", ch))), nt, preferred_element_type=F32)
          for lh, ch in zip(lhs, chains)]
    a_ab = [jnp.where(masks[z][1], x[:c], 0.0) for x, (_, z, _) in zip(ab, chains)]
    a_rb = [jnp.where(masks[z][0], x[c:], 0.0) for x, (_, z, _) in zip(ab, chains)]
    a_kk = [jnp.concatenate([jnp.where(masks[z][1], x[:c], 0.0), jnp.where(masks[z][0], x[c:], 0.0)], axis=0)
            for x, (_, z, _) in zip(ak, chains)]
    st = [s_ref[ch] for ch in chains]
    pre = [lax.dot_general(lh, _bf(s), nt, preferred_element_type=F32) for lh, s in zip(lhs, st)]
    vg = [part("v", ch) for ch in chains]
    av = [_dot(x, bdiag(v)) for x, v in zip(a_kk, vg)]
    t_inv = _tri_inv(a_ab, bd, [masks[z][2] for _, z, _ in chains])
    u = [_dot(t, bdiag(p[:c] + q[:c])) for t, p, q in zip(t_inv, pre, av)]
    yv = [p[c:] + q[c:] + _dot(x, bdiag(ui)) for p, q, x, ui in zip(pre, av, a_rb, u)]
    upd = [_dot_tn(jnp.concatenate([ui, v], axis=0), jnp.concatenate([part("b_e", ch), part("k_e", ch)], axis=0))
           for ui, v, ch in zip(u, vg, chains)]
    for ch, y_val, s, up in zip(chains, yv, st, upd):
        bi, z, gi = ch
        refs[z][6][bi, :, sl(gi)] = y_val
        s_ref[ch] = s * part("w_tot", ch) + jnp.where(bd, up, 0.0)


def _rw_scan(r, v, kk, lw, kd, a, n_ctx):
    b, l, d = r.shape
    c = CHUNK
    nc = l // c
    ncc = n_ctx // c
    hpg = 4
    gw = hpg * RW_HEAD
    bpb = SCAN_BATCH if b % SCAN_BATCH == 0 else 1

    def shared(z):
        return pl.BlockSpec((bpb, c, d), lambda i, j: (i, _chunk_index(z, j, ncc, nc), 0))

    def perdir(z):
        return pl.BlockSpec((1, bpb, c, d), lambda i, j: (z, i, _chunk_index(z, j, ncc, nc), 0))

    sd = jax.ShapeDtypeStruct((b, l, d), F32)
    return pl.pallas_call(
        functools.partial(_rw_scan_kernel, heads_per_group=hpg),
        out_shape=(sd, sd),
        grid=(b // bpb, nc),
        in_specs=[shared(0)] * 3 + [perdir(0)] * 3 + [shared(1)] * 3 + [perdir(1)] * 3,
        out_specs=(shared(0), shared(1)),
        scratch_shapes=[pltpu.VMEM((bpb, 2, d // gw, gw, gw), F32)],
        compiler_params=_params(2),
        name="rw_scan",
    )(r, v, kk, lw, kd, a, r, v, kk, lw, kd, a)


def _rw_out_kernel(x_ref, pos_ref, y0_ref, y1_ref, r_ref, v_ref, kd_ref, gate_ref, mod_ref, rk_ref, lng_ref, lnb_ref,
                   wo_ref, gsum_ref, gbc_ref, xo_ref):
    inv_n = 1.0 / RW_HEAD
    gsum = gsum_ref[...]
    gbc = gbc_ref[...]
    y = y0_ref[0] + y1_ref[0]
    mu = _dot_x2(_dot_x2(y, gsum) * inv_n, gbc)
    yc = y - mu
    var = _dot_x2(yc * yc, gsum) * inv_n
    yn = yc * _dot_x2(lax.rsqrt(var + RW_GN_EPS), gbc) * lng_ref[...] + lnb_ref[...]
    k_bonus = 0.5 * (kd_ref[0, 0] + kd_ref[1, 0])
    bonus = _dot_x2(_dot_x2(r_ref[0] * k_bonus * rk_ref[...], gsum), gbc) * v_ref[0]
    out = (yn + bonus) * gate_ref[0]
    o = _dot(out, wo_ref[...])
    xo_ref[0] = x_ref[0] + pos_ref[...] + mod_ref[0, 0, 2:3, :] * o


def _rw_out(xs, pos, y, r, v, kd, gate, mod, p, gsum, gbc, n_ctx):
    b, l, d = xs.shape
    tm = TOKEN_TILE
    nt = l // tm
    nct = n_ctx // tm
    tok = pl.BlockSpec((1, tm, d), lambda i, t: (i, t, 0))
    tok2 = pl.BlockSpec((2, 1, tm, d), lambda i, t: (0, i, t, 0))
    consts = (p["r_k"], p["ln_g"], p["ln_b"], p["w_o"], gsum, gbc)
    return pl.pallas_call(
        _rw_out_kernel,
        out_shape=jax.ShapeDtypeStruct((b, l, d), F32),
        grid=(b, nt),
        in_specs=[tok, pl.BlockSpec((tm, d), lambda i, t: (t, 0)), tok, tok, tok, tok, tok2, tok,
                  pl.BlockSpec((1, 1, 6, d), lambda i, t: (i, _seg_index(t, nct), 0, 0))]
                 + [_const_spec(a.shape) for a in consts],
        out_specs=tok,
        compiler_params=_params(2),
        name="rw_out",
    )(xs, pos, y[0], y[1], r, v, kd, gate, mod, *consts)


def _route(logits):
    ne = MOE_GROUPS * MOE_EXPERTS
    lane = lax.broadcasted_iota(jnp.int32, logits.shape, 1).astype(F32)
    far = float(4 * LANES)
    g_mask = jnp.logical_and(lane >= ne, lane < ne + MOE_GROUPS)
    gl = jnp.where(g_mask, logits, NEG_BIG)
    gmax = jnp.max(gl, axis=1, keepdims=True)
    gsum = jnp.sum(jnp.where(g_mask, jnp.exp(gl - gmax), 0.0), axis=1, keepdims=True)
    gp = 1.0 / gsum
    g_first = jnp.min(jnp.where(gl == gmax, lane, far), axis=1, keepdims=True) - ne
    in_grp = jnp.logical_and(lane >= g_first * MOE_EXPERTS, lane < (g_first + 1.0) * MOE_EXPERTS)
    el = jnp.where(in_grp, logits, NEG_BIG)
    m1 = jnp.max(el, axis=1, keepdims=True)
    i1 = jnp.min(jnp.where(el == m1, lane, far), axis=1, keepdims=True)
    el2 = jnp.where(lane == i1, NEG_BIG, el)
    m2 = jnp.max(el2, axis=1, keepdims=True)
    i2 = jnp.min(jnp.where(el2 == m2, lane, far), axis=1, keepdims=True)
    e2 = jnp.exp(m2 - m1)
    p1 = gp / (1.0 + e2)
    p2 = p1 * e2
    comb = jnp.where(lane == i1, p1, jnp.where(lane == i2, p2, 0.0))
    return jnp.where(lane == g_first + ne, 1.0, comb)


def _seg_index(t, n_ctx_tiles):
    return jnp.where(t >= n_ctx_tiles, 1, 0)


def _moe_route_kernel(x_ref, mod_ref, ng_ref, wr_ref, br_ref, h_ref):
    d = x_ref.shape[2]
    h = _norm_mod(x_ref[0], ng_ref[...], mod_ref[0, 0, 3:4, :], mod_ref[0, 0, 4:5, :])
    h_ref[0, :, 0:d] = _bf(h)
    hi, lo = _split2(_route(_dot_hh(h, wr_ref[...]) + br_ref[...]))
    h_ref[0, :, d:d + LANES] = hi
    h_ref[0, :, d + LANES:d + 2 * LANES] = lo


def _moe_route(xs, mod, ng, p, n_ctx):
    b, rows, d = xs.shape
    tm = TOKEN_TILE
    nct = n_ctx // tm
    return pl.pallas_call(
        _moe_route_kernel,
        out_shape=jax.ShapeDtypeStruct((b, rows, d + 2 * LANES), BF16),
        grid=(b, rows // tm),
        in_specs=[pl.BlockSpec((1, tm, d), lambda i, t: (i, t, 0)),
                  pl.BlockSpec((1, 1, 6, d), lambda i, t: (i, _seg_index(t, nct), 0, 0)),
                  _const_spec(ng.shape), _const_spec(p["w_r"].shape), _const_spec(p["b_r"].shape)],
        out_specs=pl.BlockSpec((1, tm, d + 2 * LANES), lambda i, t: (i, t, 0)),
        compiler_params=_params(2),
        name="moe_route",
    )(xs, mod, ng, p["w_r"], p["b_r"])


MOE_ROW_BLOCK = 128


def _moe_expert_kernel(h_ref, tri_ref, upper_ref, wgu_ref, wd_ref, m_ref, xs_s, acc_s, dest_s, seg_s, *, tb):
    rows = h_ref.shape[1]
    slots = xs_s.shape[0]
    d = acc_s.shape[1]
    ff = wd_ref.shape[1]
    ne = MOE_GROUPS * MOE_EXPERTS
    rb = MOE_ROW_BLOCK
    e = pl.program_id(1)
    n_tb = rows // tb

    @pl.when(e == 0)
    def _():
        lane = lax.broadcasted_iota(jnp.int32, (1, LANES), 1)
        g_lanes = jnp.logical_and(lane >= ne, lane < ne + MOE_GROUPS)
        carry = jnp.zeros((1, LANES), F32)
        ranks = []
        for i in range(n_tb):
            gind = jnp.where(g_lanes, h_ref[0, i * tb:(i + 1) * tb, d:d + LANES].astype(F32), 0.0)
            r = jnp.dot(tri_ref[...], _bf(gind), preferred_element_type=F32) + carry
            carry = r[tb - 1:tb, :]
            ranks.append((gind, r))
        padded = jnp.ceil(carry * (1.0 / rb)) * rb
        start = _dot_x3(jnp.broadcast_to(padded, (HALO, LANES)), upper_ref[...])[0:1]
        for g in range(MOE_GROUPS):
            pick = lane == ne + g
            seg_s[g] = jnp.sum(jnp.where(pick, start, 0.0)).astype(jnp.int32)
            seg_s[MOE_GROUPS + g] = jnp.sum(jnp.where(pick, padded, 0.0)).astype(jnp.int32)
        dest_rows = []
        for i, (gind, r) in enumerate(ranks):
            dest = jnp.sum(gind * (start + r - 1.0), axis=1, keepdims=True)
            dest_b = jnp.broadcast_to(dest, (tb, LANES))
            dest_s[i * tb:(i + 1) * tb, :] = dest_b
            dest_rows.append(dest_b.T[0:1, :])
        dest_row = jnp.concatenate(dest_rows, axis=1)
        hb = h_ref[0]

        def permute(sb, c_):
            base = pl.multiple_of(sb * tb, tb)
            slot = (lax.broadcasted_iota(jnp.int32, (tb, rows), 0) + base).astype(F32)
            perm = jnp.where(slot == dest_row, 1.0, 0.0)
            xs_s[pl.ds(base, tb), :] = _bf(jnp.dot(_bf(perm), hb, preferred_element_type=F32))
            acc_s[pl.ds(base, tb), :] = jnp.zeros((tb, d), F32)
            return c_
        lax.fori_loop(0, slots // tb, permute, 0)

    grp = e // MOE_EXPERTS
    start = seg_s[grp]
    n_rows = seg_s[MOE_GROUPS + grp]
    def expert_rows(firsts, size):
        sls = [pl.ds(pl.multiple_of(first, rb), size) for first in firsts]
        xbs = [xs_s[sl, :] for sl in sls]
        gus = [jnp.dot(xb[:, 0:d], wgu_ref[0], preferred_element_type=F32) for xb in xbs]
        acts = []
        for xb, gu in zip(xbs, gus):
            cs = xb[:, d:d + LANES].astype(F32) + xb[:, d + LANES:d + 2 * LANES].astype(F32)
            lane = lax.broadcasted_iota(jnp.int32, cs.shape, 1)
            ce = jnp.sum(jnp.where(lane == e, cs, 0.0), axis=1, keepdims=True)
            acts.append(_bf(_silu(gu[:, 0:ff]) * gu[:, ff:2 * ff] * ce))
        outs = [jnp.dot(act, wd_ref[0], preferred_element_type=F32) for act in acts]
        for sl, out in zip(sls, outs):
            acc_s[sl, :] += out

    big_rows = 2 * rb
    n_pairs = n_rows // (2 * big_rows)

    def pair(i, c_):
        first = start + i * (2 * big_rows)
        expert_rows([first, first + big_rows], big_rows)
        return c_
    lax.fori_loop(0, n_pairs, pair, 0)
    done = n_pairs * (2 * big_rows)
    rest = n_rows - done

    @pl.when(rest >= big_rows)
    def _():
        expert_rows([start + done], big_rows)

    @pl.when(rest % big_rows > 0)
    def _():
        expert_rows([start + done + (rest // big_rows) * big_rows], rb)

    @pl.when(e == pl.num_programs(1) - 1)
    def _():
        def narrow(sb, c_):
            sl = pl.ds(pl.multiple_of(sb * tb, tb), tb)
            xs_s[sl, 0:d] = _bf(acc_s[sl, :])
            return c_
        lax.fori_loop(0, slots // tb, narrow, 0)
        acc = xs_s[:, 0:d]
        for i in range(n_tb):
            slot = lax.broadcasted_iota(jnp.int32, (tb, slots), 1).astype(F32)
            unperm = jnp.where(slot == dest_s[i * tb:(i + 1) * tb, 0:1], 1.0, 0.0)
            m_ref[0, i * tb:(i + 1) * tb, :] = _bf(jnp.dot(_bf(unperm), acc, preferred_element_type=F32))


def _moe_experts(h, p):
    b, rows, da = h.shape
    ne, d, ff2 = p["w_gu"].shape
    ff = ff2 // 2
    tb = TOKEN_TILE
    slots = rows + tb * (-(-(MOE_GROUPS * MOE_ROW_BLOCK) // tb))
    tri = jnp.tril(jnp.ones((tb, tb), BF16))
    upper = jnp.triu(jnp.ones((LANES, LANES), BF16), 1)
    return pl.pallas_call(
        functools.partial(_moe_expert_kernel, tb=tb),
        out_shape=jax.ShapeDtypeStruct((b, rows, d), BF16),
        grid=(b, ne),
        in_specs=[pl.BlockSpec((1, rows, da), lambda i, e: (i, 0, 0), pipeline_mode=pl.Buffered(1)),
                  _const_spec(tri.shape), _const_spec(upper.shape),
                  pl.BlockSpec((1, d, ff2), lambda i, e: (e, 0, 0)),
                  pl.BlockSpec((1, ff, d), lambda i, e: (e, 0, 0))],
        out_specs=pl.BlockSpec((1, rows, d), lambda i, e: (i, 0, 0)),
        scratch_shapes=[pltpu.VMEM((slots, da), BF16), pltpu.VMEM((slots, d), F32),
                        pltpu.VMEM((rows, LANES), F32), pltpu.SMEM((2 * MOE_GROUPS,), jnp.int32)],
        compiler_params=_params(2),
        name="moe_experts",
    )(h, tri, upper, p["w_gu"], p["w_down"])


def _moe_resid_kernel(x_ref, m_ref, mod_ref, fg_ref, xo_ref, *, final_norm):
    xn = x_ref[0] + mod_ref[0, 0, 5:6, :] * m_ref[0].astype(F32)
    if final_norm:
        ms = jnp.mean(xn * xn, axis=-1, keepdims=True)
        xn = xn * lax.rsqrt(ms + NORM_EPS) * fg_ref[...]
    xo_ref[0] = xn


def _moe_resid(xs, m, mod, final_g, n_ctx, final_norm):
    b, rows, d = xs.shape
    tm = TOKEN_TILE
    nct = n_ctx // tm
    tok = pl.BlockSpec((1, tm, d), lambda i, t: (i, t, 0))
    return pl.pallas_call(
        functools.partial(_moe_resid_kernel, final_norm=final_norm),
        out_shape=jax.ShapeDtypeStruct((b, rows, d), F32),
        grid=(b, rows // tm),
        in_specs=[tok, tok, pl.BlockSpec((1, 1, 6, d), lambda i, t: (i, _seg_index(t, nct), 0, 0)),
                  _const_spec(final_g.shape)],
        out_specs=tok,
        compiler_params=_params(2),
        name="moe_resid",
    )(xs, m, mod, final_g)


def _moe(xs, mod, ng, p, final_g, n_ctx, final_norm):
    h = _moe_route(xs, mod, ng, p, n_ctx)
    m = _moe_experts(h, p)
    return _moe_resid(xs, m, mod, final_g, n_ctx, final_norm)


def _gd_proj_kernel(x_ref, xp_ref, xn_ref, mod_ref, ng_ref, w_ref, conv_ref, o_ref, *, tm, n_ctx, n_tot, n_qk_blk,
                    n_conv_blk):
    jb = pl.program_id(0)
    pos = pl.program_id(2) * tm
    sh = mod_ref[0, 0, 0:1, :]
    sc = mod_ref[0, 0, 1:2, :]
    g = ng_ref[...]
    prev_ok = jnp.logical_and(pos != 0, pos != n_ctx)
    next_ok = jnp.logical_and(pos + tm != n_ctx, pos + tm != n_tot)
    h = _norm_mod(x_ref[0], g, sh, sc)
    hp = jnp.where(prev_ok, _norm_mod(xp_ref[0], g, sh, sc), 0.0)
    hn = jnp.where(next_ok, _norm_mod(xn_ref[0], g, sh, sc), 0.0)
    ext = _dot(jnp.concatenate([hp, h, hn], axis=0), w_ref[...])
    n_ext = tm + 2 * HALO

    @pl.when(jb >= n_conv_blk)
    def _():
        o_ref[0] = ext[HALO:HALO + tm]

    @pl.when(jb < n_conv_blk)
    def _():
        pad = (GD_CONV_W - 1) // 2
        is_qk = jb < n_qk_blk
        scale = jnp.where(jb < n_qk_blk // 2, GD_DK ** -0.5, 1.0)
        for hh in range(ext.shape[1] // GD_DK):
            cs = slice(hh * GD_DK, (hh + 1) * GD_DK)
            e = ext[:, cs]
            acc = e[HALO:HALO + tm] * conv_ref[pad:pad + 1, cs]
            for wi in range(GD_CONV_W):
                if wi == pad:
                    continue
                shifted = pltpu.roll(e, (pad - wi) % n_ext, axis=0)
                acc = acc + shifted[HALO:HALO + tm] * conv_ref[wi:wi + 1, cs]
            act = _silu(acc)
            ss = jnp.sum(act * act, axis=-1, keepdims=True)
            o_ref[0, :, cs] = act * jnp.where(is_qk, lax.rsqrt(ss + NORM_EPS) * scale, 1.0)


def _gd_proj(xs, mod, ng, w_in, conv_w, n_ctx, kd_total, qkv_total):
    b, l, d = xs.shape
    n_out = w_in.shape[1]
    tm = TOKEN_TILE
    nb = 1024
    nt = l // tm
    nct = n_ctx // tm
    nj = n_out // nb
    n_conv_blk = qkv_total // nb
    kern = functools.partial(_gd_proj_kernel, tm=tm, n_ctx=n_ctx, n_tot=l, n_qk_blk=2 * kd_total // nb,
                             n_conv_blk=n_conv_blk)
    return pl.pallas_call(
        kern,
        out_shape=jax.ShapeDtypeStruct((b, l, n_out), F32),
        grid=(nj, b, nt),
        in_specs=_tile_specs(tm, d, nt, lead=(0,))
                 + [pl.BlockSpec((1, 1, 6, d), lambda j, i, t: (i, _seg_index(t, nct), 0, 0)),
                    _const_spec(ng.shape),
                    pl.BlockSpec((d, nb), lambda j, i, t: (0, j)),
                    pl.BlockSpec((GD_CONV_W, nb), lambda j, i, t: (0, jnp.minimum(j, n_conv_blk - 1)))],
        out_specs=pl.BlockSpec((1, tm, nb), lambda j, i, t: (i, t, j)),
        compiler_params=_params(3),
        name="gd_proj",
    )(xs, xs, xs, mod, ng, w_in, conv_w)


def _gd_ab_kernel(x_ref, mod_ref, ng_ref, wab_ref, alog_ref, dtb_ref, o_ref, *, n_vh):
    h = _norm_mod(x_ref[0], ng_ref[...], mod_ref[0, 0, 0:1, :], mod_ref[0, 0, 1:2, :])
    ab = _dot_hh(h, wab_ref[...])
    lane = lax.broadcasted_iota(jnp.int32, (1, LANES), 1)
    for z in range(2):
        abz = ab if z == 0 else pltpu.roll(ab, LANES - 2 * n_vh, axis=1)
        gdec = -jnp.exp(alog_ref[z:z + 1, :]) * _softplus(abz + dtb_ref[z:z + 1, :])
        beta = _sigmoid(abz)
        o_ref[z, 0] = jnp.where(lane < n_vh, gdec, jnp.where(lane < 2 * n_vh, beta, 0.0))


def _gd_ab(xs, mod, ng, wab, alog, dtb, n_ctx, n_vh):
    b, l, d = xs.shape
    tm = TOKEN_TILE
    nt = l // tm
    nct = n_ctx // tm
    return pl.pallas_call(
        functools.partial(_gd_ab_kernel, n_vh=n_vh),
        out_shape=jax.ShapeDtypeStruct((2, b, l, LANES), F32),
        grid=(b, nt),
        in_specs=[pl.BlockSpec((1, tm, d), lambda i, t: (i, t, 0)),
                  pl.BlockSpec((1, 1, 6, d), lambda i, t: (i, _seg_index(t, nct), 0, 0)),
                  _const_spec(ng.shape), _const_spec(wab.shape), _const_spec(alog.shape), _const_spec(dtb.shape)],
        out_specs=pl.BlockSpec((2, 1, tm, LANES), lambda i, t: (0, i, t, 0)),
        compiler_params=_params(2),
        name="gd_ab",
    )(xs, mod, ng, wab, alog, dtb)


def _gd_chunk_kernel(q0_ref, k0_ref, v0_ref, gb0_ref, q1_ref, k1_ref, v1_ref, gb1_ref,
                     e64g_ref, e64b_ref, e128g_ref, e128b_ref, o0_ref, o1_ref, s_ref, *, n_vh, heads_per_group):
    c = q0_ref.shape[1]
    n = heads_per_group
    rep = n_vh // (q0_ref.shape[2] // GD_DK)
    n_groups = n_vh // n
    gw = n * c
    kh_per_group = n // rep

    @pl.when(pl.program_id(1) == 0)
    def _():
        s_ref[...] = jnp.zeros_like(s_ref)

    bd = _block_mask(gw, gw, c, c)
    bd_k = _block_mask(gw, kh_per_group * GD_DK, c, GD_DK, rdiv=rep)
    bd_v = _block_mask(gw, n * GD_DV, c, GD_DV)

    def bdiag_v(x):
        return jnp.where(bd_v, _tile_rows(x, n), 0.0)

    refs = ((q0_ref, k0_ref, v0_ref, gb0_ref, o0_ref), (q1_ref, k1_ref, v1_ref, gb1_ref, o1_ref))
    prep = []
    for z, (q_ref, k_ref, v_ref, gb_ref, _) in enumerate(refs):
        gb = gb_ref[0, 0]
        gc = _cumsum_rows(gb, z == 1)
        gtot = jnp.sum(gb, axis=0, keepdims=True)
        incl_all, _, eye_all = _order_masks(z, c, n_vh)
        gt64 = _dot_x3(gc, e64g_ref[...])
        gs64 = _dot_ex3(jnp.ones((c, c), F32), gt64 * eye_all)
        prep.append({
            "gam": jnp.where(incl_all, jnp.exp(jnp.where(incl_all, gt64 - gs64, 0.0)), 0.0),
            "bt64": _dot(gb, e64b_ref[...]),
            "e_g": _dot(jnp.exp(gc), e128g_ref[...]),
            "e_end": _dot(jnp.exp(gtot - gc), e128g_ref[...]),
            "beta": _dot(gb, e128b_ref[...]),
            "gl": jnp.exp(gtot),
            "q": q_ref[0], "k": k_ref[0], "v": v_ref[0], "masks": _order_masks(z, c, n)})

    chains = [(z, gi) for z in range(2) for gi in range(n_groups)]

    def ksl(gi):
        return slice(gi * kh_per_group * GD_DK, (gi + 1) * kh_per_group * GD_DK)

    def csl(gi):
        return slice(gi * gw, (gi + 1) * gw)

    def vsl(gi):
        return slice(gi * n * GD_DV, (gi + 1) * n * GD_DV)

    def per_vhead(x):
        return jnp.concatenate([x[:, (hh // rep) * GD_DK:(hh // rep + 1) * GD_DK] for hh in range(n)], axis=1)

    kg_ = [prep[z]["k"][:, ksl(gi)] for z, gi in chains]
    qg_ = [prep[z]["q"][:, ksl(gi)] for z, gi in chains]
    qkk = [_dot_nt(jnp.concatenate([kx, qx], axis=0), jnp.where(bd_k, _tile_rows(kx, n), 0.0))
           for kx, qx in zip(kg_, qg_)]
    a_mat = [jnp.where(prep[z]["masks"][1], x[:c] * prep[z]["gam"][:, csl(gi)] * prep[z]["bt64"][:, csl(gi)], 0.0)
             for x, (z, gi) in zip(qkk, chains)]
    aqk = [jnp.where(prep[z]["masks"][0], x[c:] * prep[z]["gam"][:, csl(gi)], 0.0) for x, (z, gi) in zip(qkk, chains)]
    t_inv = _tri_inv([-a for a in a_mat], bd, [prep[z]["masks"][2] for z, _ in chains])

    k2 = [per_vhead(x) for x in kg_]
    q2 = [per_vhead(x) for x in qg_]
    bg = [prep[z]["beta"][:, vsl(gi)] for z, gi in chains]
    eg = [prep[z]["e_g"][:, vsl(gi)] for z, gi in chains]
    u = [_dot(t, bdiag_v(prep[z]["v"][:, vsl(gi)] * b_)) for t, b_, (z, gi) in zip(t_inv, bg, chains)]
    w = [_dot(t, bdiag_v(kx * b_ * e_)) for t, kx, b_, e_ in zip(t_inv, k2, bg, eg)]
    qe = [qx * e_ for qx, e_ in zip(q2, eg)]
    ke = [kx * prep[z]["e_end"][:, vsl(gi)] for kx, (z, gi) in zip(k2, chains)]

    heads = [(ci, hh) for ci in range(len(chains)) for hh in range(n)]

    def hs(hh):
        return slice(hh * GD_DV, (hh + 1) * GD_DV)

    def state_index(ci, hh):
        z, gi = chains[ci]
        return z, gi * n + hh

    ws = {(ci, hh): _dot(jnp.concatenate([w[ci][:, hs(hh)], qe[ci][:, hs(hh)]], axis=0), s_ref[state_index(ci, hh)])
          for ci, hh in heads}
    vn = [jnp.concatenate([u[ci][:, hs(hh)] - ws[ci, hh][:c] for hh in range(n)], axis=1) for ci in range(len(chains))]
    for ci, (z, gi) in enumerate(chains):
        pre = jnp.concatenate([ws[ci, hh][c:] for hh in range(n)], axis=1)
        refs[z][4][0, :, vsl(gi)] = pre + _dot(aqk[ci], bdiag_v(vn[ci]))
    for ci, hh in heads:
        z, hv = state_index(ci, hh)
        gl = jnp.broadcast_to(prep[z]["gl"][:, hv:hv + 1], (1, GD_DV))
        s_ref[z, hv] = s_ref[z, hv] * gl + _dot_tn(ke[ci][:, hs(hh)], vn[ci][:, hs(hh)])


def _gd_chunk(proj, gb, consts, n_ctx, kd_total, vd_total):
    b, l, _ = proj.shape
    c = CHUNK
    nc = l // c
    ncc = n_ctx // c
    n_vh = vd_total // GD_DV

    def col(z, width, blk):
        return pl.BlockSpec((1, c, width), lambda i, j: (i, _chunk_index(z, j, ncc, nc), blk))

    def perdir(z):
        return pl.BlockSpec((1, 1, c, LANES), lambda i, j: (z, i, _chunk_index(z, j, ncc, nc), 0))

    def dir_specs(z):
        return [col(z, kd_total, 0), col(z, kd_total, 1), col(z, vd_total, 2 * kd_total // vd_total), perdir(z)]

    sd = jax.ShapeDtypeStruct((b, l, vd_total), F32)
    return pl.pallas_call(
        functools.partial(_gd_chunk_kernel, n_vh=n_vh, heads_per_group=4),
        out_shape=(sd, sd),
        grid=(b, nc),
        in_specs=dir_specs(0) + dir_specs(1) + [_const_spec(a.shape) for a in consts],
        out_specs=(col(0, vd_total, 0), col(1, vd_total, 0)),
        scratch_shapes=[pltpu.VMEM((2, n_vh, GD_DK, GD_DV), F32)],
        compiler_params=_params(2),
        name="gd_chunk",
    )(proj, proj, proj, gb, proj, proj, proj, gb, *consts)


def _gd_out_kernel(x_ref, o0_ref, o1_ref, z_ref, mod_ref, ngd_ref, wo_ref, xo_ref):
    o = o0_ref[0] + o1_ref[0]
    z = z_ref[0]
    parts = []
    for hh in range(o.shape[1] // GD_DV):
        seg = o[:, hh * GD_DV:(hh + 1) * GD_DV]
        ms = jnp.mean(seg * seg, axis=-1, keepdims=True)
        parts.append(seg * lax.rsqrt(ms + NORM_EPS) * ngd_ref[...])
    on = jnp.concatenate(parts, axis=1) * _silu(z)
    xo_ref[0] = x_ref[0] + mod_ref[0, 0, 2:3, :] * _dot(on, wo_ref[...])


def _gd_out(xs, o, proj, mod, ngd, w_o, n_ctx, z_blk):
    b, l, d = xs.shape
    vd = o[0].shape[-1]
    tm = TOKEN_TILE
    nct = n_ctx // tm
    nt = (l - n_ctx) // tm
    return pl.pallas_call(
        _gd_out_kernel,
        out_shape=jax.ShapeDtypeStruct((b, l - n_ctx, d), F32),
        grid=(b, nt),
        in_specs=[pl.BlockSpec((1, tm, d), lambda i, t: (i, t + nct, 0)),
                  pl.BlockSpec((1, tm, vd), lambda i, t: (i, t + nct, 0)),
                  pl.BlockSpec((1, tm, vd), lambda i, t: (i, t + nct, 0)),
                  pl.BlockSpec((1, tm, vd), lambda i, t: (i, t + nct, z_blk)),
                  pl.BlockSpec((1, 1, 6, d), lambda i, t: (i, 1, 0, 0)),
                  _const_spec(ngd.shape), _const_spec(w_o.shape)],
        out_specs=pl.BlockSpec((1, tm, d), lambda i, t: (i, t, 0)),
        compiler_params=_params(2),
        name="gd_out",
    )(xs, o[0], o[1], proj, mod, ngd, w_o)


def _pos_embed_2d(rows, d):
    quarter = d // 4
    omega = 1.0 / (POS_BASE ** (jnp.arange(quarter, dtype=F32) / quarter))

    def axis_emb(n):
        ang = jnp.arange(n, dtype=F32)[:, None] * omega[None, :]
        return jnp.concatenate([jnp.sin(ang), jnp.cos(ang)], axis=-1)

    e_row = jnp.broadcast_to(axis_emb(rows)[:, None, :], (rows, GRID_W, d // 2))
    e_col = jnp.broadcast_to(axis_emb(GRID_W)[None, :, :], (rows, GRID_W, d // 2))
    return jnp.concatenate([e_row, e_col], axis=-1).reshape(rows * GRID_W, d)


def _head_indicator(d, head):
    ch = jnp.arange(d)[:, None] // head
    ind = (ch == jnp.arange(LANES)[None, :]).astype(BF16)
    return ind, ind.T


def _expand_matrix(offset, n_heads, width):
    src = jnp.arange(LANES)[:, None]
    dst = jnp.arange(n_heads * width)[None, :] // width
    return (src == dst + offset).astype(BF16)


def _block_diag2(m):
    z = jnp.zeros_like(m[0])
    return jnp.concatenate([jnp.concatenate([m[0], z], axis=1), jnp.concatenate([z, m[1]], axis=1)], axis=0)


def _moe_params(i, moe_w_rg, moe_b_rg, moe_w_re, moe_b_re, moe_w_gate, moe_w_up, moe_w_down):
    d = moe_w_rg.shape[1]
    ne = MOE_GROUPS * MOE_EXPERTS
    pad = LANES - ne - MOE_GROUPS
    w_r = jnp.concatenate([moe_w_re[i], moe_w_rg[i], jnp.zeros((d, pad), F32)], axis=1)
    b_r = jnp.concatenate([moe_b_re[i], moe_b_rg[i], jnp.zeros((pad,), F32)])[None, :]
    ff = moe_w_gate.shape[-1]
    return {"w_r": w_r, "b_r": b_r,
            "w_gu": jnp.concatenate([_bf(moe_w_gate[i]), _bf(moe_w_up[i])], axis=-1).reshape(ne, d, 2 * ff),
            "w_down": _bf(moe_w_down[i]).reshape(ne, ff, d)}


def kernel(x, c, ctx, c_ctx, ada_w, ada_b, norm1_g, norm2_g, rw_mix, rw_w_rkv, rw_w0, rw_w1, rw_w2, rw_a0, rw_a1, rw_a2, rw_g1, rw_g2, rw_k_k, rw_k_a, rw_r_k, rw_ln_g, rw_ln_b, rw_w_o, gd_w_in, gd_conv, gd_w_ab, gd_a_log, gd_dt_bias, gd_norm_g, gd_w_o, moe_w_rg, moe_b_rg, moe_w_re, moe_b_re, moe_w_gate, moe_w_up, moe_w_down, final_g):
    bsz, n_lat, d = x.shape
    n_ctx = ctx.shape[1]
    assert n_ctx % TOKEN_TILE == 0 and n_lat % TOKEN_TILE == 0 and n_ctx % CHUNK == 0
    assert ada_w.shape[0] == 2 and d % (4 * RW_HEAD) == 0

    rows = -(-(bsz + 1) // HALO) * HALO
    cs = jnp.zeros((rows, d), F32).at[:bsz].set(c).at[bsz].set(c_ctx)
    mod_all = _ada(cs, ada_w, ada_b)

    def mod_of(i):
        lat = mod_all[i, :bsz].reshape(bsz, 1, 6, d)
        cx = jnp.broadcast_to(mod_all[i, bsz].reshape(1, 1, 6, d), (bsz, 1, 6, d))
        return jnp.concatenate([cx, lat], axis=1)

    xs = jnp.concatenate([ctx, x], axis=1)
    pos = jnp.concatenate([jnp.zeros((n_ctx, d), F32), _pos_embed_2d(n_lat // GRID_W, d)], axis=0)
    gsum, gbc = _head_indicator(d, RW_HEAD)

    mod0 = mod_of(0)
    ng1 = norm1_g[0][None, :]
    rw = {"mix": rw_mix[0], "w_rkv": _bf(rw_w_rkv[0]), "w0": rw_w0[0],
          "w1": _bf(jnp.concatenate([rw_w1[0, 0], rw_w1[0, 1]], axis=1)), "w2": _bf(_block_diag2(rw_w2[0])),
          "a0": rw_a0[0],
          "a1": _bf(jnp.concatenate([rw_a1[0, 0], rw_a1[0, 1]], axis=1)), "a2": _bf(_block_diag2(rw_a2[0])),
          "g1": _bf(rw_g1[0]), "g2": _bf(rw_g2[0]), "k_k": rw_k_k[0][None, :], "k_a": rw_k_a[0][None, :],
          "r_k": rw_r_k[0].reshape(1, d), "ln_g": rw_ln_g[0][None, :], "ln_b": rw_ln_b[0][None, :],
          "w_o": _bf(rw_w_o[0])}
    r, v, kk, gate, lw, kd, ar = _rw_feat(xs, pos, mod0, ng1, rw, gsum, gbc, n_ctx)
    y = _rw_scan(r, v, kk, lw, kd, ar, n_ctx)
    xs = _rw_out(xs, pos, y, r, v, kd, gate, mod0, rw, gsum, gbc, n_ctx)
    moe_args = (moe_w_rg, moe_b_rg, moe_w_re, moe_b_re, moe_w_gate, moe_w_up, moe_w_down)
    fg = final_g[None, :]
    xs = _moe(xs, mod0, norm2_g[0][None, :], _moe_params(0, *moe_args), fg, n_ctx, False)

    mod1 = mod_of(1)
    n_vh = gd_a_log.shape[-1]
    vd_total = n_vh * GD_DV
    kd_total = (gd_w_in.shape[-1] - 2 * vd_total) // 2
    qkv_total = 2 * kd_total + vd_total
    ng1 = norm1_g[1][None, :]
    proj = _gd_proj(xs, mod1, ng1, _bf(gd_w_in[0]), gd_conv[0], n_ctx, kd_total, qkv_total)
    wab = jnp.concatenate([gd_w_ab[0, 0], gd_w_ab[0, 1], jnp.zeros((d, LANES - 4 * n_vh), F32)], axis=1)
    lane_pad = jnp.zeros((2, LANES - n_vh), F32)
    alog = jnp.concatenate([gd_a_log[0], lane_pad], axis=1)
    dtb = jnp.concatenate([gd_dt_bias[0], lane_pad], axis=1)
    gb = _gd_ab(xs, mod1, ng1, wab, alog, dtb, n_ctx, n_vh)
    consts = (_expand_matrix(0, n_vh, CHUNK), _expand_matrix(n_vh, n_vh, CHUNK),
              _expand_matrix(0, n_vh, GD_DV), _expand_matrix(n_vh, n_vh, GD_DV))
    o = _gd_chunk(proj, gb, consts, n_ctx, kd_total, vd_total)
    ngd = gd_norm_g[0][None, :]
    x_lat = _gd_out(xs, o, proj, mod1, ngd, _bf(gd_w_o[0]), n_ctx, qkv_total // vd_total)
    return _moe(x_lat, mod1, norm2_g[1][None, :], _moe_params(1, *moe_args), fg, 0, True)
```

```python
import functools
import math

import jax
import jax.numpy as jnp
from jax import lax
from jax.experimental import pallas as pl
from jax.experimental.pallas import tpu as pltpu

F32 = jnp.float32
BF16 = jnp.bfloat16

NORM_EPS = 1e-6
RW_GN_EPS = 64e-5
POS_BASE = 10000.0
GRID_W = 64
RW_HEAD = 64
RW_LORA = 64
GD_DK = 128
GD_DV = 128
GD_CONV_W = 5
MOE_GROUPS = 4
MOE_EXPERTS = 8
CHUNK = 64
LANES = 128
HALO = 8
TOKEN_TILE = 256
VMEM_LIMIT = 56 * 1024 * 1024
NEG_BIG = -1e30


def _bf(x):
    return x.astype(BF16)


def _dot(a, b):
    return jnp.dot(_bf(a), _bf(b), preferred_element_type=F32)


def _dot_nt(a, b):
    return lax.dot_general(_bf(a), _bf(b), (((1,), (1,)), ((), ())), preferred_element_type=F32)


def _dot_tn(a, b):
    return lax.dot_general(_bf(a), _bf(b), (((0,), (0,)), ((), ())), preferred_element_type=F32)


def _split2(x):
    hi = x.astype(BF16)
    lo = (x - hi.astype(F32)).astype(BF16)
    return hi, lo


def _split3(x):
    hi = x.astype(BF16)
    r1 = x - hi.astype(F32)
    mid = r1.astype(BF16)
    lo = (r1 - mid.astype(F32)).astype(BF16)
    return hi, mid, lo


def _dot_x2(a, b_exact):
    hi, lo = _split2(a)
    b = _bf(b_exact)
    return (jnp.dot(hi, b, preferred_element_type=F32) + jnp.dot(lo, b, preferred_element_type=F32))


def _dot_x3(a, b_exact):
    hi, mid, lo = _split3(a)
    b = _bf(b_exact)
    return (jnp.dot(hi, b, preferred_element_type=F32) + jnp.dot(mid, b, preferred_element_type=F32)
            + jnp.dot(lo, b, preferred_element_type=F32))


def _dot_ex3(a_exact, b):
    hi, mid, lo = _split3(b)
    a = _bf(a_exact)
    return (jnp.dot(a, hi, preferred_element_type=F32) + jnp.dot(a, mid, preferred_element_type=F32)
            + jnp.dot(a, lo, preferred_element_type=F32))


def _dot_hh(a, b):
    ah, al = _split2(a)
    bh, bl = _split2(b)
    return (jnp.dot(ah, bh, preferred_element_type=F32) + jnp.dot(al, bh, preferred_element_type=F32)
            + jnp.dot(ah, bl, preferred_element_type=F32))


def _sigmoid(x):
    return 1.0 / (1.0 + jnp.exp(-x))


def _silu(x):
    return x * _sigmoid(x)


def _softplus(x):
    return jnp.maximum(x, 0.0) + jnp.log(1.0 + jnp.exp(-jnp.abs(x)))


def _norm_mod(x, g, shift, scale):
    ms = jnp.mean(x * x, axis=-1, keepdims=True)
    return (x * lax.rsqrt(ms + NORM_EPS) * g) * (1.0 + scale) + shift


def _tile_rows(x, n):
    return jnp.concatenate([x] * n, axis=0)


def _tile_lanes(x, n):
    return jnp.concatenate([x] * n, axis=1)


def _order_masks(direction, c, n=1):
    row = lax.broadcasted_iota(jnp.int32, (c, n * c), 0)
    col = lax.broadcasted_iota(jnp.int32, (c, n * c), 1) % c
    diff = (col - row) * (1 - 2 * direction)
    return diff <= 0, diff < 0, jnp.where(diff == 0, 1.0, 0.0)


def _cumsum_rows(x, reverse):
    n = x.shape[0]
    row = lax.broadcasted_iota(jnp.int32, (n, 1), 0)
    s = 1
    while s < n:
        if reverse:
            shifted = jnp.where(row < n - s, pltpu.roll(x, n - s, axis=0), 0.0)
        else:
            shifted = jnp.where(row >= s, pltpu.roll(x, s, axis=0), 0.0)
        x = x + shifted
        s *= 2
    return x


def _block_mask(rows, cols, rblk, cblk, rdiv=1):
    r = lax.broadcasted_iota(jnp.int32, (rows, cols), 0) // rblk
    c = lax.broadcasted_iota(jnp.int32, (rows, cols), 1) // cblk
    return (r // rdiv) == c if rdiv != 1 else r == c


TRI_BASE = 4
SCAN_BATCH = 2


def _tri_inv(a_list, bd, eye_list, rev_list):
    c = a_list[0].shape[0]
    n = a_list[0].shape[1] // c
    row = lax.broadcasted_iota(jnp.int32, a_list[0].shape, 0)
    col = lax.broadcasted_iota(jnp.int32, a_list[0].shape, 1) % c

    def same_block(m):
        return (row // m) == (col // m)

    def mm(x, y):
        return jnp.dot(_bf(x), _bf(jnp.where(bd, _tile_rows(y, n), 0.0)), preferred_element_type=F32)

    base = same_block(TRI_BASE)
    a0 = [jnp.where(base, a, 0.0) for a in a_list]
    x = [e + a for e, a in zip(eye_list, a0)]
    p = [mm(a, a) for a in a0]
    base_levels = int(math.log2(TRI_BASE)) - 1
    for lvl in range(base_levels):
        x = [xi + mm(pi, xi) for xi, pi in zip(x, p)]
        if lvl + 1 < base_levels:
            p = [mm(pi, pi) for pi in p]
    m = TRI_BASE
    while m < c:
        off = jnp.logical_and(same_block(2 * m), jnp.logical_not(same_block(m)))
        a_off = [jnp.where(off, a, 0.0) for a in a_list]
        if m % HALO:
            t = [mm(ao, xi) for ao, xi in zip(a_off, x)]
            x = [xi + mm(xi, ti) for xi, ti in zip(x, t)]
        else:
            zeros = jnp.zeros((m, a_list[0].shape[1]), F32)

            def pick(v, rev):
                return jnp.concatenate([v[(2 * i + (0 if rev else 1)) * m:(2 * i + (0 if rev else 1) + 1) * m]
                                        for i in range(c // (2 * m))], axis=0)

            def place(v, rev):
                parts = []
                for i in range(c // (2 * m)):
                    blk = v[i * m:(i + 1) * m]
                    parts += [blk, zeros] if rev else [zeros, blk]
                return jnp.concatenate(parts, axis=0)

            t = [place(mm(pick(ao, rev), xi), rev) for ao, xi, rev in zip(a_off, x, rev_list)]
            x = [xi + place(mm(pick(xi, rev), ti), rev) for xi, ti, rev in zip(x, t, rev_list)]
        m *= 2
    return x


def _chunk_index(direction, j, n_ctx_chunks, n_chunks):
    bwd = jnp.where(j < n_ctx_chunks, n_ctx_chunks - 1 - j, n_chunks + n_ctx_chunks - 1 - j)
    return jnp.where(direction == 0, j, bwd)


def _const_spec(shape):
    nd = len(shape)
    return pl.BlockSpec(shape, lambda *_: (0,) * nd, pipeline_mode=pl.Buffered(1))


def _params(n_axes):
    return pltpu.CompilerParams(dimension_semantics=("arbitrary",) * n_axes, vmem_limit_bytes=VMEM_LIMIT)


def _ada_kernel(cs_ref, w_ref, b_ref, o_ref):
    s = _silu(cs_ref[...])
    o_ref[0] = _dot_hh(s, w_ref[0]) + b_ref[0]


def _ada(cs, ada_w, ada_b):
    n_layers, d, n6 = ada_w.shape
    rows = cs.shape[0]
    tn = 1536
    return pl.pallas_call(
        _ada_kernel,
        out_shape=jax.ShapeDtypeStruct((n_layers, rows, n6), F32),
        grid=(n_layers, n6 // tn),
        in_specs=[pl.BlockSpec((rows, d), lambda i, j: (0, 0)),
                  pl.BlockSpec((1, d, tn), lambda i, j: (i, 0, j)),
                  pl.BlockSpec((1, 1, tn), lambda i, j: (i, 0, j))],
        out_specs=pl.BlockSpec((1, rows, tn), lambda i, j: (i, 0, j)),
        compiler_params=_params(2),
        name="ada",
    )(cs, ada_w, ada_b.reshape(n_layers, 1, n6))


def _shifted_neighbours(h, h_prev, h_next, pos, tm, n_ctx, n_tot):
    prev_ok = jnp.logical_and(pos != 0, pos != n_ctx)
    next_ok = jnp.logical_and(pos + tm != n_ctx, pos + tm != n_tot)
    h_prev = jnp.where(prev_ok, h_prev, 0.0)
    h_next = jnp.where(next_ok, h_next, 0.0)
    row = lax.broadcasted_iota(jnp.int32, (tm, 1), 0)
    prev = jnp.where(row == 0, h_prev, pltpu.roll(h, 1, axis=0))
    nxt = jnp.where(row == tm - 1, h_next, pltpu.roll(h, tm - 1, axis=0))
    return prev, nxt


def _tile_specs(tm, d, n_tiles, lead=()):
    nl = len(lead)
    hb = tm // HALO
    last = n_tiles * hb - 1

    def cur(*g):
        return (g[nl], g[nl + 1], 0)

    def prv(*g):
        return (g[nl], jnp.maximum(g[nl + 1] * hb - 1, 0), 0)

    def nxt(*g):
        return (g[nl], jnp.minimum((g[nl + 1] + 1) * hb, last), 0)

    return [pl.BlockSpec((1, tm, d), cur), pl.BlockSpec((1, HALO, d), prv), pl.BlockSpec((1, HALO, d), nxt)]


def _rw_feat_kernel(x_ref, xp_ref, xn_ref, pos_ref, posp_ref, posn_ref, mod_ref, ng_ref, mix_ref,
                    wrkv_ref, w0_ref, w1_ref, w2_ref, a0_ref, a1_ref, a2_ref, g1_ref, g2_ref,
                    kkp_ref, kap_ref, gsum_ref, gbc_ref,
                    r_o, v_o, kk_o, gate_o, lw_o, kd_o, a_o, *, tm, n_ctx, n_tot):
    d = x_ref.shape[-1]
    pos = pl.program_id(1) * tm
    sh = mod_ref[0, 0, 0:1, :]
    sc = mod_ref[0, 0, 1:2, :]
    g = ng_ref[...]
    h = _norm_mod(x_ref[0] + pos_ref[...], g, sh, sc)
    hp = _norm_mod(xp_ref[0, HALO - 1:HALO, :] + posp_ref[HALO - 1:HALO, :], g, sh, sc)
    hn = _norm_mod(xn_ref[0, 0:1, :] + posn_ref[0:1, :], g, sh, sc)
    prev, nxt = _shifted_neighbours(h, hp, hn, pos, tm, n_ctx, n_tot)
    xx = 0.5 * (prev + nxt) - h
    mix = mix_ref[...]
    xr = h + xx * mix[0:1]
    xw = h + xx * mix[1:2]
    xk = h + xx * mix[2:3]
    xv = h + xx * mix[3:4]
    xa = h + xx * mix[4:5]
    xg = h + xx * mix[5:6]

    r_o[0] = _dot(xr, wrkv_ref[0])
    v_o[0] = _dot(xv, wrkv_ref[2])
    gate_o[0] = _dot(_sigmoid(_dot(xg, g1_ref[...])), g2_ref[...])
    k = _dot(xk, wrkv_ref[1])

    kq = k * kkp_ref[...]
    ssq = _dot_x2(kq * kq, gsum_ref[...])
    kk_o[0] = kq * _dot_x2(lax.rsqrt(ssq + NORM_EPS), gbc_ref[...])

    wl = _dot(jnp.tanh(_dot(xw, w1_ref[...])), w2_ref[...])
    al = _dot(_dot(xa, a1_ref[...]), a2_ref[...])
    ka = kap_ref[...]
    for z in range(2):
        zw = w0_ref[z:z + 1, :] + wl[:, z * d:(z + 1) * d]
        lw_o[z, 0] = (-math.exp(-0.5)) * _sigmoid(zw)
        rate = _sigmoid(a0_ref[z:z + 1, :] + al[:, z * d:(z + 1) * d])
        a_o[z, 0] = rate
        kd_o[z, 0] = k * (1.0 + (rate - 1.0) * ka)


def _rw_feat(xs, pos, mod, ng, p, gsum, gbc, n_ctx):
    b, l, d = xs.shape
    tm = TOKEN_TILE
    nt = l // tm
    nct = n_ctx // tm
    hb = tm // HALO
    kern = functools.partial(_rw_feat_kernel, tm=tm, n_ctx=n_ctx, n_tot=l)
    tok = pl.BlockSpec((1, tm, d), lambda i, t: (i, t, 0))
    tok2 = pl.BlockSpec((2, 1, tm, d), lambda i, t: (0, i, t, 0))
    pos_specs = [pl.BlockSpec((tm, d), lambda i, t: (t, 0)),
                 pl.BlockSpec((HALO, d), lambda i, t: (jnp.maximum(t * hb - 1, 0), 0)),
                 pl.BlockSpec((HALO, d), lambda i, t: (jnp.minimum((t + 1) * hb, nt * hb - 1), 0))]
    in_specs = (_tile_specs(tm, d, nt) + pos_specs
                + [pl.BlockSpec((1, 1, 6, d), lambda i, t: (i, _seg_index(t, nct), 0, 0))]
                + [_const_spec(a.shape) for a in (ng, p["mix"], p["w_rkv"], p["w0"], p["w1"], p["w2"], p["a0"],
                                                  p["a1"], p["a2"], p["g1"], p["g2"], p["k_k"], p["k_a"], gsum, gbc)])
    sd = jax.ShapeDtypeStruct((b, l, d), F32)
    sd2 = jax.ShapeDtypeStruct((2, b, l, d), F32)
    return pl.pallas_call(
        kern,
        out_shape=(sd, sd, sd, sd, sd2, sd2, sd2),
        grid=(b, nt),
        in_specs=in_specs,
        out_specs=(tok, tok, tok, tok, tok2, tok2, tok2),
        compiler_params=_params(2),
        name="rw_feat",
    )(xs, xs, xs, pos, pos, pos, mod, ng, p["mix"], p["w_rkv"], p["w0"], p["w1"], p["w2"], p["a0"], p["a1"], p["a2"],
      p["g1"], p["g2"], p["k_k"], p["k_a"], gsum, gbc)


def _rw_scan_kernel(r0_ref, v0_ref, kk0_ref, lw0_ref, kd0_ref, a0_ref,
                    r1_ref, v1_ref, kk1_ref, lw1_ref, kd1_ref, a1_ref,
                    y0_ref, y1_ref, s_ref, *, heads_per_group):
    c = r0_ref.shape[1]
    d = r0_ref.shape[2]
    n = heads_per_group
    gw = n * RW_HEAD
    n_groups = d // gw
    nt = (((1,), (1,)), ((), ()))

    @pl.when(pl.program_id(1) == 0)
    def _():
        s_ref[...] = jnp.zeros_like(s_ref)

    bd = _block_mask(n * c, gw, c, RW_HEAD)

    def bdiag(x):
        return jnp.where(bd, _tile_rows(x, n), 0.0)

    refs = ((r0_ref, v0_ref, kk0_ref, lw0_ref, kd0_ref, a0_ref, y0_ref),
            (r1_ref, v1_ref, kk1_ref, lw1_ref, kd1_ref, a1_ref, y1_ref))
    masks = [_order_masks(z, c, n) for z in range(2)]
    prep = {}
    for bi in range(r0_ref.shape[0]):
        for z, (r_ref, v_ref, kk_ref, lw_ref, kd_ref, a_ref, _) in enumerate(refs):
            lw = lw_ref[0, bi]
            lam = _cumsum_rows(lw, z == 1)
            tot = jnp.sum(lw, axis=0, keepdims=True)
            w_inv = jnp.exp(-lam)
            w_end = jnp.exp(tot - lam)
            kk = kk_ref[bi]
            kd = kd_ref[0, bi]
            bb = kk * a_ref[0, bi]
            prep[bi, z] = {"a_t": -(kk * jnp.exp(lam - lw)), "r_t": r_ref[bi] * jnp.exp(lam), "b_t": bb * w_inv,
                           "k_t": kd * w_inv, "b_e": bb * w_end, "k_e": kd * w_end, "v": v_ref[bi],
                           "w_tot": jnp.exp(tot)}

    chains = [(bi, z, gi) for bi in range(r0_ref.shape[0]) for z in range(2) for gi in range(n_groups)]

    def sl(gi):
        return slice(gi * gw, (gi + 1) * gw)

    def part(name, ch):
        bi, z, gi = ch
        return prep[bi, z][name][:, sl(gi)]

    lhs = [_bf(jnp.concatenate([part("a_t", ch), part("r_t", ch)], axis=0)) for ch in chains]
    ab = [lax.dot_general(lh, _bf(bdiag(part("b_t", ch))), nt, preferred_element_type=F32)
          for lh, ch in zip(lhs, chains)]
    ak = [lax.dot_general(lh, _bf(bdiag(part("k_t", ch))), nt, preferred_element_type=F32)
          for lh, ch in zip(lhs, chains)]
    a_ab = [jnp.where(masks[z][1], x[:c], 0.0) for x, (_, z, _) in zip(ab, chains)]
    a_rb = [jnp.where(masks[z][0], x[c:], 0.0) for x, (_, z, _) in zip(ab, chains)]
    a_kk = [jnp.concatenate([jnp.where(masks[z][1], x[:c], 0.0), jnp.where(masks[z][0], x[c:], 0.0)], axis=0)
            for x, (_, z, _) in zip(ak, chains)]
    st = [s_ref[ch] for ch in chains]
    pre = [lax.dot_general(lh, _bf(s), nt, preferred_element_type=F32) for lh, s in zip(lhs, st)]
    vg = [part("v", ch) for ch in chains]
    av = [_dot(x, bdiag(v)) for x, v in zip(a_kk, vg)]
    t_inv = _tri_inv(a_ab, bd, [masks[z][2] for _, z, _ in chains], [z == 1 for _, z, _ in chains])
    u = [_dot(t, bdiag(p[:c] + q[:c])) for t, p, q in zip(t_inv, pre, av)]
    yv = [p[c:] + q[c:] + _dot(x, bdiag(ui)) for p, q, x, ui in zip(pre, av, a_rb, u)]
    upd = [_dot_tn(jnp.concatenate([ui, v], axis=0), jnp.concatenate([part("b_e", ch), part("k_e", ch)], axis=0))
           for ui, v, ch in zip(u, vg, chains)]
    for ch, y_val, s, up in zip(chains, yv, st, upd):
        bi, z, gi = ch
        refs[z][6][bi, :, sl(gi)] = y_val
        s_ref[ch] = s * part("w_tot", ch) + jnp.where(bd, up, 0.0)


def _rw_scan(r, v, kk, lw, kd, a, n_ctx):
    b, l, d = r.shape
    c = CHUNK
    nc = l // c
    ncc = n_ctx // c
    hpg = 4
    gw = hpg * RW_HEAD
    bpb = SCAN_BATCH if b % SCAN_BATCH == 0 else 1

    def shared(z):
        return pl.BlockSpec((bpb, c, d), lambda i, j: (i, _chunk_index(z, j, ncc, nc), 0))

    def perdir(z):
        return pl.BlockSpec((1, bpb, c, d), lambda i, j: (z, i, _chunk_index(z, j, ncc, nc), 0))

    sd = jax.ShapeDtypeStruct((b, l, d), F32)
    return pl.pallas_call(
        functools.partial(_rw_scan_kernel, heads_per_group=hpg),
        out_shape=(sd, sd),
        grid=(b // bpb, nc),
        in_specs=[shared(0)] * 3 + [perdir(0)] * 3 + [shared(1)] * 3 + [perdir(1)] * 3,
        out_specs=(shared(0), shared(1)),
        scratch_shapes=[pltpu.VMEM((bpb, 2, d // gw, gw, gw), F32)],
        compiler_params=_params(2),
        name="rw_scan",
    )(r, v, kk, lw, kd, a, r, v, kk, lw, kd, a)


def _rw_out_kernel(x_ref, pos_ref, y0_ref, y1_ref, r_ref, v_ref, kd_ref, gate_ref, mod_ref, rk_ref, lng_ref, lnb_ref,
                   wo_ref, gsum_ref, gbc_ref, xo_ref):
    inv_n = 1.0 / RW_HEAD
    gsum = gsum_ref[...]
    gbc = gbc_ref[...]
    y = y0_ref[0] + y1_ref[0]
    mu = _dot_x2(_dot_x2(y, gsum) * inv_n, gbc)
    yc = y - mu
    var = _dot_x2(yc * yc, gsum) * inv_n
    yn = yc * _dot_x2(lax.rsqrt(var + RW_GN_EPS), gbc) * lng_ref[...] + lnb_ref[...]
    k_bonus = 0.5 * (kd_ref[0, 0] + kd_ref[1, 0])
    bonus = _dot_x2(_dot_x2(r_ref[0] * k_bonus * rk_ref[...], gsum), gbc) * v_ref[0]
    out = (yn + bonus) * gate_ref[0]
    o = _dot(out, wo_ref[...])
    xo_ref[0] = x_ref[0] + pos_ref[...] + mod_ref[0, 0, 2:3, :] * o


def _rw_out(xs, pos, y, r, v, kd, gate, mod, p, gsum, gbc, n_ctx):
    b, l, d = xs.shape
    tm = TOKEN_TILE
    nt = l // tm
    nct = n_ctx // tm
    tok = pl.BlockSpec((1, tm, d), lambda i, t: (i, t, 0))
    tok2 = pl.BlockSpec((2, 1, tm, d), lambda i, t: (0, i, t, 0))
    consts = (p["r_k"], p["ln_g"], p["ln_b"], p["w_o"], gsum, gbc)
    return pl.pallas_call(
        _rw_out_kernel,
        out_shape=jax.ShapeDtypeStruct((b, l, d), F32),
        grid=(b, nt),
        in_specs=[tok, pl.BlockSpec((tm, d), lambda i, t: (t, 0)), tok, tok, tok, tok, tok2, tok,
                  pl.BlockSpec((1, 1, 6, d), lambda i, t: (i, _seg_index(t, nct), 0, 0))]
                 + [_const_spec(a.shape) for a in consts],
        out_specs=tok,
        compiler_params=_params(2),
        name="rw_out",
    )(xs, pos, y[0], y[1], r, v, kd, gate, mod, *consts)


def _route(logits):
    ne = MOE_GROUPS * MOE_EXPERTS
    lane = lax.broadcasted_iota(jnp.int32, logits.shape, 1).astype(F32)
    far = float(4 * LANES)
    g_mask = jnp.logical_and(lane >= ne, lane < ne + MOE_GROUPS)
    gl = jnp.where(g_mask, logits, NEG_BIG)
    gmax = jnp.max(gl, axis=1, keepdims=True)
    gsum = jnp.sum(jnp.where(g_mask, jnp.exp(gl - gmax), 0.0), axis=1, keepdims=True)
    gp = 1.0 / gsum
    g_first = jnp.min(jnp.where(gl == gmax, lane, far), axis=1, keepdims=True) - ne
    in_grp = jnp.logical_and(lane >= g_first * MOE_EXPERTS, lane < (g_first + 1.0) * MOE_EXPERTS)
    el = jnp.where(in_grp, logits, NEG_BIG)
    m1 = jnp.max(el, axis=1, keepdims=True)
    i1 = jnp.min(jnp.where(el == m1, lane, far), axis=1, keepdims=True)
    el2 = jnp.where(lane == i1, NEG_BIG, el)
    m2 = jnp.max(el2, axis=1, keepdims=True)
    i2 = jnp.min(jnp.where(el2 == m2, lane, far), axis=1, keepdims=True)
    e2 = jnp.exp(m2 - m1)
    p1 = gp / (1.0 + e2)
    p2 = p1 * e2
    comb = jnp.where(lane == i1, p1, jnp.where(lane == i2, p2, 0.0))
    return jnp.where(lane == g_first + ne, 1.0, comb)


def _seg_index(t, n_ctx_tiles):
    return jnp.where(t >= n_ctx_tiles, 1, 0)


def _moe_route_kernel(x_ref, mod_ref, ng_ref, wr_ref, br_ref, h_ref):
    d = x_ref.shape[2]
    h = _norm_mod(x_ref[0], ng_ref[...], mod_ref[0, 0, 3:4, :], mod_ref[0, 0, 4:5, :])
    h_ref[0, :, 0:d] = _bf(h)
    hi, lo = _split2(_route(_dot_hh(h, wr_ref[...]) + br_ref[...]))
    h_ref[0, :, d:d + LANES] = hi
    h_ref[0, :, d + LANES:d + 2 * LANES] = lo


def _moe_route(xs, mod, ng, p, n_ctx):
    b, rows, d = xs.shape
    tm = TOKEN_TILE
    nct = n_ctx // tm
    return pl.pallas_call(
        _moe_route_kernel,
        out_shape=jax.ShapeDtypeStruct((b, rows, d + 2 * LANES), BF16),
        grid=(b, rows // tm),
        in_specs=[pl.BlockSpec((1, tm, d), lambda i, t: (i, t, 0)),
                  pl.BlockSpec((1, 1, 6, d), lambda i, t: (i, _seg_index(t, nct), 0, 0)),
                  _const_spec(ng.shape), _const_spec(p["w_r"].shape), _const_spec(p["b_r"].shape)],
        out_specs=pl.BlockSpec((1, tm, d + 2 * LANES), lambda i, t: (i, t, 0)),
        compiler_params=_params(2),
        name="moe_route",
    )(xs, mod, ng, p["w_r"], p["b_r"])


MOE_ROW_BLOCK = 128


def _moe_expert_kernel(h_ref, tri_ref, upper_ref, wgu_ref, wd_ref, m_ref, xs_s, acc_s, dest_s, seg_s, *, tb):
    rows = h_ref.shape[1]
    slots = xs_s.shape[0]
    d = acc_s.shape[1]
    ff = wd_ref.shape[1]
    ne = MOE_GROUPS * MOE_EXPERTS
    rb = MOE_ROW_BLOCK
    e = pl.program_id(1)
    n_tb = rows // tb

    @pl.when(e == 0)
    def _():
        lane = lax.broadcasted_iota(jnp.int32, (1, LANES), 1)
        g_lanes = jnp.logical_and(lane >= ne, lane < ne + MOE_GROUPS)
        carry = jnp.zeros((1, LANES), F32)
        ranks = []
        for i in range(n_tb):
            gind = jnp.where(g_lanes, h_ref[0, i * tb:(i + 1) * tb, d:d + LANES].astype(F32), 0.0)
            r = jnp.dot(tri_ref[...], _bf(gind), preferred_element_type=F32) + carry
            carry = r[tb - 1:tb, :]
            ranks.append((gind, r))
        padded = jnp.ceil(carry * (1.0 / rb)) * rb
        start = _dot_x3(jnp.broadcast_to(padded, (HALO, LANES)), upper_ref[...])[0:1]
        for g in range(MOE_GROUPS):
            pick = lane == ne + g
            seg_s[g] = jnp.sum(jnp.where(pick, start, 0.0)).astype(jnp.int32)
            seg_s[MOE_GROUPS + g] = jnp.sum(jnp.where(pick, padded, 0.0)).astype(jnp.int32)
        dest_rows = []
        for i, (gind, r) in enumerate(ranks):
            dest = jnp.sum(gind * (start + r - 1.0), axis=1, keepdims=True)
            dest_b = jnp.broadcast_to(dest, (tb, LANES))
            dest_s[i * tb:(i + 1) * tb, :] = dest_b
            dest_rows.append(dest_b.T[0:1, :])
        dest_row = jnp.concatenate(dest_rows, axis=1)
        hb = h_ref[0]

        def permute(sb, c_):
            base = pl.multiple_of(sb * tb, tb)
            slot = (lax.broadcasted_iota(jnp.int32, (tb, rows), 0) + base).astype(F32)
            perm = jnp.where(slot == dest_row, 1.0, 0.0)
            xs_s[pl.ds(base, tb), :] = _bf(jnp.dot(_bf(perm), hb, preferred_element_type=F32))
            acc_s[pl.ds(base, tb), :] = jnp.zeros((tb, d), F32)
            return c_
        lax.fori_loop(0, slots // tb, permute, 0)

    grp = e // MOE_EXPERTS
    start = seg_s[grp]
    n_rows = seg_s[MOE_GROUPS + grp]
    def expert_rows(firsts, size):
        sls = [pl.ds(pl.multiple_of(first, rb), size) for first in firsts]
        xbs = [xs_s[sl, :] for sl in sls]
        gus = [jnp.dot(xb[:, 0:d], wgu_ref[0], preferred_element_type=F32) for xb in xbs]
        acts = []
        for xb, gu in zip(xbs, gus):
            cs = xb[:, d:d + LANES].astype(F32) + xb[:, d + LANES:d + 2 * LANES].astype(F32)
            lane = lax.broadcasted_iota(jnp.int32, cs.shape, 1)
            ce = jnp.sum(jnp.where(lane == e, cs, 0.0), axis=1, keepdims=True)
            acts.append(_bf(_silu(gu[:, 0:ff]) * gu[:, ff:2 * ff] * ce))
        outs = [jnp.dot(act, wd_ref[0], preferred_element_type=F32) for act in acts]
        for sl, out in zip(sls, outs):
            acc_s[sl, :] += out

    big_rows = 2 * rb
    n_pairs = n_rows // (2 * big_rows)

    def pair(i, c_):
        first = start + i * (2 * big_rows)
        expert_rows([first, first + big_rows], big_rows)
        return c_
    lax.fori_loop(0, n_pairs, pair, 0)
    done = n_pairs * (2 * big_rows)
    rest = n_rows - done

    @pl.when(rest >= big_rows)
    def _():
        expert_rows([start + done], big_rows)

    @pl.when(rest % big_rows > 0)
    def _():
        expert_rows([start + done + (rest // big_rows) * big_rows], rb)

    @pl.when(e == pl.num_programs(1) - 1)
    def _():
        def narrow(sb, c_):
            sl = pl.ds(pl.multiple_of(sb * tb, tb), tb)
            xs_s[sl, 0:d] = _bf(acc_s[sl, :])
            return c_
        lax.fori_loop(0, slots // tb, narrow, 0)
        acc = xs_s[:, 0:d]
        for i in range(n_tb):
            slot = lax.broadcasted_iota(jnp.int32, (tb, slots), 1).astype(F32)
            unperm = jnp.where(slot == dest_s[i * tb:(i + 1) * tb, 0:1], 1.0, 0.0)
            m_ref[0, i * tb:(i + 1) * tb, :] = _bf(jnp.dot(_bf(unperm), acc, preferred_element_type=F32))


def _moe_experts(h, p):
    b, rows, da = h.shape
    ne, d, ff2 = p["w_gu"].shape
    ff = ff2 // 2
    tb = TOKEN_TILE
    slots = rows + tb * (-(-(MOE_GROUPS * MOE_ROW_BLOCK) // tb))
    tri = jnp.tril(jnp.ones((tb, tb), BF16))
    upper = jnp.triu(jnp.ones((LANES, LANES), BF16), 1)
    return pl.pallas_call(
        functools.partial(_moe_expert_kernel, tb=tb),
        out_shape=jax.ShapeDtypeStruct((b, rows, d), BF16),
        grid=(b, ne),
        in_specs=[pl.BlockSpec((1, rows, da), lambda i, e: (i, 0, 0), pipeline_mode=pl.Buffered(1)),
                  _const_spec(tri.shape), _const_spec(upper.shape),
                  pl.BlockSpec((1, d, ff2), lambda i, e: (e, 0, 0)),
                  pl.BlockSpec((1, ff, d), lambda i, e: (e, 0, 0))],
        out_specs=pl.BlockSpec((1, rows, d), lambda i, e: (i, 0, 0)),
        scratch_shapes=[pltpu.VMEM((slots, da), BF16), pltpu.VMEM((slots, d), F32),
                        pltpu.VMEM((rows, LANES), F32), pltpu.SMEM((2 * MOE_GROUPS,), jnp.int32)],
        compiler_params=_params(2),
        name="moe_experts",
    )(h, tri, upper, p["w_gu"], p["w_down"])


def _moe_resid_kernel(x_ref, m_ref, mod_ref, fg_ref, xo_ref, *, final_norm):
    xn = x_ref[0] + mod_ref[0, 0, 5:6, :] * m_ref[0].astype(F32)
    if final_norm:
        ms = jnp.mean(xn * xn, axis=-1, keepdims=True)
        xn = xn * lax.rsqrt(ms + NORM_EPS) * fg_ref[...]
    xo_ref[0] = xn


def _moe_resid(xs, m, mod, final_g, n_ctx, final_norm):
    b, rows, d = xs.shape
    tm = TOKEN_TILE
    nct = n_ctx // tm
    tok = pl.BlockSpec((1, tm, d), lambda i, t: (i, t, 0))
    return pl.pallas_call(
        functools.partial(_moe_resid_kernel, final_norm=final_norm),
        out_shape=jax.ShapeDtypeStruct((b, rows, d), F32),
        grid=(b, rows // tm),
        in_specs=[tok, tok, pl.BlockSpec((1, 1, 6, d), lambda i, t: (i, _seg_index(t, nct), 0, 0)),
                  _const_spec(final_g.shape)],
        out_specs=tok,
        compiler_params=_params(2),
        name="moe_resid",
    )(xs, m, mod, final_g)


def _moe(xs, mod, ng, p, final_g, n_ctx, final_norm):
    h = _moe_route(xs, mod, ng, p, n_ctx)
    m = _moe_experts(h, p)
    return _moe_resid(xs, m, mod, final_g, n_ctx, final_norm)


def _gd_proj_kernel(x_ref, xp_ref, xn_ref, mod_ref, ng_ref, w_ref, conv_ref, wab_ref, alog_ref, dtb_ref,
                    o_ref, gb_ref, h_s, *, tm, n_ctx, n_tot, n_qk_blk, n_conv_blk, n_vh):
    jb = pl.program_id(2)
    pos = pl.program_id(1) * tm

    @pl.when(jb == 0)
    def _():
        sh = mod_ref[0, 0, 0:1, :]
        sc = mod_ref[0, 0, 1:2, :]
        g = ng_ref[...]
        prev_ok = jnp.logical_and(pos != 0, pos != n_ctx)
        next_ok = jnp.logical_and(pos + tm != n_ctx, pos + tm != n_tot)
        h = _norm_mod(x_ref[0], g, sh, sc)
        hp = jnp.where(prev_ok, _norm_mod(xp_ref[0], g, sh, sc), 0.0)
        hn = jnp.where(next_ok, _norm_mod(xn_ref[0], g, sh, sc), 0.0)
        h_s[...] = _bf(jnp.concatenate([hp, h, hn], axis=0))
        ab = _dot_hh(h, wab_ref[...])
        lane = lax.broadcasted_iota(jnp.int32, (1, LANES), 1)
        for z in range(2):
            abz = ab if z == 0 else pltpu.roll(ab, LANES - 2 * n_vh, axis=1)
            gdec = -jnp.exp(alog_ref[z:z + 1, :]) * _softplus(abz + dtb_ref[z:z + 1, :])
            gb_ref[z, 0] = jnp.where(lane < n_vh, gdec, jnp.where(lane < 2 * n_vh, _sigmoid(abz), 0.0))

    ext = jnp.dot(h_s[...], w_ref[...], preferred_element_type=F32)
    n_ext = tm + 2 * HALO

    @pl.when(jb >= n_conv_blk)
    def _():
        o_ref[0] = ext[HALO:HALO + tm]

    @pl.when(jb < n_conv_blk)
    def _():
        pad = (GD_CONV_W - 1) // 2
        is_qk = jb < n_qk_blk
        scale = jnp.where(jb < n_qk_blk // 2, GD_DK ** -0.5, 1.0)
        for hh in range(ext.shape[1] // GD_DK):
            cs = slice(hh * GD_DK, (hh + 1) * GD_DK)
            e = ext[:, cs]
            acc = e[HALO:HALO + tm] * conv_ref[pad:pad + 1, cs]
            for wi in range(GD_CONV_W):
                if wi == pad:
                    continue
                shifted = pltpu.roll(e, (pad - wi) % n_ext, axis=0)
                acc = acc + shifted[HALO:HALO + tm] * conv_ref[wi:wi + 1, cs]
            act = _silu(acc)
            ss = jnp.sum(act * act, axis=-1, keepdims=True)
            o_ref[0, :, cs] = act * jnp.where(is_qk, lax.rsqrt(ss + NORM_EPS) * scale, 1.0)


def _gd_proj(xs, mod, ng, w_in, conv_w, wab, alog, dtb, n_ctx, kd_total, qkv_total, n_vh):
    b, l, d = xs.shape
    n_out = w_in.shape[1]
    tm = TOKEN_TILE
    nb = 1024
    nt = l // tm
    nct = n_ctx // tm
    nj = n_out // nb
    n_conv_blk = qkv_total // nb
    kern = functools.partial(_gd_proj_kernel, tm=tm, n_ctx=n_ctx, n_tot=l, n_qk_blk=2 * kd_total // nb,
                             n_conv_blk=n_conv_blk, n_vh=n_vh)
    return pl.pallas_call(
        kern,
        out_shape=(jax.ShapeDtypeStruct((b, l, n_out), F32), jax.ShapeDtypeStruct((2, b, l, LANES), F32)),
        grid=(b, nt, nj),
        in_specs=_tile_specs(tm, d, nt)
                 + [pl.BlockSpec((1, 1, 6, d), lambda i, t, j: (i, _seg_index(t, nct), 0, 0)),
                    _const_spec(ng.shape),
                    pl.BlockSpec((d, nb), lambda i, t, j: (0, j)),
                    pl.BlockSpec((GD_CONV_W, nb), lambda i, t, j: (0, jnp.minimum(j, n_conv_blk - 1))),
                    _const_spec(wab.shape), _const_spec(alog.shape), _const_spec(dtb.shape)],
        out_specs=(pl.BlockSpec((1, tm, nb), lambda i, t, j: (i, t, j)),
                   pl.BlockSpec((2, 1, tm, LANES), lambda i, t, j: (0, i, t, 0))),
        scratch_shapes=[pltpu.VMEM((tm + 2 * HALO, d), BF16)],
        compiler_params=_params(3),
        name="gd_proj",
    )(xs, xs, xs, mod, ng, w_in, conv_w, wab, alog, dtb)


def _gd_chunk_kernel(q0_ref, k0_ref, v0_ref, gb0_ref, q1_ref, k1_ref, v1_ref, gb1_ref,
                     o0_ref, o1_ref, s_ref, *, n_vh, heads_per_group):
    c = q0_ref.shape[1]
    n = heads_per_group
    rep = n_vh // (q0_ref.shape[2] // GD_DK)
    n_groups = n_vh // n
    gw = n * c
    kh_per_group = n // rep
    assert 2 * c == LANES and n_vh % 2 == 0

    @pl.when(pl.program_id(1) == 0)
    def _():
        s_ref[...] = jnp.zeros_like(s_ref)

    bd = _block_mask(gw, gw, c, c)
    bd_k = _block_mask(gw, kh_per_group * GD_DK, c, GD_DK, rdiv=rep)
    bd_v = _block_mask(gw, n * GD_DV, c, GD_DV)
    low_half = lax.broadcasted_iota(jnp.int32, (1, LANES), 1) < c

    def bdiag_v(x):
        return jnp.where(bd_v, _tile_rows(x, n), 0.0)

    def spread_dv(x, off):
        return jnp.concatenate([jnp.broadcast_to(x[:, off + h:off + h + 1], (c, GD_DV)) for h in range(n_vh)], axis=1)

    def spread_c(x, off):
        cols = [jnp.broadcast_to(x[:, off + h:off + h + 1], (c, LANES)) for h in range(n_vh)]
        return jnp.concatenate([jnp.where(low_half, cols[h], cols[h + 1]) for h in range(0, n_vh, 2)], axis=1)

    refs = ((q0_ref, k0_ref, v0_ref, gb0_ref, o0_ref), (q1_ref, k1_ref, v1_ref, gb1_ref, o1_ref))
    prep = []
    for z, (q_ref, k_ref, v_ref, gb_ref, _) in enumerate(refs):
        gb = gb_ref[0, 0]
        gc = _cumsum_rows(gb, z == 1)
        gtot = jnp.sum(gb, axis=0, keepdims=True)
        incl_all, _, _ = _order_masks(z, c, n_vh)
        gt64 = spread_c(gc, 0)
        sq = jnp.concatenate([gc, jnp.zeros((LANES - c, LANES), F32)], axis=0).T
        sq_hi = pltpu.roll(sq, c, axis=1)
        gs64 = jnp.concatenate([jnp.where(low_half, sq[h:h + 1, :], sq_hi[h + 1:h + 2, :])
                                for h in range(0, n_vh, 2)], axis=1)
        prep.append({
            "gam": jnp.where(incl_all, jnp.exp(jnp.where(incl_all, gt64 - gs64, 0.0)), 0.0),
            "bt64": spread_c(gb, n_vh),
            "e_g": spread_dv(jnp.exp(gc), 0),
            "e_end": spread_dv(jnp.exp(gtot - gc), 0),
            "beta": spread_dv(gb, n_vh),
            "gl": jnp.exp(gtot),
            "q": q_ref[0], "k": k_ref[0], "v": v_ref[0], "masks": _order_masks(z, c, n)})

    chains = [(z, gi) for z in range(2) for gi in range(n_groups)]

    def ksl(gi):
        return slice(gi * kh_per_group * GD_DK, (gi + 1) * kh_per_group * GD_DK)

    def csl(gi):
        return slice(gi * gw, (gi + 1) * gw)

    def vsl(gi):
        return slice(gi * n * GD_DV, (gi + 1) * n * GD_DV)

    def per_vhead(x):
        return jnp.concatenate([x[:, (hh // rep) * GD_DK:(hh // rep + 1) * GD_DK] for hh in range(n)], axis=1)

    kg_ = [prep[z]["k"][:, ksl(gi)] for z, gi in chains]
    qg_ = [prep[z]["q"][:, ksl(gi)] for z, gi in chains]
    qkk = [_dot_nt(jnp.concatenate([kx, qx], axis=0), jnp.where(bd_k, _tile_rows(kx, n), 0.0))
           for kx, qx in zip(kg_, qg_)]
    a_mat = [jnp.where(prep[z]["masks"][1], x[:c] * prep[z]["gam"][:, csl(gi)] * prep[z]["bt64"][:, csl(gi)], 0.0)
             for x, (z, gi) in zip(qkk, chains)]
    aqk = [jnp.where(prep[z]["masks"][0], x[c:] * prep[z]["gam"][:, csl(gi)], 0.0) for x, (z, gi) in zip(qkk, chains)]
    t_inv = _tri_inv([-a for a in a_mat], bd, [prep[z]["masks"][2] for z, _ in chains], [z == 1 for z, _ in chains])

    k2 = [per_vhead(x) for x in kg_]
    q2 = [per_vhead(x) for x in qg_]
    bg = [prep[z]["beta"][:, vsl(gi)] for z, gi in chains]
    eg = [prep[z]["e_g"][:, vsl(gi)] for z, gi in chains]
    u = [_dot(t, bdiag_v(prep[z]["v"][:, vsl(gi)] * b_)) for t, b_, (z, gi) in zip(t_inv, bg, chains)]
    w = [_dot(t, bdiag_v(kx * b_ * e_)) for t, kx, b_, e_ in zip(t_inv, k2, bg, eg)]
    qe = [qx * e_ for qx, e_ in zip(q2, eg)]
    ke = [kx * prep[z]["e_end"][:, vsl(gi)] for kx, (z, gi) in zip(k2, chains)]

    heads = [(ci, hh) for ci in range(len(chains)) for hh in range(n)]

    def hs(hh):
        return slice(hh * GD_DV, (hh + 1) * GD_DV)

    def state_index(ci, hh):
        z, gi = chains[ci]
        return z, gi * n + hh

    ws = {(ci, hh): _dot(jnp.concatenate([w[ci][:, hs(hh)], qe[ci][:, hs(hh)]], axis=0), s_ref[state_index(ci, hh)])
          for ci, hh in heads}
    vn = [jnp.concatenate([u[ci][:, hs(hh)] - ws[ci, hh][:c] for hh in range(n)], axis=1) for ci in range(len(chains))]
    for ci, (z, gi) in enumerate(chains):
        pre = jnp.concatenate([ws[ci, hh][c:] for hh in range(n)], axis=1)
        refs[z][4][0, :, vsl(gi)] = pre + _dot(aqk[ci], bdiag_v(vn[ci]))
    for ci, hh in heads:
        z, hv = state_index(ci, hh)
        gl = jnp.broadcast_to(prep[z]["gl"][:, hv:hv + 1], (1, GD_DV))
        s_ref[z, hv] = s_ref[z, hv] * gl + _dot_tn(ke[ci][:, hs(hh)], vn[ci][:, hs(hh)])


def _gd_chunk(proj, gb, n_ctx, kd_total, vd_total):
    b, l, _ = proj.shape
    c = CHUNK
    nc = l // c
    ncc = n_ctx // c
    n_vh = vd_total // GD_DV

    def col(z, width, blk):
        return pl.BlockSpec((1, c, width), lambda i, j: (i, _chunk_index(z, j, ncc, nc), blk))

    def perdir(z):
        return pl.BlockSpec((1, 1, c, LANES), lambda i, j: (z, i, _chunk_index(z, j, ncc, nc), 0))

    def dir_specs(z):
        return [col(z, kd_total, 0), col(z, kd_total, 1), col(z, vd_total, 2 * kd_total // vd_total), perdir(z)]

    sd = jax.ShapeDtypeStruct((b, l, vd_total), F32)
    return pl.pallas_call(
        functools.partial(_gd_chunk_kernel, n_vh=n_vh, heads_per_group=4),
        out_shape=(sd, sd),
        grid=(b, nc),
        in_specs=dir_specs(0) + dir_specs(1),
        out_specs=(col(0, vd_total, 0), col(1, vd_total, 0)),
        scratch_shapes=[pltpu.VMEM((2, n_vh, GD_DK, GD_DV), F32)],
        compiler_params=_params(2),
        name="gd_chunk",
    )(proj, proj, proj, gb, proj, proj, proj, gb)


def _gd_out_kernel(x_ref, o0_ref, o1_ref, z_ref, mod_ref, ngd_ref, wo_ref, xo_ref):
    o = o0_ref[0] + o1_ref[0]
    z = z_ref[0]
    parts = []
    for hh in range(o.shape[1] // GD_DV):
        seg = o[:, hh * GD_DV:(hh + 1) * GD_DV]
        ms = jnp.mean(seg * seg, axis=-1, keepdims=True)
        parts.append(seg * lax.rsqrt(ms + NORM_EPS) * ngd_ref[...])
    on = jnp.concatenate(parts, axis=1) * _silu(z)
    xo_ref[0] = x_ref[0] + mod_ref[0, 0, 2:3, :] * _dot(on, wo_ref[...])


def _gd_out(xs, o, proj, mod, ngd, w_o, n_ctx, z_blk):
    b, l, d = xs.shape
    vd = o[0].shape[-1]
    tm = TOKEN_TILE
    nct = n_ctx // tm
    nt = (l - n_ctx) // tm
    return pl.pallas_call(
        _gd_out_kernel,
        out_shape=jax.ShapeDtypeStruct((b, l - n_ctx, d), F32),
        grid=(b, nt),
        in_specs=[pl.BlockSpec((1, tm, d), lambda i, t: (i, t + nct, 0)),
                  pl.BlockSpec((1, tm, vd), lambda i, t: (i, t + nct, 0)),
                  pl.BlockSpec((1, tm, vd), lambda i, t: (i, t + nct, 0)),
                  pl.BlockSpec((1, tm, vd), lambda i, t: (i, t + nct, z_blk)),
                  pl.BlockSpec((1, 1, 6, d), lambda i, t: (i, 1, 0, 0)),
                  _const_spec(ngd.shape), _const_spec(w_o.shape)],
        out_specs=pl.BlockSpec((1, tm, d), lambda i, t: (i, t, 0)),
        compiler_params=_params(2),
        name="gd_out",
    )(xs, o[0], o[1], proj, mod, ngd, w_o)


def _pos_embed_2d(rows, d):
    quarter = d // 4
    omega = 1.0 / (POS_BASE ** (jnp.arange(quarter, dtype=F32) / quarter))

    def axis_emb(n):
        ang = jnp.arange(n, dtype=F32)[:, None] * omega[None, :]
        return jnp.concatenate([jnp.sin(ang), jnp.cos(ang)], axis=-1)

    e_row = jnp.broadcast_to(axis_emb(rows)[:, None, :], (rows, GRID_W, d // 2))
    e_col = jnp.broadcast_to(axis_emb(GRID_W)[None, :, :], (rows, GRID_W, d // 2))
    return jnp.concatenate([e_row, e_col], axis=-1).reshape(rows * GRID_W, d)


def _head_indicator(d, head):
    ch = jnp.arange(d)[:, None] // head
    ind = (ch == jnp.arange(LANES)[None, :]).astype(BF16)
    return ind, ind.T


def _block_diag2(m):
    z = jnp.zeros_like(m[0])
    return jnp.concatenate([jnp.concatenate([m[0], z], axis=1), jnp.concatenate([z, m[1]], axis=1)], axis=0)


def _moe_params(i, moe_w_rg, moe_b_rg, moe_w_re, moe_b_re, moe_w_gate, moe_w_up, moe_w_down):
    d = moe_w_rg.shape[1]
    ne = MOE_GROUPS * MOE_EXPERTS
    pad = LANES - ne - MOE_GROUPS
    w_r = jnp.concatenate([moe_w_re[i], moe_w_rg[i], jnp.zeros((d, pad), F32)], axis=1)
    b_r = jnp.concatenate([moe_b_re[i], moe_b_rg[i], jnp.zeros((pad,), F32)])[None, :]
    ff = moe_w_gate.shape[-1]
    return {"w_r": w_r, "b_r": b_r,
            "w_gu": jnp.concatenate([_bf(moe_w_gate[i]), _bf(moe_w_up[i])], axis=-1).reshape(ne, d, 2 * ff),
            "w_down": _bf(moe_w_down[i]).reshape(ne, ff, d)}


def kernel(x, c, ctx, c_ctx, ada_w, ada_b, norm1_g, norm2_g, rw_mix, rw_w_rkv, rw_w0, rw_w1, rw_w2, rw_a0, rw_a1, rw_a2, rw_g1, rw_g2, rw_k_k, rw_k_a, rw_r_k, rw_ln_g, rw_ln_b, rw_w_o, gd_w_in, gd_conv, gd_w_ab, gd_a_log, gd_dt_bias, gd_norm_g, gd_w_o, moe_w_rg, moe_b_rg, moe_w_re, moe_b_re, moe_w_gate, moe_w_up, moe_w_down, final_g):
    bsz, n_lat, d = x.shape
    n_ctx = ctx.shape[1]
    assert n_ctx % TOKEN_TILE == 0 and n_lat % TOKEN_TILE == 0 and n_ctx % CHUNK == 0
    assert ada_w.shape[0] == 2 and d % (4 * RW_HEAD) == 0

    rows = -(-(bsz + 1) // HALO) * HALO
    cs = jnp.zeros((rows, d), F32).at[:bsz].set(c).at[bsz].set(c_ctx)
    mod_all = _ada(cs, ada_w, ada_b)

    def mod_of(i):
        lat = mod_all[i, :bsz].reshape(bsz, 1, 6, d)
        cx = jnp.broadcast_to(mod_all[i, bsz].reshape(1, 1, 6, d), (bsz, 1, 6, d))
        return jnp.concatenate([cx, lat], axis=1)

    xs = jnp.concatenate([ctx, x], axis=1)
    pos = jnp.concatenate([jnp.zeros((n_ctx, d), F32), _pos_embed_2d(n_lat // GRID_W, d)], axis=0)
    gsum, gbc = _head_indicator(d, RW_HEAD)

    mod0 = mod_of(0)
    ng1 = norm1_g[0][None, :]
    rw = {"mix": rw_mix[0], "w_rkv": _bf(rw_w_rkv[0]), "w0": rw_w0[0],
          "w1": _bf(jnp.concatenate([rw_w1[0, 0], rw_w1[0, 1]], axis=1)), "w2": _bf(_block_diag2(rw_w2[0])),
          "a0": rw_a0[0],
          "a1": _bf(jnp.concatenate([rw_a1[0, 0], rw_a1[0, 1]], axis=1)), "a2": _bf(_block_diag2(rw_a2[0])),
          "g1": _bf(rw_g1[0]), "g2": _bf(rw_g2[0]), "k_k": rw_k_k[0][None, :], "k_a": rw_k_a[0][None, :],
          "r_k": rw_r_k[0].reshape(1, d), "ln_g": rw_ln_g[0][None, :], "ln_b": rw_ln_b[0][None, :],
          "w_o": _bf(rw_w_o[0])}
    r, v, kk, gate, lw, kd, ar = _rw_feat(xs, pos, mod0, ng1, rw, gsum, gbc, n_ctx)
    y = _rw_scan(r, v, kk, lw, kd, ar, n_ctx)
    xs = _rw_out(xs, pos, y, r, v, kd, gate, mod0, rw, gsum, gbc, n_ctx)
    moe_args = (moe_w_rg, moe_b_rg, moe_w_re, moe_b_re, moe_w_gate, moe_w_up, moe_w_down)
    fg = final_g[None, :]
    xs = _moe(xs, mod0, norm2_g[0][None, :], _moe_params(0, *moe_args), fg, n_ctx, False)

    mod1 = mod_of(1)
    n_vh = gd_a_log.shape[-1]
    vd_total = n_vh * GD_DV
    kd_total = (gd_w_in.shape[-1] - 2 * vd_total) // 2
    qkv_total = 2 * kd_total + vd_total
    ng1 = norm1_g[1][None, :]
    wab = jnp.concatenate([gd_w_ab[0, 0], gd_w_ab[0, 1], jnp.zeros((d, LANES - 4 * n_vh), F32)], axis=1)
    lane_pad = jnp.zeros((2, LANES - n_vh), F32)
    alog = jnp.concatenate([gd_a_log[0], lane_pad], axis=1)
    dtb = jnp.concatenate([gd_dt_bias[0], lane_pad], axis=1)
    proj, gb = _gd_proj(xs, mod1, ng1, _bf(gd_w_in[0]), gd_conv[0], wab, alog, dtb, n_ctx, kd_total, qkv_total, n_vh)
    o = _gd_chunk(proj, gb, n_ctx, kd_total, vd_total)
    ngd = gd_norm_g[0][None, :]
    x_lat = _gd_out(xs, o, proj, mod1, ngd, _bf(gd_w_o[0]), n_ctx, qkv_total // vd_total)
    return _moe(x_lat, mod1, norm2_g[1][None, :], _moe_params(1, *moe_args), fg, 0, True)
```

```python
import functools
import math

import jax
import jax.numpy as jnp
from jax import lax
from jax.experimental import pallas as pl
from jax.experimental.pallas import tpu as pltpu

F32 = jnp.float32
BF16 = jnp.bfloat16

NORM_EPS = 1e-6
RW_GN_EPS = 64e-5
POS_BASE = 10000.0
GRID_W = 64
RW_HEAD = 64
RW_LORA = 64
GD_DK = 128
GD_DV = 128
GD_CONV_W = 5
MOE_GROUPS = 4
MOE_EXPERTS = 8
CHUNK = 64
LANES = 128
HALO = 8
TOKEN_TILE = 256
VMEM_LIMIT = 56 * 1024 * 1024
NEG_BIG = -1e30


def _bf(x):
    return x.astype(BF16)


def _dot(a, b):
    return jnp.dot(_bf(a), _bf(b), preferred_element_type=F32)


def _dot_nt(a, b):
    return lax.dot_general(_bf(a), _bf(b), (((1,), (1,)), ((), ())), preferred_element_type=F32)


def _dot_tn(a, b):
    return lax.dot_general(_bf(a), _bf(b), (((0,), (0,)), ((), ())), preferred_element_type=F32)


def _split2(x):
    hi = x.astype(BF16)
    lo = (x - hi.astype(F32)).astype(BF16)
    return hi, lo


def _split3(x):
    hi = x.astype(BF16)
    r1 = x - hi.astype(F32)
    mid = r1.astype(BF16)
    lo = (r1 - mid.astype(F32)).astype(BF16)
    return hi, mid, lo


def _dot_x2(a, b_exact):
    hi, lo = _split2(a)
    b = _bf(b_exact)
    return (jnp.dot(hi, b, preferred_element_type=F32) + jnp.dot(lo, b, preferred_element_type=F32))


def _dot_x3(a, b_exact):
    hi, mid, lo = _split3(a)
    b = _bf(b_exact)
    return (jnp.dot(hi, b, preferred_element_type=F32) + jnp.dot(mid, b, preferred_element_type=F32)
            + jnp.dot(lo, b, preferred_element_type=F32))


def _dot_ex3(a_exact, b):
    hi, mid, lo = _split3(b)
    a = _bf(a_exact)
    return (jnp.dot(a, hi, preferred_element_type=F32) + jnp.dot(a, mid, preferred_element_type=F32)
            + jnp.dot(a, lo, preferred_element_type=F32))


def _dot_hh(a, b):
    ah, al = _split2(a)
    bh, bl = _split2(b)
    return (jnp.dot(ah, bh, preferred_element_type=F32) + jnp.dot(al, bh, preferred_element_type=F32)
            + jnp.dot(ah, bl, preferred_element_type=F32))


def _sigmoid(x):
    return 1.0 / (1.0 + jnp.exp(-x))


def _silu(x):
    return x * _sigmoid(x)


def _softplus(x):
    return jnp.maximum(x, 0.0) + jnp.log(1.0 + jnp.exp(-jnp.abs(x)))


def _norm_mod(x, g, shift, scale):
    ms = jnp.mean(x * x, axis=-1, keepdims=True)
    return (x * lax.rsqrt(ms + NORM_EPS) * g) * (1.0 + scale) + shift


def _tile_rows(x, n):
    return jnp.concatenate([x] * n, axis=0)


def _tile_lanes(x, n):
    return jnp.concatenate([x] * n, axis=1)


def _order_masks(direction, c, n=1):
    row = lax.broadcasted_iota(jnp.int32, (c, n * c), 0)
    col = lax.broadcasted_iota(jnp.int32, (c, n * c), 1) % c
    diff = (col - row) * (1 - 2 * direction)
    return diff <= 0, diff < 0, jnp.where(diff == 0, 1.0, 0.0)


def _cumsum_rows(x, reverse):
    n = x.shape[0]
    row = lax.broadcasted_iota(jnp.int32, (n, 1), 0)
    s = 1
    while s < n:
        if reverse:
            shifted = jnp.where(row < n - s, pltpu.roll(x, n - s, axis=0), 0.0)
        else:
            shifted = jnp.where(row >= s, pltpu.roll(x, s, axis=0), 0.0)
        x = x + shifted
        s *= 2
    return x


def _block_mask(rows, cols, rblk, cblk, rdiv=1):
    r = lax.broadcasted_iota(jnp.int32, (rows, cols), 0) // rblk
    c = lax.broadcasted_iota(jnp.int32, (rows, cols), 1) // cblk
    return (r // rdiv) == c if rdiv != 1 else r == c


TRI_BASE = 4
SCAN_BATCH = 2


def _tri_inv(a_list, bd, eye_list, rev_list):
    c = a_list[0].shape[0]
    n = a_list[0].shape[1] // c
    row = lax.broadcasted_iota(jnp.int32, a_list[0].shape, 0)
    col = lax.broadcasted_iota(jnp.int32, a_list[0].shape, 1) % c

    def same_block(m):
        return (row // m) == (col // m)

    def mm(x, y):
        return jnp.dot(_bf(x), _bf(jnp.where(bd, _tile_rows(y, n), 0.0)), preferred_element_type=F32)

    base = same_block(TRI_BASE)
    a0 = [jnp.where(base, a, 0.0) for a in a_list]
    x = [e + a for e, a in zip(eye_list, a0)]
    p = [mm(a, a) for a in a0]
    base_levels = int(math.log2(TRI_BASE)) - 1
    for lvl in range(base_levels):
        x = [xi + mm(pi, xi) for xi, pi in zip(x, p)]
        if lvl + 1 < base_levels:
            p = [mm(pi, pi) for pi in p]
    m = TRI_BASE
    while m < c:
        off = jnp.logical_and(same_block(2 * m), jnp.logical_not(same_block(m)))
        a_off = [jnp.where(off, a, 0.0) for a in a_list]
        if m % HALO:
            t = [mm(ao, xi) for ao, xi in zip(a_off, x)]
            x = [xi + mm(xi, ti) for xi, ti in zip(x, t)]
        else:
            zeros = jnp.zeros((m, a_list[0].shape[1]), F32)

            def pick(v, rev):
                return jnp.concatenate([v[(2 * i + (0 if rev else 1)) * m:(2 * i + (0 if rev else 1) + 1) * m]
                                        for i in range(c // (2 * m))], axis=0)

            def place(v, rev):
                parts = []
                for i in range(c // (2 * m)):
                    blk = v[i * m:(i + 1) * m]
                    parts += [blk, zeros] if rev else [zeros, blk]
                return jnp.concatenate(parts, axis=0)

            t = [place(mm(pick(ao, rev), xi), rev) for ao, xi, rev in zip(a_off, x, rev_list)]
            x = [xi + place(mm(pick(xi, rev), ti), rev) for xi, ti, rev in zip(x, t, rev_list)]
        m *= 2
    return x


def _chunk_index(direction, j, n_ctx_chunks, n_chunks):
    bwd = jnp.where(j < n_ctx_chunks, n_ctx_chunks - 1 - j, n_chunks + n_ctx_chunks - 1 - j)
    return jnp.where(direction == 0, j, bwd)


def _const_spec(shape):
    nd = len(shape)
    return pl.BlockSpec(shape, lambda *_: (0,) * nd, pipeline_mode=pl.Buffered(1))


def _params(n_axes):
    return pltpu.CompilerParams(dimension_semantics=("arbitrary",) * n_axes, vmem_limit_bytes=VMEM_LIMIT)


def _ada_kernel(cs_ref, w_ref, b_ref, o_ref):
    s = _silu(cs_ref[...])
    o_ref[0] = _dot_hh(s, w_ref[0]) + b_ref[0]


def _ada(cs, ada_w, ada_b):
    n_layers, d, n6 = ada_w.shape
    rows = cs.shape[0]
    tn = 1536
    return pl.pallas_call(
        _ada_kernel,
        out_shape=jax.ShapeDtypeStruct((n_layers, rows, n6), F32),
        grid=(n_layers, n6 // tn),
        in_specs=[pl.BlockSpec((rows, d), lambda i, j: (0, 0)),
                  pl.BlockSpec((1, d, tn), lambda i, j: (i, 0, j)),
                  pl.BlockSpec((1, 1, tn), lambda i, j: (i, 0, j))],
        out_specs=pl.BlockSpec((1, rows, tn), lambda i, j: (i, 0, j)),
        compiler_params=_params(2),
        name="ada",
    )(cs, ada_w, ada_b.reshape(n_layers, 1, n6))


def _shifted_neighbours(h, h_prev, h_next, pos, tm, n_ctx, n_tot):
    prev_ok = jnp.logical_and(pos != 0, pos != n_ctx)
    next_ok = jnp.logical_and(pos + tm != n_ctx, pos + tm != n_tot)
    h_prev = jnp.where(prev_ok, h_prev, 0.0)
    h_next = jnp.where(next_ok, h_next, 0.0)
    row = lax.broadcasted_iota(jnp.int32, (tm, 1), 0)
    prev = jnp.where(row == 0, h_prev, pltpu.roll(h, 1, axis=0))
    nxt = jnp.where(row == tm - 1, h_next, pltpu.roll(h, tm - 1, axis=0))
    return prev, nxt


def _tile_specs(tm, d, n_tiles, lead=()):
    nl = len(lead)
    hb = tm // HALO
    last = n_tiles * hb - 1

    def cur(*g):
        return (g[nl], g[nl + 1], 0)

    def prv(*g):
        return (g[nl], jnp.maximum(g[nl + 1] * hb - 1, 0), 0)

    def nxt(*g):
        return (g[nl], jnp.minimum((g[nl + 1] + 1) * hb, last), 0)

    return [pl.BlockSpec((1, tm, d), cur), pl.BlockSpec((1, HALO, d), prv), pl.BlockSpec((1, HALO, d), nxt)]


def _rw_feat_kernel(x_ref, xp_ref, xn_ref, pos_ref, posp_ref, posn_ref, mod_ref, ng_ref, mix_ref,
                    wrkv_ref, w0_ref, w1_ref, w2_ref, a0_ref, a1_ref, a2_ref, g1_ref, g2_ref,
                    kkp_ref, kap_ref, gsum_ref, gbc_ref,
                    r_o, v_o, kk_o, gate_o, lw_o, kd_o, a_o, *, tm, n_ctx, n_tot):
    d = x_ref.shape[-1]
    pos = pl.program_id(1) * tm
    sh = mod_ref[0, 0, 0:1, :]
    sc = mod_ref[0, 0, 1:2, :]
    g = ng_ref[...]
    h = _norm_mod(x_ref[0] + pos_ref[...], g, sh, sc)
    hp = _norm_mod(xp_ref[0, HALO - 1:HALO, :] + posp_ref[HALO - 1:HALO, :], g, sh, sc)
    hn = _norm_mod(xn_ref[0, 0:1, :] + posn_ref[0:1, :], g, sh, sc)
    prev, nxt = _shifted_neighbours(h, hp, hn, pos, tm, n_ctx, n_tot)
    xx = 0.5 * (prev + nxt) - h
    mix = mix_ref[...]
    xr = h + xx * mix[0:1]
    xw = h + xx * mix[1:2]
    xk = h + xx * mix[2:3]
    xv = h + xx * mix[3:4]
    xa = h + xx * mix[4:5]
    xg = h + xx * mix[5:6]

    r_o[0] = _bf(_dot(xr, wrkv_ref[0]))
    v_o[0] = _bf(_dot(xv, wrkv_ref[2]))
    gate_o[0] = _bf(_dot(_sigmoid(_dot(xg, g1_ref[...])), g2_ref[...]))
    k = _dot(xk, wrkv_ref[1])

    kq = k * kkp_ref[...]
    ssq = _dot_x2(kq * kq, gsum_ref[...])
    kk_o[0] = _bf(kq * _dot_x2(lax.rsqrt(ssq + NORM_EPS), gbc_ref[...]))

    wl = _dot(jnp.tanh(_dot(xw, w1_ref[...])), w2_ref[...])
    al = _dot(_dot(xa, a1_ref[...]), a2_ref[...])
    ka = kap_ref[...]
    for z in range(2):
        zw = w0_ref[z:z + 1, :] + wl[:, z * d:(z + 1) * d]
        lw_o[z, 0] = (-math.exp(-0.5)) * _sigmoid(zw)
        rate = _sigmoid(a0_ref[z:z + 1, :] + al[:, z * d:(z + 1) * d])
        a_o[z, 0] = _bf(rate)
        kd_o[z, 0] = _bf(k * (1.0 + (rate - 1.0) * ka))


def _rw_feat(xs, pos, mod, ng, p, gsum, gbc, n_ctx):
    b, l, d = xs.shape
    tm = TOKEN_TILE
    nt = l // tm
    nct = n_ctx // tm
    hb = tm // HALO
    kern = functools.partial(_rw_feat_kernel, tm=tm, n_ctx=n_ctx, n_tot=l)
    tok = pl.BlockSpec((1, tm, d), lambda i, t: (i, t, 0))
    tok2 = pl.BlockSpec((2, 1, tm, d), lambda i, t: (0, i, t, 0))
    pos_specs = [pl.BlockSpec((tm, d), lambda i, t: (t, 0)),
                 pl.BlockSpec((HALO, d), lambda i, t: (jnp.maximum(t * hb - 1, 0), 0)),
                 pl.BlockSpec((HALO, d), lambda i, t: (jnp.minimum((t + 1) * hb, nt * hb - 1), 0))]
    in_specs = (_tile_specs(tm, d, nt) + pos_specs
                + [pl.BlockSpec((1, 1, 6, d), lambda i, t: (i, _seg_index(t, nct), 0, 0))]
                + [_const_spec(a.shape) for a in (ng, p["mix"], p["w_rkv"], p["w0"], p["w1"], p["w2"], p["a0"],
                                                  p["a1"], p["a2"], p["g1"], p["g2"], p["k_k"], p["k_a"], gsum, gbc)])
    sd = jax.ShapeDtypeStruct((b, l, d), BF16)
    sd2 = jax.ShapeDtypeStruct((2, b, l, d), BF16)
    return pl.pallas_call(
        kern,
        out_shape=(sd, sd, sd, sd, jax.ShapeDtypeStruct((2, b, l, d), F32), sd2, sd2),
        grid=(b, nt),
        in_specs=in_specs,
        out_specs=(tok, tok, tok, tok, tok2, tok2, tok2),
        compiler_params=_params(2),
        name="rw_feat",
    )(xs, xs, xs, pos, pos, pos, mod, ng, p["mix"], p["w_rkv"], p["w0"], p["w1"], p["w2"], p["a0"], p["a1"], p["a2"],
      p["g1"], p["g2"], p["k_k"], p["k_a"], gsum, gbc)


def _rw_scan_kernel(r0_ref, v0_ref, kk0_ref, lw0_ref, kd0_ref, a0_ref,
                    r1_ref, v1_ref, kk1_ref, lw1_ref, kd1_ref, a1_ref,
                    y0_ref, y1_ref, s_ref, *, heads_per_group):
    c = r0_ref.shape[1]
    d = r0_ref.shape[2]
    n = heads_per_group
    gw = n * RW_HEAD
    n_groups = d // gw
    nt = (((1,), (1,)), ((), ()))

    @pl.when(pl.program_id(1) == 0)
    def _():
        s_ref[...] = jnp.zeros_like(s_ref)

    bd = _block_mask(n * c, gw, c, RW_HEAD)

    def bdiag(x):
        return jnp.where(bd, _tile_rows(x, n), 0.0)

    refs = ((r0_ref, v0_ref, kk0_ref, lw0_ref, kd0_ref, a0_ref, y0_ref),
            (r1_ref, v1_ref, kk1_ref, lw1_ref, kd1_ref, a1_ref, y1_ref))
    masks = [_order_masks(z, c, n) for z in range(2)]
    prep = {}
    for bi in range(r0_ref.shape[0]):
        for z, (r_ref, v_ref, kk_ref, lw_ref, kd_ref, a_ref, _) in enumerate(refs):
            lw = lw_ref[0, bi]
            lam = _cumsum_rows(lw, z == 1)
            tot = jnp.sum(lw, axis=0, keepdims=True)
            w_inv = jnp.exp(-lam)
            w_end = jnp.exp(tot - lam)
            kk = kk_ref[bi].astype(F32)
            kd = kd_ref[0, bi].astype(F32)
            bb = kk * a_ref[0, bi].astype(F32)
            prep[bi, z] = {"a_t": -(kk * jnp.exp(lam - lw)), "r_t": r_ref[bi].astype(F32) * jnp.exp(lam),
                           "b_t": bb * w_inv, "k_t": kd * w_inv, "b_e": bb * w_end, "k_e": kd * w_end,
                           "v": v_ref[bi].astype(F32),
                           "w_tot": jnp.exp(tot)}

    chains = [(bi, z, gi) for bi in range(r0_ref.shape[0]) for z in range(2) for gi in range(n_groups)]

    def sl(gi):
        return slice(gi * gw, (gi + 1) * gw)

    def part(name, ch):
        bi, z, gi = ch
        return prep[bi, z][name][:, sl(gi)]

    lhs = [_bf(jnp.concatenate([part("a_t", ch), part("r_t", ch)], axis=0)) for ch in chains]
    ab = [lax.dot_general(lh, _bf(bdiag(part("b_t", ch))), nt, preferred_element_type=F32)
          for lh, ch in zip(lhs, chains)]
    ak = [lax.dot_general(lh, _bf(bdiag(part("k_t", ch))), nt, preferred_element_type=F32)
          for lh, ch in zip(lhs, chains)]
    a_ab = [jnp.where(masks[z][1], x[:c], 0.0) for x, (_, z, _) in zip(ab, chains)]
    a_rb = [jnp.where(masks[z][0], x[c:], 0.0) for x, (_, z, _) in zip(ab, chains)]
    a_kk = [jnp.concatenate([jnp.where(masks[z][1], x[:c], 0.0), jnp.where(masks[z][0], x[c:], 0.0)], axis=0)
            for x, (_, z, _) in zip(ak, chains)]
    st = [s_ref[ch] for ch in chains]
    pre = [lax.dot_general(lh, _bf(s), nt, preferred_element_type=F32) for lh, s in zip(lhs, st)]
    vg = [part("v", ch) for ch in chains]
    av = [_dot(x, bdiag(v)) for x, v in zip(a_kk, vg)]
    t_inv = _tri_inv(a_ab, bd, [masks[z][2] for _, z, _ in chains], [z == 1 for _, z, _ in chains])
    u = [_dot(t, bdiag(p[:c] + q[:c])) for t, p, q in zip(t_inv, pre, av)]
    yv = [p[c:] + q[c:] + _dot(x, bdiag(ui)) for p, q, x, ui in zip(pre, av, a_rb, u)]
    upd = [_dot_tn(jnp.concatenate([ui, v], axis=0), jnp.concatenate([part("b_e", ch), part("k_e", ch)], axis=0))
           for ui, v, ch in zip(u, vg, chains)]
    for ch, y_val, s, up in zip(chains, yv, st, upd):
        bi, z, gi = ch
        refs[z][6][bi, :, sl(gi)] = y_val
        s_ref[ch] = s * part("w_tot", ch) + jnp.where(bd, up, 0.0)


def _rw_scan(r, v, kk, lw, kd, a, n_ctx):
    b, l, d = r.shape
    c = CHUNK
    nc = l // c
    ncc = n_ctx // c
    hpg = 4
    gw = hpg * RW_HEAD
    bpb = SCAN_BATCH if b % SCAN_BATCH == 0 else 1

    def shared(z):
        return pl.BlockSpec((bpb, c, d), lambda i, j: (i, _chunk_index(z, j, ncc, nc), 0))

    def perdir(z):
        return pl.BlockSpec((1, bpb, c, d), lambda i, j: (z, i, _chunk_index(z, j, ncc, nc), 0))

    sd = jax.ShapeDtypeStruct((b, l, d), F32)
    return pl.pallas_call(
        functools.partial(_rw_scan_kernel, heads_per_group=hpg),
        out_shape=(sd, sd),
        grid=(b // bpb, nc),
        in_specs=[shared(0)] * 3 + [perdir(0)] * 3 + [shared(1)] * 3 + [perdir(1)] * 3,
        out_specs=(shared(0), shared(1)),
        scratch_shapes=[pltpu.VMEM((bpb, 2, d // gw, gw, gw), F32)],
        compiler_params=_params(2),
        name="rw_scan",
    )(r, v, kk, lw, kd, a, r, v, kk, lw, kd, a)


def _rw_out_kernel(x_ref, pos_ref, y0_ref, y1_ref, r_ref, v_ref, kd_ref, gate_ref, mod_ref, rk_ref, lng_ref, lnb_ref,
                   wo_ref, gsum_ref, gbc_ref, xo_ref):
    inv_n = 1.0 / RW_HEAD
    gsum = gsum_ref[...]
    gbc = gbc_ref[...]
    y = y0_ref[0] + y1_ref[0]
    mu = _dot_x2(_dot_x2(y, gsum) * inv_n, gbc)
    yc = y - mu
    var = _dot_x2(yc * yc, gsum) * inv_n
    yn = yc * _dot_x2(lax.rsqrt(var + RW_GN_EPS), gbc) * lng_ref[...] + lnb_ref[...]
    k_bonus = 0.5 * (kd_ref[0, 0].astype(F32) + kd_ref[1, 0].astype(F32))
    bonus = _dot_x2(_dot_x2(r_ref[0].astype(F32) * k_bonus * rk_ref[...], gsum), gbc) * v_ref[0].astype(F32)
    out = (yn + bonus) * gate_ref[0].astype(F32)
    o = _dot(out, wo_ref[...])
    xo_ref[0] = x_ref[0] + pos_ref[...] + mod_ref[0, 0, 2:3, :] * o


def _rw_out(xs, pos, y, r, v, kd, gate, mod, p, gsum, gbc, n_ctx):
    b, l, d = xs.shape
    tm = TOKEN_TILE
    nt = l // tm
    nct = n_ctx // tm
    tok = pl.BlockSpec((1, tm, d), lambda i, t: (i, t, 0))
    tok2 = pl.BlockSpec((2, 1, tm, d), lambda i, t: (0, i, t, 0))
    consts = (p["r_k"], p["ln_g"], p["ln_b"], p["w_o"], gsum, gbc)
    return pl.pallas_call(
        _rw_out_kernel,
        out_shape=jax.ShapeDtypeStruct((b, l, d), F32),
        grid=(b, nt),
        in_specs=[tok, pl.BlockSpec((tm, d), lambda i, t: (t, 0)), tok, tok, tok, tok, tok2, tok,
                  pl.BlockSpec((1, 1, 6, d), lambda i, t: (i, _seg_index(t, nct), 0, 0))]
                 + [_const_spec(a.shape) for a in consts],
        out_specs=tok,
        compiler_params=_params(2),
        name="rw_out",
    )(xs, pos, y[0], y[1], r, v, kd, gate, mod, *consts)


def _route(logits):
    ne = MOE_GROUPS * MOE_EXPERTS
    lane = lax.broadcasted_iota(jnp.int32, logits.shape, 1).astype(F32)
    far = float(4 * LANES)
    g_mask = jnp.logical_and(lane >= ne, lane < ne + MOE_GROUPS)
    gl = jnp.where(g_mask, logits, NEG_BIG)
    gmax = jnp.max(gl, axis=1, keepdims=True)
    gsum = jnp.sum(jnp.where(g_mask, jnp.exp(gl - gmax), 0.0), axis=1, keepdims=True)
    gp = 1.0 / gsum
    g_first = jnp.min(jnp.where(gl == gmax, lane, far), axis=1, keepdims=True) - ne
    in_grp = jnp.logical_and(lane >= g_first * MOE_EXPERTS, lane < (g_first + 1.0) * MOE_EXPERTS)
    el = jnp.where(in_grp, logits, NEG_BIG)
    m1 = jnp.max(el, axis=1, keepdims=True)
    i1 = jnp.min(jnp.where(el == m1, lane, far), axis=1, keepdims=True)
    el2 = jnp.where(lane == i1, NEG_BIG, el)
    m2 = jnp.max(el2, axis=1, keepdims=True)
    i2 = jnp.min(jnp.where(el2 == m2, lane, far), axis=1, keepdims=True)
    e2 = jnp.exp(m2 - m1)
    p1 = gp / (1.0 + e2)
    p2 = p1 * e2
    comb = jnp.where(lane == i1, p1, jnp.where(lane == i2, p2, 0.0))
    return jnp.where(lane == g_first + ne, 1.0, comb)


def _seg_index(t, n_ctx_tiles):
    return jnp.where(t >= n_ctx_tiles, 1, 0)


def _moe_route_kernel(x_ref, mod_ref, ng_ref, wr_ref, br_ref, h_ref):
    d = x_ref.shape[2]
    h = _norm_mod(x_ref[0], ng_ref[...], mod_ref[0, 0, 3:4, :], mod_ref[0, 0, 4:5, :])
    h_ref[0, :, 0:d] = _bf(h)
    hi, lo = _split2(_route(_dot_hh(h, wr_ref[...]) + br_ref[...]))
    h_ref[0, :, d:d + LANES] = hi
    h_ref[0, :, d + LANES:d + 2 * LANES] = lo


def _moe_route(xs, mod, ng, p, n_ctx):
    b, rows, d = xs.shape
    tm = TOKEN_TILE
    nct = n_ctx // tm
    return pl.pallas_call(
        _moe_route_kernel,
        out_shape=jax.ShapeDtypeStruct((b, rows, d + 2 * LANES), BF16),
        grid=(b, rows // tm),
        in_specs=[pl.BlockSpec((1, tm, d), lambda i, t: (i, t, 0)),
                  pl.BlockSpec((1, 1, 6, d), lambda i, t: (i, _seg_index(t, nct), 0, 0)),
                  _const_spec(ng.shape), _const_spec(p["w_r"].shape), _const_spec(p["b_r"].shape)],
        out_specs=pl.BlockSpec((1, tm, d + 2 * LANES), lambda i, t: (i, t, 0)),
        compiler_params=_params(2),
        name="moe_route",
    )(xs, mod, ng, p["w_r"], p["b_r"])


MOE_ROW_BLOCK = 128
MOE_EXPERTS_PER_STEP = 2


def _moe_expert_kernel(h_ref, tri_ref, upper_ref, wgu_ref, wd_ref, m_ref, xs_s, acc_s, dest_s, seg_s, *, tb):
    rows = h_ref.shape[1]
    slots = xs_s.shape[0]
    d = acc_s.shape[1]
    ff = wd_ref.shape[1]
    ne = MOE_GROUPS * MOE_EXPERTS
    rb = MOE_ROW_BLOCK
    e = pl.program_id(1)
    n_tb = rows // tb

    @pl.when(e == 0)
    def _():
        lane = lax.broadcasted_iota(jnp.int32, (1, LANES), 1)
        g_lanes = jnp.logical_and(lane >= ne, lane < ne + MOE_GROUPS)
        carry = jnp.zeros((1, LANES), F32)
        ranks = []
        for i in range(n_tb):
            gind = jnp.where(g_lanes, h_ref[0, i * tb:(i + 1) * tb, d:d + LANES].astype(F32), 0.0)
            r = jnp.dot(tri_ref[...], _bf(gind), preferred_element_type=F32) + carry
            carry = r[tb - 1:tb, :]
            ranks.append((gind, r))
        padded = jnp.ceil(carry * (1.0 / rb)) * rb
        start = _dot_x3(jnp.broadcast_to(padded, (HALO, LANES)), upper_ref[...])[0:1]
        for g in range(MOE_GROUPS):
            pick = lane == ne + g
            seg_s[g] = jnp.sum(jnp.where(pick, start, 0.0)).astype(jnp.int32)
            seg_s[MOE_GROUPS + g] = jnp.sum(jnp.where(pick, padded, 0.0)).astype(jnp.int32)
        dest_rows = []
        for i, (gind, r) in enumerate(ranks):
            dest = jnp.sum(gind * (start + r - 1.0), axis=1, keepdims=True)
            dest_b = jnp.broadcast_to(dest, (tb, LANES))
            dest_s[i * tb:(i + 1) * tb, :] = dest_b
            dest_rows.append(dest_b.T[0:1, :])
        dest_row = jnp.concatenate(dest_rows, axis=1)
        hb = h_ref[0]

        def permute(sb, c_):
            base = pl.multiple_of(sb * tb, tb)
            slot = (lax.broadcasted_iota(jnp.int32, (tb, rows), 0) + base).astype(F32)
            perm = jnp.where(slot == dest_row, 1.0, 0.0)
            xs_s[pl.ds(base, tb), :] = _bf(jnp.dot(_bf(perm), hb, preferred_element_type=F32))
            acc_s[pl.ds(base, tb), :] = jnp.zeros((tb, d), F32)
            return c_
        lax.fori_loop(0, slots // tb, permute, 0)

    n_here = wgu_ref.shape[0]
    grp = (e * n_here) // MOE_EXPERTS
    start = seg_s[grp]
    n_rows = seg_s[MOE_GROUPS + grp]

    def expert_rows(firsts, size):
        sls = [pl.ds(pl.multiple_of(first, rb), size) for first in firsts]
        xbs = [xs_s[sl, :] for sl in sls]
        work = [(bi, k) for bi in range(len(sls)) for k in range(n_here)]
        gus = [jnp.dot(xbs[bi][:, 0:d], wgu_ref[k], preferred_element_type=F32) for bi, k in work]
        css = [xb[:, d:d + LANES].astype(F32) + xb[:, d + LANES:d + 2 * LANES].astype(F32) for xb in xbs]
        acts = []
        for (bi, k), gu in zip(work, gus):
            lane = lax.broadcasted_iota(jnp.int32, css[bi].shape, 1)
            ce = jnp.sum(jnp.where(lane == e * n_here + k, css[bi], 0.0), axis=1, keepdims=True)
            acts.append(_bf(_silu(gu[:, 0:ff]) * gu[:, ff:2 * ff] * ce))
        outs = [jnp.dot(act, wd_ref[k], preferred_element_type=F32) for act, (_, k) in zip(acts, work)]
        for bi, sl in enumerate(sls):
            acc_s[sl, :] += sum(out for out, (bj, _) in zip(outs, work) if bj == bi)

    big_rows = 2 * rb
    n_pairs = n_rows // (2 * big_rows)

    def pair(i, c_):
        first = start + i * (2 * big_rows)
        expert_rows([first, first + big_rows], big_rows)
        return c_
    lax.fori_loop(0, n_pairs, pair, 0)
    done = n_pairs * (2 * big_rows)
    rest = n_rows - done

    @pl.when(rest >= big_rows)
    def _():
        expert_rows([start + done], big_rows)

    @pl.when(rest % big_rows > 0)
    def _():
        expert_rows([start + done + (rest // big_rows) * big_rows], rb)

    @pl.when(e == pl.num_programs(1) - 1)
    def _():
        def narrow(sb, c_):
            sl = pl.ds(pl.multiple_of(sb * tb, tb), tb)
            xs_s[sl, 0:d] = _bf(acc_s[sl, :])
            return c_
        lax.fori_loop(0, slots // tb, narrow, 0)
        acc = xs_s[:, 0:d]
        for i in range(n_tb):
            slot = lax.broadcasted_iota(jnp.int32, (tb, slots), 1).astype(F32)
            unperm = jnp.where(slot == dest_s[i * tb:(i + 1) * tb, 0:1], 1.0, 0.0)
            m_ref[0, i * tb:(i + 1) * tb, :] = _bf(jnp.dot(_bf(unperm), acc, preferred_element_type=F32))


def _moe_experts(h, p):
    b, rows, da = h.shape
    ne, d, ff2 = p["w_gu"].shape
    ff = ff2 // 2
    tb = TOKEN_TILE
    eps = MOE_EXPERTS_PER_STEP
    assert MOE_EXPERTS % eps == 0
    slots = rows + tb * (-(-(MOE_GROUPS * MOE_ROW_BLOCK) // tb))
    tri = jnp.tril(jnp.ones((tb, tb), BF16))
    upper = jnp.triu(jnp.ones((LANES, LANES), BF16), 1)
    return pl.pallas_call(
        functools.partial(_moe_expert_kernel, tb=tb),
        out_shape=jax.ShapeDtypeStruct((b, rows, d), BF16),
        grid=(b, ne // eps),
        in_specs=[pl.BlockSpec((1, rows, da), lambda i, e: (i, 0, 0), pipeline_mode=pl.Buffered(1)),
                  _const_spec(tri.shape), _const_spec(upper.shape),
                  pl.BlockSpec((eps, d, ff2), lambda i, e: (e, 0, 0)),
                  pl.BlockSpec((eps, ff, d), lambda i, e: (e, 0, 0))],
        out_specs=pl.BlockSpec((1, rows, d), lambda i, e: (i, 0, 0)),
        scratch_shapes=[pltpu.VMEM((slots, da), BF16), pltpu.VMEM((slots, d), F32),
                        pltpu.VMEM((rows, LANES), F32), pltpu.SMEM((2 * MOE_GROUPS,), jnp.int32)],
        compiler_params=_params(2),
        name="moe_experts",
    )(h, tri, upper, p["w_gu"], p["w_down"])


def _moe_resid_kernel(x_ref, m_ref, mod_ref, fg_ref, xo_ref, *, final_norm):
    xn = x_ref[0] + mod_ref[0, 0, 5:6, :] * m_ref[0].astype(F32)
    if final_norm:
        ms = jnp.mean(xn * xn, axis=-1, keepdims=True)
        xn = xn * lax.rsqrt(ms + NORM_EPS) * fg_ref[...]
    xo_ref[0] = xn


def _moe_resid(xs, m, mod, final_g, n_ctx, final_norm):
    b, rows, d = xs.shape
    tm = TOKEN_TILE
    nct = n_ctx // tm
    tok = pl.BlockSpec((1, tm, d), lambda i, t: (i, t, 0))
    return pl.pallas_call(
        functools.partial(_moe_resid_kernel, final_norm=final_norm),
        out_shape=jax.ShapeDtypeStruct((b, rows, d), F32),
        grid=(b, rows // tm),
        in_specs=[tok, tok, pl.BlockSpec((1, 1, 6, d), lambda i, t: (i, _seg_index(t, nct), 0, 0)),
                  _const_spec(final_g.shape)],
        out_specs=tok,
        compiler_params=_params(2),
        name="moe_resid",
    )(xs, m, mod, final_g)


def _moe(xs, mod, ng, p, final_g, n_ctx, final_norm):
    h = _moe_route(xs, mod, ng, p, n_ctx)
    m = _moe_experts(h, p)
    return _moe_resid(xs, m, mod, final_g, n_ctx, final_norm)


def _gd_proj_kernel(x_ref, xp_ref, xn_ref, mod_ref, ng_ref, w_ref, conv_ref, wab_ref, alog_ref, dtb_ref,
                    o_ref, gb_ref, h_s, *, tm, n_ctx, n_tot, n_qk_blk, n_conv_blk, n_vh):
    jb = pl.program_id(2)
    pos = pl.program_id(1) * tm

    @pl.when(jb == 0)
    def _():
        sh = mod_ref[0, 0, 0:1, :]
        sc = mod_ref[0, 0, 1:2, :]
        g = ng_ref[...]
        prev_ok = jnp.logical_and(pos != 0, pos != n_ctx)
        next_ok = jnp.logical_and(pos + tm != n_ctx, pos + tm != n_tot)
        h = _norm_mod(x_ref[0], g, sh, sc)
        hp = jnp.where(prev_ok, _norm_mod(xp_ref[0], g, sh, sc), 0.0)
        hn = jnp.where(next_ok, _norm_mod(xn_ref[0], g, sh, sc), 0.0)
        h_s[...] = _bf(jnp.concatenate([hp, h, hn], axis=0))
        ab = _dot_hh(h, wab_ref[...])
        lane = lax.broadcasted_iota(jnp.int32, (1, LANES), 1)
        for z in range(2):
            abz = ab if z == 0 else pltpu.roll(ab, LANES - 2 * n_vh, axis=1)
            gdec = -jnp.exp(alog_ref[z:z + 1, :]) * _softplus(abz + dtb_ref[z:z + 1, :])
            gb_ref[z, 0] = jnp.where(lane < n_vh, gdec, jnp.where(lane < 2 * n_vh, _sigmoid(abz), 0.0))

    ext = jnp.dot(h_s[...], w_ref[...], preferred_element_type=F32)
    n_ext = tm + 2 * HALO

    @pl.when(jb >= n_conv_blk)
    def _():
        o_ref[0] = ext[HALO:HALO + tm]

    @pl.when(jb < n_conv_blk)
    def _():
        pad = (GD_CONV_W - 1) // 2
        is_qk = jb < n_qk_blk
        scale = jnp.where(jb < n_qk_blk // 2, GD_DK ** -0.5, 1.0)
        for hh in range(ext.shape[1] // GD_DK):
            cs = slice(hh * GD_DK, (hh + 1) * GD_DK)
            e = ext[:, cs]
            acc = e[HALO:HALO + tm] * conv_ref[pad:pad + 1, cs]
            for wi in range(GD_CONV_W):
                if wi == pad:
                    continue
                shifted = pltpu.roll(e, (pad - wi) % n_ext, axis=0)
                acc = acc + shifted[HALO:HALO + tm] * conv_ref[wi:wi + 1, cs]
            act = _silu(acc)
            ss = jnp.sum(act * act, axis=-1, keepdims=True)
            o_ref[0, :, cs] = act * jnp.where(is_qk, lax.rsqrt(ss + NORM_EPS) * scale, 1.0)


def _gd_proj(xs, mod, ng, w_in, conv_w, wab, alog, dtb, n_ctx, kd_total, qkv_total, n_vh):
    b, l, d = xs.shape
    n_out = w_in.shape[1]
    tm = TOKEN_TILE
    nb = 1024
    nt = l // tm
    nct = n_ctx // tm
    nj = n_out // nb
    n_conv_blk = qkv_total // nb
    kern = functools.partial(_gd_proj_kernel, tm=tm, n_ctx=n_ctx, n_tot=l, n_qk_blk=2 * kd_total // nb,
                             n_conv_blk=n_conv_blk, n_vh=n_vh)
    return pl.pallas_call(
        kern,
        out_shape=(jax.ShapeDtypeStruct((b, l, n_out), F32), jax.ShapeDtypeStruct((2, b, l, LANES), F32)),
        grid=(b, nt, nj),
        in_specs=_tile_specs(tm, d, nt)
                 + [pl.BlockSpec((1, 1, 6, d), lambda i, t, j: (i, _seg_index(t, nct), 0, 0)),
                    _const_spec(ng.shape),
                    pl.BlockSpec((d, nb), lambda i, t, j: (0, j)),
                    pl.BlockSpec((GD_CONV_W, nb), lambda i, t, j: (0, jnp.minimum(j, n_conv_blk - 1))),
                    _const_spec(wab.shape), _const_spec(alog.shape), _const_spec(dtb.shape)],
        out_specs=(pl.BlockSpec((1, tm, nb), lambda i, t, j: (i, t, j)),
                   pl.BlockSpec((2, 1, tm, LANES), lambda i, t, j: (0, i, t, 0))),
        scratch_shapes=[pltpu.VMEM((tm + 2 * HALO, d), BF16)],
        compiler_params=_params(3),
        name="gd_proj",
    )(xs, xs, xs, mod, ng, w_in, conv_w, wab, alog, dtb)


def _gd_chunk_kernel(q0_ref, k0_ref, v0_ref, gb0_ref, q1_ref, k1_ref, v1_ref, gb1_ref,
                     o0_ref, o1_ref, s_ref, *, n_vh, heads_per_group):
    c = q0_ref.shape[1]
    n = heads_per_group
    rep = n_vh // (q0_ref.shape[2] // GD_DK)
    n_groups = n_vh // n
    gw = n * c
    kh_per_group = n // rep
    assert 2 * c == LANES and n_vh % 2 == 0

    @pl.when(pl.program_id(1) == 0)
    def _():
        s_ref[...] = jnp.zeros_like(s_ref)

    bd = _block_mask(gw, gw, c, c)
    bd_k = _block_mask(gw, kh_per_group * GD_DK, c, GD_DK, rdiv=rep)
    bd_v = _block_mask(gw, n * GD_DV, c, GD_DV)
    low_half = lax.broadcasted_iota(jnp.int32, (1, LANES), 1) < c

    def bdiag_v(x):
        return jnp.where(bd_v, _tile_rows(x, n), 0.0)

    def spread_dv(x, off):
        return jnp.concatenate([jnp.broadcast_to(x[:, off + h:off + h + 1], (c, GD_DV)) for h in range(n_vh)], axis=1)

    def spread_c(x, off):
        cols = [jnp.broadcast_to(x[:, off + h:off + h + 1], (c, LANES)) for h in range(n_vh)]
        return jnp.concatenate([jnp.where(low_half, cols[h], cols[h + 1]) for h in range(0, n_vh, 2)], axis=1)

    refs = ((q0_ref, k0_ref, v0_ref, gb0_ref, o0_ref), (q1_ref, k1_ref, v1_ref, gb1_ref, o1_ref))
    prep = []
    for z, (q_ref, k_ref, v_ref, gb_ref, _) in enumerate(refs):
        gb = gb_ref[0, 0]
        gc = _cumsum_rows(gb, z == 1)
        gtot = jnp.sum(gb, axis=0, keepdims=True)
        incl_all, _, _ = _order_masks(z, c, n_vh)
        gt64 = spread_c(gc, 0)
        sq = jnp.concatenate([gc, jnp.zeros((LANES - c, LANES), F32)], axis=0).T
        sq_hi = pltpu.roll(sq, c, axis=1)
        gs64 = jnp.concatenate([jnp.where(low_half, sq[h:h + 1, :], sq_hi[h + 1:h + 2, :])
                                for h in range(0, n_vh, 2)], axis=1)
        prep.append({
            "gam": jnp.where(incl_all, jnp.exp(jnp.where(incl_all, gt64 - gs64, 0.0)), 0.0),
            "bt64": spread_c(gb, n_vh),
            "e_g": spread_dv(jnp.exp(gc), 0),
            "e_end": spread_dv(jnp.exp(gtot - gc), 0),
            "beta": spread_dv(gb, n_vh),
            "gl": jnp.exp(gtot),
            "q": q_ref[0], "k": k_ref[0], "v": v_ref[0], "masks": _order_masks(z, c, n)})

    chains = [(z, gi) for z in range(2) for gi in range(n_groups)]

    def ksl(gi):
        return slice(gi * kh_per_group * GD_DK, (gi + 1) * kh_per_group * GD_DK)

    def csl(gi):
        return slice(gi * gw, (gi + 1) * gw)

    def vsl(gi):
        return slice(gi * n * GD_DV, (gi + 1) * n * GD_DV)

    def per_vhead(x):
        return jnp.concatenate([x[:, (hh // rep) * GD_DK:(hh // rep + 1) * GD_DK] for hh in range(n)], axis=1)

    kg_ = [prep[z]["k"][:, ksl(gi)] for z, gi in chains]
    qg_ = [prep[z]["q"][:, ksl(gi)] for z, gi in chains]
    qkk = [_dot_nt(jnp.concatenate([kx, qx], axis=0), jnp.where(bd_k, _tile_rows(kx, n), 0.0))
           for kx, qx in zip(kg_, qg_)]
    a_mat = [jnp.where(prep[z]["masks"][1], x[:c] * prep[z]["gam"][:, csl(gi)] * prep[z]["bt64"][:, csl(gi)], 0.0)
             for x, (z, gi) in zip(qkk, chains)]
    aqk = [jnp.where(prep[z]["masks"][0], x[c:] * prep[z]["gam"][:, csl(gi)], 0.0) for x, (z, gi) in zip(qkk, chains)]
    t_inv = _tri_inv([-a for a in a_mat], bd, [prep[z]["masks"][2] for z, _ in chains], [z == 1 for z, _ in chains])

    k2 = [per_vhead(x) for x in kg_]
    q2 = [per_vhead(x) for x in qg_]
    bg = [prep[z]["beta"][:, vsl(gi)] for z, gi in chains]
    eg = [prep[z]["e_g"][:, vsl(gi)] for z, gi in chains]
    u = [_dot(t, bdiag_v(prep[z]["v"][:, vsl(gi)] * b_)) for t, b_, (z, gi) in zip(t_inv, bg, chains)]
    w = [_dot(t, bdiag_v(kx * b_ * e_)) for t, kx, b_, e_ in zip(t_inv, k2, bg, eg)]
    qe = [qx * e_ for qx, e_ in zip(q2, eg)]
    ke = [kx * prep[z]["e_end"][:, vsl(gi)] for kx, (z, gi) in zip(k2, chains)]

    heads = [(ci, hh) for ci in range(len(chains)) for hh in range(n)]

    def hs(hh):
        return slice(hh * GD_DV, (hh + 1) * GD_DV)

    def state_index(ci, hh):
        z, gi = chains[ci]
        return z, gi * n + hh

    ws = {(ci, hh): _dot(jnp.concatenate([w[ci][:, hs(hh)], qe[ci][:, hs(hh)]], axis=0), s_ref[state_index(ci, hh)])
          for ci, hh in heads}
    vn = [jnp.concatenate([u[ci][:, hs(hh)] - ws[ci, hh][:c] for hh in range(n)], axis=1) for ci in range(len(chains))]
    for ci, (z, gi) in enumerate(chains):
        pre = jnp.concatenate([ws[ci, hh][c:] for hh in range(n)], axis=1)
        refs[z][4][0, :, vsl(gi)] = pre + _dot(aqk[ci], bdiag_v(vn[ci]))
    for ci, hh in heads:
        z, hv = state_index(ci, hh)
        gl = jnp.broadcast_to(prep[z]["gl"][:, hv:hv + 1], (1, GD_DV))
        s_ref[z, hv] = s_ref[z, hv] * gl + _dot_tn(ke[ci][:, hs(hh)], vn[ci][:, hs(hh)])


def _gd_chunk(proj, gb, n_ctx, kd_total, vd_total):
    b, l, _ = proj.shape
    c = CHUNK
    nc = l // c
    ncc = n_ctx // c
    n_vh = vd_total // GD_DV

    def col(z, width, blk):
        return pl.BlockSpec((1, c, width), lambda i, j: (i, _chunk_index(z, j, ncc, nc), blk))

    def perdir(z):
        return pl.BlockSpec((1, 1, c, LANES), lambda i, j: (z, i, _chunk_index(z, j, ncc, nc), 0))

    def dir_specs(z):
        return [col(z, kd_total, 0), col(z, kd_total, 1), col(z, vd_total, 2 * kd_total // vd_total), perdir(z)]

    sd = jax.ShapeDtypeStruct((b, l, vd_total), F32)
    return pl.pallas_call(
        functools.partial(_gd_chunk_kernel, n_vh=n_vh, heads_per_group=4),
        out_shape=(sd, sd),
        grid=(b, nc),
        in_specs=dir_specs(0) + dir_specs(1),
        out_specs=(col(0, vd_total, 0), col(1, vd_total, 0)),
        scratch_shapes=[pltpu.VMEM((2, n_vh, GD_DK, GD_DV), F32)],
        compiler_params=_params(2),
        name="gd_chunk",
    )(proj, proj, proj, gb, proj, proj, proj, gb)


def _gd_out_kernel(x_ref, o0_ref, o1_ref, z_ref, mod_ref, ngd_ref, wo_ref, xo_ref):
    o = o0_ref[0] + o1_ref[0]
    z = z_ref[0]
    parts = []
    for hh in range(o.shape[1] // GD_DV):
        seg = o[:, hh * GD_DV:(hh + 1) * GD_DV]
        ms = jnp.mean(seg * seg, axis=-1, keepdims=True)
        parts.append(seg * lax.rsqrt(ms + NORM_EPS) * ngd_ref[...])
    on = jnp.concatenate(parts, axis=1) * _silu(z)
    xo_ref[0] = x_ref[0] + mod_ref[0, 0, 2:3, :] * _dot(on, wo_ref[...])


def _gd_out(xs, o, proj, mod, ngd, w_o, n_ctx, z_blk):
    b, l, d = xs.shape
    vd = o[0].shape[-1]
    tm = TOKEN_TILE
    nct = n_ctx // tm
    nt = (l - n_ctx) // tm
    return pl.pallas_call(
        _gd_out_kernel,
        out_shape=jax.ShapeDtypeStruct((b, l - n_ctx, d), F32),
        grid=(b, nt),
        in_specs=[pl.BlockSpec((1, tm, d), lambda i, t: (i, t + nct, 0)),
                  pl.BlockSpec((1, tm, vd), lambda i, t: (i, t + nct, 0)),
                  pl.BlockSpec((1, tm, vd), lambda i, t: (i, t + nct, 0)),
                  pl.BlockSpec((1, tm, vd), lambda i, t: (i, t + nct, z_blk)),
                  pl.BlockSpec((1, 1, 6, d), lambda i, t: (i, 1, 0, 0)),
                  _const_spec(ngd.shape), _const_spec(w_o.shape)],
        out_specs=pl.BlockSpec((1, tm, d), lambda i, t: (i, t, 0)),
        compiler_params=_params(2),
        name="gd_out",
    )(xs, o[0], o[1], proj, mod, ngd, w_o)


def _pos_embed_2d(rows, d):
    quarter = d // 4
    omega = 1.0 / (POS_BASE ** (jnp.arange(quarter, dtype=F32) / quarter))

    def axis_emb(n):
        ang = jnp.arange(n, dtype=F32)[:, None] * omega[None, :]
        return jnp.concatenate([jnp.sin(ang), jnp.cos(ang)], axis=-1)

    e_row = jnp.broadcast_to(axis_emb(rows)[:, None, :], (rows, GRID_W, d // 2))
    e_col = jnp.broadcast_to(axis_emb(GRID_W)[None, :, :], (rows, GRID_W, d // 2))
    return jnp.concatenate([e_row, e_col], axis=-1).reshape(rows * GRID_W, d)


def _head_indicator(d, head):
    ch = jnp.arange(d)[:, None] // head
    ind = (ch == jnp.arange(LANES)[None, :]).astype(BF16)
    return ind, ind.T


def _block_diag2(m):
    z = jnp.zeros_like(m[0])
    return jnp.concatenate([jnp.concatenate([m[0], z], axis=1), jnp.concatenate([z, m[1]], axis=1)], axis=0)


def _moe_params(i, moe_w_rg, moe_b_rg, moe_w_re, moe_b_re, moe_w_gate, moe_w_up, moe_w_down):
    d = moe_w_rg.shape[1]
    ne = MOE_GROUPS * MOE_EXPERTS
    pad = LANES - ne - MOE_GROUPS
    w_r = jnp.concatenate([moe_w_re[i], moe_w_rg[i], jnp.zeros((d, pad), F32)], axis=1)
    b_r = jnp.concatenate([moe_b_re[i], moe_b_rg[i], jnp.zeros((pad,), F32)])[None, :]
    ff = moe_w_gate.shape[-1]
    return {"w_r": w_r, "b_r": b_r,
            "w_gu": jnp.concatenate([_bf(moe_w_gate[i]), _bf(moe_w_up[i])], axis=-1).reshape(ne, d, 2 * ff),
            "w_down": _bf(moe_w_down[i]).reshape(ne, ff, d)}


def kernel(x, c, ctx, c_ctx, ada_w, ada_b, norm1_g, norm2_g, rw_mix, rw_w_rkv, rw_w0, rw_w1, rw_w2, rw_a0, rw_a1, rw_a2, rw_g1, rw_g2, rw_k_k, rw_k_a, rw_r_k, rw_ln_g, rw_ln_b, rw_w_o, gd_w_in, gd_conv, gd_w_ab, gd_a_log, gd_dt_bias, gd_norm_g, gd_w_o, moe_w_rg, moe_b_rg, moe_w_re, moe_b_re, moe_w_gate, moe_w_up, moe_w_down, final_g):
    bsz, n_lat, d = x.shape
    n_ctx = ctx.shape[1]
    assert n_ctx % TOKEN_TILE == 0 and n_lat % TOKEN_TILE == 0 and n_ctx % CHUNK == 0
    assert ada_w.shape[0] == 2 and d % (4 * RW_HEAD) == 0

    rows = -(-(bsz + 1) // HALO) * HALO
    cs = jnp.zeros((rows, d), F32).at[:bsz].set(c).at[bsz].set(c_ctx)
    mod_all = _ada(cs, ada_w, ada_b)

    def mod_of(i):
        lat = mod_all[i, :bsz].reshape(bsz, 1, 6, d)
        cx = jnp.broadcast_to(mod_all[i, bsz].reshape(1, 1, 6, d), (bsz, 1, 6, d))
        return jnp.concatenate([cx, lat], axis=1)

    xs = jnp.concatenate([ctx, x], axis=1)
    pos = jnp.concatenate([jnp.zeros((n_ctx, d), F32), _pos_embed_2d(n_lat // GRID_W, d)], axis=0)
    gsum, gbc = _head_indicator(d, RW_HEAD)

    mod0 = mod_of(0)
    ng1 = norm1_g[0][None, :]
    rw = {"mix": rw_mix[0], "w_rkv": _bf(rw_w_rkv[0]), "w0": rw_w0[0],
          "w1": _bf(jnp.concatenate([rw_w1[0, 0], rw_w1[0, 1]], axis=1)), "w2": _bf(_block_diag2(rw_w2[0])),
          "a0": rw_a0[0],
          "a1": _bf(jnp.concatenate([rw_a1[0, 0], rw_a1[0, 1]], axis=1)), "a2": _bf(_block_diag2(rw_a2[0])),
          "g1": _bf(rw_g1[0]), "g2": _bf(rw_g2[0]), "k_k": rw_k_k[0][None, :], "k_a": rw_k_a[0][None, :],
          "r_k": rw_r_k[0].reshape(1, d), "ln_g": rw_ln_g[0][None, :], "ln_b": rw_ln_b[0][None, :],
          "w_o": _bf(rw_w_o[0])}
    r, v, kk, gate, lw, kd, ar = _rw_feat(xs, pos, mod0, ng1, rw, gsum, gbc, n_ctx)
    y = _rw_scan(r, v, kk, lw, kd, ar, n_ctx)
    xs = _rw_out(xs, pos, y, r, v, kd, gate, mod0, rw, gsum, gbc, n_ctx)
    moe_args = (moe_w_rg, moe_b_rg, moe_w_re, moe_b_re, moe_w_gate, moe_w_up, moe_w_down)
    fg = final_g[None, :]
    xs = _moe(xs, mod0, norm2_g[0][None, :], _moe_params(0, *moe_args), fg, n_ctx, False)

    mod1 = mod_of(1)
    n_vh = gd_a_log.shape[-1]
    vd_total = n_vh * GD_DV
    kd_total = (gd_w_in.shape[-1] - 2 * vd_total) // 2
    qkv_total = 2 * kd_total + vd_total
    ng1 = norm1_g[1][None, :]
    wab = jnp.concatenate([gd_w_ab[0, 0], gd_w_ab[0, 1], jnp.zeros((d, LANES - 4 * n_vh), F32)], axis=1)
    lane_pad = jnp.zeros((2, LANES - n_vh), F32)
    alog = jnp.concatenate([gd_a_log[0], lane_pad], axis=1)
    dtb = jnp.concatenate([gd_dt_bias[0], lane_pad], axis=1)
    proj, gb = _gd_proj(xs, mod1, ng1, _bf(gd_w_in[0]), gd_conv[0], wab, alog, dtb, n_ctx, kd_total, qkv_total, n_vh)
    o = _gd_chunk(proj, gb, n_ctx, kd_total, vd_total)
    ngd = gd_norm_g[0][None, :]
    x_lat = _gd_out(xs, o, proj, mod1, ngd, _bf(gd_w_o[0]), n_ctx, qkv_total // vd_total)
    return _moe(x_lat, mod1, norm2_g[1][None, :], _moe_params(1, *moe_args), fg, 0, True)
```

```python
import functools
import math

import jax
import jax.numpy as jnp
from jax import lax
from jax.experimental import pallas as pl
from jax.experimental.pallas import tpu as pltpu

F32 = jnp.float32
BF16 = jnp.bfloat16

NORM_EPS = 1e-6
RW_GN_EPS = 64e-5
POS_BASE = 10000.0
GRID_W = 64
RW_HEAD = 64
RW_LORA = 64
GD_DK = 128
GD_DV = 128
GD_CONV_W = 5
MOE_GROUPS = 4
MOE_EXPERTS = 8
CHUNK = 64
LANES = 128
HALO = 8
TOKEN_TILE = 256
VMEM_LIMIT = 56 * 1024 * 1024
NEG_BIG = -1e30


def _bf(x):
    return x.astype(BF16)


def _dot(a, b):
    return jnp.dot(_bf(a), _bf(b), preferred_element_type=F32)


def _dot_nt(a, b):
    return lax.dot_general(_bf(a), _bf(b), (((1,), (1,)), ((), ())), preferred_element_type=F32)


def _dot_tn(a, b):
    return lax.dot_general(_bf(a), _bf(b), (((0,), (0,)), ((), ())), preferred_element_type=F32)


def _split2(x):
    hi = x.astype(BF16)
    lo = (x - hi.astype(F32)).astype(BF16)
    return hi, lo


def _split3(x):
    hi = x.astype(BF16)
    r1 = x - hi.astype(F32)
    mid = r1.astype(BF16)
    lo = (r1 - mid.astype(F32)).astype(BF16)
    return hi, mid, lo


def _dot_x2(a, b_exact):
    hi, lo = _split2(a)
    b = _bf(b_exact)
    return (jnp.dot(hi, b, preferred_element_type=F32) + jnp.dot(lo, b, preferred_element_type=F32))


def _dot_x3(a, b_exact):
    hi, mid, lo = _split3(a)
    b = _bf(b_exact)
    return (jnp.dot(hi, b, preferred_element_type=F32) + jnp.dot(mid, b, preferred_element_type=F32)
            + jnp.dot(lo, b, preferred_element_type=F32))


def _dot_ex3(a_exact, b):
    hi, mid, lo = _split3(b)
    a = _bf(a_exact)
    return (jnp.dot(a, hi, preferred_element_type=F32) + jnp.dot(a, mid, preferred_element_type=F32)
            + jnp.dot(a, lo, preferred_element_type=F32))


def _dot_hh(a, b):
    ah, al = _split2(a)
    bh, bl = _split2(b)
    return (jnp.dot(ah, bh, preferred_element_type=F32) + jnp.dot(al, bh, preferred_element_type=F32)
            + jnp.dot(ah, bl, preferred_element_type=F32))


def _sigmoid(x):
    return 1.0 / (1.0 + jnp.exp(-x))


def _silu(x):
    return x * _sigmoid(x)


def _softplus(x):
    return jnp.maximum(x, 0.0) + jnp.log(1.0 + jnp.exp(-jnp.abs(x)))


def _norm_mod(x, g, shift, scale):
    ms = jnp.mean(x * x, axis=-1, keepdims=True)
    return (x * lax.rsqrt(ms + NORM_EPS) * g) * (1.0 + scale) + shift


def _tile_rows(x, n):
    return jnp.concatenate([x] * n, axis=0)


def _tile_lanes(x, n):
    return jnp.concatenate([x] * n, axis=1)


def _order_masks(direction, c, n=1):
    row = lax.broadcasted_iota(jnp.int32, (c, n * c), 0)
    col = lax.broadcasted_iota(jnp.int32, (c, n * c), 1) % c
    diff = (col - row) * (1 - 2 * direction)
    return diff <= 0, diff < 0, jnp.where(diff == 0, 1.0, 0.0)


def _cumsum_rows(x, reverse):
    n = x.shape[0]
    row = lax.broadcasted_iota(jnp.int32, (n, 1), 0)
    s = 1
    while s < n:
        if reverse:
            shifted = jnp.where(row < n - s, pltpu.roll(x, n - s, axis=0), 0.0)
        else:
            shifted = jnp.where(row >= s, pltpu.roll(x, s, axis=0), 0.0)
        x = x + shifted
        s *= 2
    return x


def _block_mask(rows, cols, rblk, cblk, rdiv=1):
    r = lax.broadcasted_iota(jnp.int32, (rows, cols), 0) // rblk
    c = lax.broadcasted_iota(jnp.int32, (rows, cols), 1) // cblk
    return (r // rdiv) == c if rdiv != 1 else r == c


TRI_BASE = 4
SCAN_BATCH = 2


def _tri_inv(a_list, bd, eye_list, rev_list):
    c = a_list[0].shape[0]
    n = a_list[0].shape[1] // c
    row = lax.broadcasted_iota(jnp.int32, a_list[0].shape, 0)
    col = lax.broadcasted_iota(jnp.int32, a_list[0].shape, 1) % c

    def same_block(m):
        return (row // m) == (col // m)

    def mm(x, y):
        return jnp.dot(_bf(x), _bf(jnp.where(bd, _tile_rows(y, n), 0.0)), preferred_element_type=F32)

    base = same_block(TRI_BASE)
    a0 = [jnp.where(base, a, 0.0) for a in a_list]
    x = [e + a for e, a in zip(eye_list, a0)]
    p = [mm(a, a) for a in a0]
    base_levels = int(math.log2(TRI_BASE)) - 1
    for lvl in range(base_levels):
        x = [xi + mm(pi, xi) for xi, pi in zip(x, p)]
        if lvl + 1 < base_levels:
            p = [mm(pi, pi) for pi in p]
    m = TRI_BASE
    while m < c:
        off = jnp.logical_and(same_block(2 * m), jnp.logical_not(same_block(m)))
        a_off = [jnp.where(off, a, 0.0) for a in a_list]
        if m % HALO:
            t = [mm(ao, xi) for ao, xi in zip(a_off, x)]
            x = [xi + mm(xi, ti) for xi, ti in zip(x, t)]
        else:
            zeros = jnp.zeros((m, a_list[0].shape[1]), F32)

            def pick(v, rev):
                return jnp.concatenate([v[(2 * i + (0 if rev else 1)) * m:(2 * i + (0 if rev else 1) + 1) * m]
                                        for i in range(c // (2 * m))], axis=0)

            def place(v, rev):
                parts = []
                for i in range(c // (2 * m)):
                    blk = v[i * m:(i + 1) * m]
                    parts += [blk, zeros] if rev else [zeros, blk]
                return jnp.concatenate(parts, axis=0)

            t = [place(mm(pick(ao, rev), xi), rev) for ao, xi, rev in zip(a_off, x, rev_list)]
            x = [xi + place(mm(pick(xi, rev), ti), rev) for xi, ti, rev in zip(x, t, rev_list)]
        m *= 2
    return x


def _chunk_index(direction, j, n_ctx_chunks, n_chunks):
    bwd = jnp.where(j < n_ctx_chunks, n_ctx_chunks - 1 - j, n_chunks + n_ctx_chunks - 1 - j)
    return jnp.where(direction == 0, j, bwd)


def _const_spec(shape):
    nd = len(shape)
    return pl.BlockSpec(shape, lambda *_: (0,) * nd, pipeline_mode=pl.Buffered(1))


def _params(n_axes):
    return pltpu.CompilerParams(dimension_semantics=("arbitrary",) * n_axes, vmem_limit_bytes=VMEM_LIMIT)


def _ada_kernel(cs_ref, w_ref, b_ref, o_ref):
    s = _silu(cs_ref[...])
    o_ref[0] = _dot_hh(s, w_ref[0]) + b_ref[0]


def _ada(cs, ada_w, ada_b):
    n_layers, d, n6 = ada_w.shape
    rows = cs.shape[0]
    tn = 1536
    return pl.pallas_call(
        _ada_kernel,
        out_shape=jax.ShapeDtypeStruct((n_layers, rows, n6), F32),
        grid=(n_layers, n6 // tn),
        in_specs=[pl.BlockSpec((rows, d), lambda i, j: (0, 0)),
                  pl.BlockSpec((1, d, tn), lambda i, j: (i, 0, j)),
                  pl.BlockSpec((1, 1, tn), lambda i, j: (i, 0, j))],
        out_specs=pl.BlockSpec((1, rows, tn), lambda i, j: (i, 0, j)),
        compiler_params=_params(2),
        name="ada",
    )(cs, ada_w, ada_b.reshape(n_layers, 1, n6))


def _shifted_neighbours(h, h_prev, h_next, pos, tm, n_ctx, n_tot):
    prev_ok = jnp.logical_and(pos != 0, pos != n_ctx)
    next_ok = jnp.logical_and(pos + tm != n_ctx, pos + tm != n_tot)
    h_prev = jnp.where(prev_ok, h_prev, 0.0)
    h_next = jnp.where(next_ok, h_next, 0.0)
    row = lax.broadcasted_iota(jnp.int32, (tm, 1), 0)
    prev = jnp.where(row == 0, h_prev, pltpu.roll(h, 1, axis=0))
    nxt = jnp.where(row == tm - 1, h_next, pltpu.roll(h, tm - 1, axis=0))
    return prev, nxt


def _tile_specs(tm, d, n_tiles, lead=()):
    nl = len(lead)
    hb = tm // HALO
    last = n_tiles * hb - 1

    def cur(*g):
        return (g[nl], g[nl + 1], 0)

    def prv(*g):
        return (g[nl], jnp.maximum(g[nl + 1] * hb - 1, 0), 0)

    def nxt(*g):
        return (g[nl], jnp.minimum((g[nl + 1] + 1) * hb, last), 0)

    return [pl.BlockSpec((1, tm, d), cur), pl.BlockSpec((1, HALO, d), prv), pl.BlockSpec((1, HALO, d), nxt)]


def _rw_feat_kernel(x_ref, xp_ref, xn_ref, pos_ref, posp_ref, posn_ref, mod_ref, ng_ref, mix_ref,
                    wrkv_ref, w0_ref, w1_ref, w2_ref, a0_ref, a1_ref, a2_ref, g1_ref, g2_ref,
                    kkp_ref, kap_ref, gsum_ref, gbc_ref,
                    r_o, v_o, kk_o, gate_o, lw_o, kd_o, a_o, *, tm, n_ctx, n_tot):
    d = x_ref.shape[-1]
    pos = pl.program_id(1) * tm
    sh = mod_ref[0, 0, 0:1, :]
    sc = mod_ref[0, 0, 1:2, :]
    g = ng_ref[...]
    h = _norm_mod(x_ref[0] + pos_ref[...], g, sh, sc)
    hp = _norm_mod(xp_ref[0, HALO - 1:HALO, :] + posp_ref[HALO - 1:HALO, :], g, sh, sc)
    hn = _norm_mod(xn_ref[0, 0:1, :] + posn_ref[0:1, :], g, sh, sc)
    prev, nxt = _shifted_neighbours(h, hp, hn, pos, tm, n_ctx, n_tot)
    xx = 0.5 * (prev + nxt) - h
    mix = mix_ref[...]
    xr = h + xx * mix[0:1]
    xw = h + xx * mix[1:2]
    xk = h + xx * mix[2:3]
    xv = h + xx * mix[3:4]
    xa = h + xx * mix[4:5]
    xg = h + xx * mix[5:6]

    r_o[0] = _bf(_dot(xr, wrkv_ref[0]))
    v_o[0] = _bf(_dot(xv, wrkv_ref[2]))
    gate_o[0] = _bf(_dot(_sigmoid(_dot(xg, g1_ref[...])), g2_ref[...]))
    k = _dot(xk, wrkv_ref[1])

    kq = k * kkp_ref[...]
    ssq = _dot_x2(kq * kq, gsum_ref[...])
    kk_o[0] = _bf(kq * _dot_x2(lax.rsqrt(ssq + NORM_EPS), gbc_ref[...]))

    wl = _dot(jnp.tanh(_dot(xw, w1_ref[...])), w2_ref[...])
    al = _dot(_dot(xa, a1_ref[...]), a2_ref[...])
    ka = kap_ref[...]
    for z in range(2):
        zw = w0_ref[z:z + 1, :] + wl[:, z * d:(z + 1) * d]
        lw_o[z, 0] = (-math.exp(-0.5)) * _sigmoid(zw)
        rate = _sigmoid(a0_ref[z:z + 1, :] + al[:, z * d:(z + 1) * d])
        a_o[z, 0] = _bf(rate)
        kd_o[z, 0] = _bf(k * (1.0 + (rate - 1.0) * ka))


def _rw_feat(xs, pos, mod, ng, p, gsum, gbc, n_ctx):
    b, l, d = xs.shape
    tm = TOKEN_TILE
    nt = l // tm
    nct = n_ctx // tm
    hb = tm // HALO
    kern = functools.partial(_rw_feat_kernel, tm=tm, n_ctx=n_ctx, n_tot=l)
    tok = pl.BlockSpec((1, tm, d), lambda i, t: (i, t, 0))
    tok2 = pl.BlockSpec((2, 1, tm, d), lambda i, t: (0, i, t, 0))
    pos_specs = [pl.BlockSpec((tm, d), lambda i, t: (t, 0)),
                 pl.BlockSpec((HALO, d), lambda i, t: (jnp.maximum(t * hb - 1, 0), 0)),
                 pl.BlockSpec((HALO, d), lambda i, t: (jnp.minimum((t + 1) * hb, nt * hb - 1), 0))]
    in_specs = (_tile_specs(tm, d, nt) + pos_specs
                + [pl.BlockSpec((1, 1, 6, d), lambda i, t: (i, _seg_index(t, nct), 0, 0))]
                + [_const_spec(a.shape) for a in (ng, p["mix"], p["w_rkv"], p["w0"], p["w1"], p["w2"], p["a0"],
                                                  p["a1"], p["a2"], p["g1"], p["g2"], p["k_k"], p["k_a"], gsum, gbc)])
    sd = jax.ShapeDtypeStruct((b, l, d), BF16)
    sd2 = jax.ShapeDtypeStruct((2, b, l, d), BF16)
    return pl.pallas_call(
        kern,
        out_shape=(sd, sd, sd, sd, jax.ShapeDtypeStruct((2, b, l, d), F32), sd2, sd2),
        grid=(b, nt),
        in_specs=in_specs,
        out_specs=(tok, tok, tok, tok, tok2, tok2, tok2),
        compiler_params=_params(2),
        name="rw_feat",
    )(xs, xs, xs, pos, pos, pos, mod, ng, p["mix"], p["w_rkv"], p["w0"], p["w1"], p["w2"], p["a0"], p["a1"], p["a2"],
      p["g1"], p["g2"], p["k_k"], p["k_a"], gsum, gbc)


def _rw_scan_kernel(r0_ref, v0_ref, kk0_ref, lw0_ref, kd0_ref, a0_ref,
                    r1_ref, v1_ref, kk1_ref, lw1_ref, kd1_ref, a1_ref,
                    y0_ref, y1_ref, s_ref, *, heads_per_group):
    c = r0_ref.shape[1]
    d = r0_ref.shape[2]
    n = heads_per_group
    gw = n * RW_HEAD
    n_groups = d // gw
    nt = (((1,), (1,)), ((), ()))

    @pl.when(pl.program_id(1) == 0)
    def _():
        s_ref[...] = jnp.zeros_like(s_ref)

    bd = _block_mask(n * c, gw, c, RW_HEAD)

    def bdiag(x):
        return jnp.where(bd, _tile_rows(x, n), 0.0)

    refs = ((r0_ref, v0_ref, kk0_ref, lw0_ref, kd0_ref, a0_ref, y0_ref),
            (r1_ref, v1_ref, kk1_ref, lw1_ref, kd1_ref, a1_ref, y1_ref))
    masks = [_order_masks(z, c, n) for z in range(2)]
    prep = {}
    for bi in range(r0_ref.shape[0]):
        for z, (r_ref, v_ref, kk_ref, lw_ref, kd_ref, a_ref, _) in enumerate(refs):
            lw = lw_ref[0, bi]
            lam = _cumsum_rows(lw, z == 1)
            tot = jnp.sum(lw, axis=0, keepdims=True)
            w_inv = jnp.exp(-lam)
            w_end = jnp.exp(tot - lam)
            kk = kk_ref[bi].astype(F32)
            kd = kd_ref[0, bi].astype(F32)
            bb = kk * a_ref[0, bi].astype(F32)
            prep[bi, z] = {"a_t": -(kk * jnp.exp(lam - lw)), "r_t": r_ref[bi].astype(F32) * jnp.exp(lam),
                           "b_t": bb * w_inv, "k_t": kd * w_inv, "b_e": bb * w_end, "k_e": kd * w_end,
                           "v": v_ref[bi].astype(F32),
                           "w_tot": jnp.exp(tot)}

    chains = [(bi, z, gi) for bi in range(r0_ref.shape[0]) for z in range(2) for gi in range(n_groups)]

    def sl(gi):
        return slice(gi * gw, (gi + 1) * gw)

    def part(name, ch):
        bi, z, gi = ch
        return prep[bi, z][name][:, sl(gi)]

    lhs = [_bf(jnp.concatenate([part("a_t", ch), part("r_t", ch)], axis=0)) for ch in chains]
    ab = [lax.dot_general(lh, _bf(bdiag(part("b_t", ch))), nt, preferred_element_type=F32)
          for lh, ch in zip(lhs, chains)]
    ak = [lax.dot_general(lh, _bf(bdiag(part("k_t", ch))), nt, preferred_element_type=F32)
          for lh, ch in zip(lhs, chains)]
    a_ab = [jnp.where(masks[z][1], x[:c], 0.0) for x, (_, z, _) in zip(ab, chains)]
    a_rb = [jnp.where(masks[z][0], x[c:], 0.0) for x, (_, z, _) in zip(ab, chains)]
    a_kk = [jnp.concatenate([jnp.where(masks[z][1], x[:c], 0.0), jnp.where(masks[z][0], x[c:], 0.0)], axis=0)
            for x, (_, z, _) in zip(ak, chains)]
    st = [s_ref[ch] for ch in chains]
    pre = [lax.dot_general(lh, _bf(s), nt, preferred_element_type=F32) for lh, s in zip(lhs, st)]
    vg = [part("v", ch) for ch in chains]
    av = [_dot(x, bdiag(v)) for x, v in zip(a_kk, vg)]
    t_inv = _tri_inv(a_ab, bd, [masks[z][2] for _, z, _ in chains], [z == 1 for _, z, _ in chains])
    u = [_dot(t, bdiag(p[:c] + q[:c])) for t, p, q in zip(t_inv, pre, av)]
    yv = [p[c:] + q[c:] + _dot(x, bdiag(ui)) for p, q, x, ui in zip(pre, av, a_rb, u)]
    upd = [_dot_tn(jnp.concatenate([ui, v], axis=0), jnp.concatenate([part("b_e", ch), part("k_e", ch)], axis=0))
           for ui, v, ch in zip(u, vg, chains)]
    for ch, y_val, s, up in zip(chains, yv, st, upd):
        bi, z, gi = ch
        refs[z][6][bi, :, sl(gi)] = y_val
        s_ref[ch] = s * part("w_tot", ch) + jnp.where(bd, up, 0.0)


def _rw_scan(r, v, kk, lw, kd, a, n_ctx):
    b, l, d = r.shape
    c = CHUNK
    nc = l // c
    ncc = n_ctx // c
    hpg = 4
    gw = hpg * RW_HEAD
    bpb = SCAN_BATCH if b % SCAN_BATCH == 0 else 1

    def shared(z):
        return pl.BlockSpec((bpb, c, d), lambda i, j: (i, _chunk_index(z, j, ncc, nc), 0))

    def perdir(z):
        return pl.BlockSpec((1, bpb, c, d), lambda i, j: (z, i, _chunk_index(z, j, ncc, nc), 0))

    sd = jax.ShapeDtypeStruct((b, l, d), F32)
    return pl.pallas_call(
        functools.partial(_rw_scan_kernel, heads_per_group=hpg),
        out_shape=(sd, sd),
        grid=(b // bpb, nc),
        in_specs=[shared(0)] * 3 + [perdir(0)] * 3 + [shared(1)] * 3 + [perdir(1)] * 3,
        out_specs=(shared(0), shared(1)),
        scratch_shapes=[pltpu.VMEM((bpb, 2, d // gw, gw, gw), F32)],
        compiler_params=_params(2),
        name="rw_scan",
    )(r, v, kk, lw, kd, a, r, v, kk, lw, kd, a)


def _rw_out_kernel(x_ref, pos_ref, y0_ref, y1_ref, r_ref, v_ref, kd_ref, gate_ref, mod_ref, rk_ref, lng_ref, lnb_ref,
                   wo_ref, gsum_ref, gbc_ref, xo_ref):
    inv_n = 1.0 / RW_HEAD
    gsum = gsum_ref[...]
    gbc = gbc_ref[...]
    y = y0_ref[0] + y1_ref[0]
    mu = _dot_x2(_dot_x2(y, gsum) * inv_n, gbc)
    yc = y - mu
    var = _dot_x2(yc * yc, gsum) * inv_n
    yn = yc * _dot_x2(lax.rsqrt(var + RW_GN_EPS), gbc) * lng_ref[...] + lnb_ref[...]
    k_bonus = 0.5 * (kd_ref[0, 0].astype(F32) + kd_ref[1, 0].astype(F32))
    bonus = _dot_x2(_dot_x2(r_ref[0].astype(F32) * k_bonus * rk_ref[...], gsum), gbc) * v_ref[0].astype(F32)
    out = (yn + bonus) * gate_ref[0].astype(F32)
    o = _dot(out, wo_ref[...])
    xo_ref[0] = x_ref[0] + pos_ref[...] + mod_ref[0, 0, 2:3, :] * o


def _rw_out(xs, pos, y, r, v, kd, gate, mod, p, gsum, gbc, n_ctx):
    b, l, d = xs.shape
    tm = TOKEN_TILE
    nt = l // tm
    nct = n_ctx // tm
    tok = pl.BlockSpec((1, tm, d), lambda i, t: (i, t, 0))
    tok2 = pl.BlockSpec((2, 1, tm, d), lambda i, t: (0, i, t, 0))
    consts = (p["r_k"], p["ln_g"], p["ln_b"], p["w_o"], gsum, gbc)
    return pl.pallas_call(
        _rw_out_kernel,
        out_shape=jax.ShapeDtypeStruct((b, l, d), F32),
        grid=(b, nt),
        in_specs=[tok, pl.BlockSpec((tm, d), lambda i, t: (t, 0)), tok, tok, tok, tok, tok2, tok,
                  pl.BlockSpec((1, 1, 6, d), lambda i, t: (i, _seg_index(t, nct), 0, 0))]
                 + [_const_spec(a.shape) for a in consts],
        out_specs=tok,
        compiler_params=_params(2),
        name="rw_out",
    )(xs, pos, y[0], y[1], r, v, kd, gate, mod, *consts)


def _route(logits):
    ne = MOE_GROUPS * MOE_EXPERTS
    lane = lax.broadcasted_iota(jnp.int32, logits.shape, 1).astype(F32)
    far = float(4 * LANES)
    g_mask = jnp.logical_and(lane >= ne, lane < ne + MOE_GROUPS)
    gl = jnp.where(g_mask, logits, NEG_BIG)
    gmax = jnp.max(gl, axis=1, keepdims=True)
    gsum = jnp.sum(jnp.where(g_mask, jnp.exp(gl - gmax), 0.0), axis=1, keepdims=True)
    gp = 1.0 / gsum
    g_first = jnp.min(jnp.where(gl == gmax, lane, far), axis=1, keepdims=True) - ne
    in_grp = jnp.logical_and(lane >= g_first * MOE_EXPERTS, lane < (g_first + 1.0) * MOE_EXPERTS)
    el = jnp.where(in_grp, logits, NEG_BIG)
    m1 = jnp.max(el, axis=1, keepdims=True)
    i1 = jnp.min(jnp.where(el == m1, lane, far), axis=1, keepdims=True)
    el2 = jnp.where(lane == i1, NEG_BIG, el)
    m2 = jnp.max(el2, axis=1, keepdims=True)
    i2 = jnp.min(jnp.where(el2 == m2, lane, far), axis=1, keepdims=True)
    e2 = jnp.exp(m2 - m1)
    p1 = gp / (1.0 + e2)
    p2 = p1 * e2
    comb = jnp.where(lane == i1, p1, jnp.where(lane == i2, p2, 0.0))
    return jnp.where(lane == g_first + ne, 1.0, comb)


def _seg_index(t, n_ctx_tiles):
    return jnp.where(t >= n_ctx_tiles, 1, 0)


def _moe_route_kernel(x_ref, mod_ref, ng_ref, wr_ref, br_ref, h_ref):
    d = x_ref.shape[2]
    h = _norm_mod(x_ref[0], ng_ref[...], mod_ref[0, 0, 3:4, :], mod_ref[0, 0, 4:5, :])
    h_ref[0, :, 0:d] = _bf(h)
    hi, lo = _split2(_route(_dot_hh(h, wr_ref[...]) + br_ref[...]))
    h_ref[0, :, d:d + LANES] = hi
    h_ref[0, :, d + LANES:d + 2 * LANES] = lo


def _moe_route(xs, mod, ng, p, n_ctx):
    b, rows, d = xs.shape
    tm = TOKEN_TILE
    nct = n_ctx // tm
    return pl.pallas_call(
        _moe_route_kernel,
        out_shape=jax.ShapeDtypeStruct((b, rows, d + 2 * LANES), BF16),
        grid=(b, rows // tm),
        in_specs=[pl.BlockSpec((1, tm, d), lambda i, t: (i, t, 0)),
                  pl.BlockSpec((1, 1, 6, d), lambda i, t: (i, _seg_index(t, nct), 0, 0)),
                  _const_spec(ng.shape), _const_spec(p["w_r"].shape), _const_spec(p["b_r"].shape)],
        out_specs=pl.BlockSpec((1, tm, d + 2 * LANES), lambda i, t: (i, t, 0)),
        compiler_params=_params(2),
        name="moe_route",
    )(xs, mod, ng, p["w_r"], p["b_r"])


MOE_ROW_BLOCK = 128
MOE_EXPERTS_PER_STEP = 2


def _moe_expert_kernel(h_ref, tri_ref, upper_ref, wgu_ref, wd_ref, m_ref, xs_s, acc_s, dest_s, seg_s, *, tb):
    rows = h_ref.shape[1]
    slots = xs_s.shape[0]
    d = acc_s.shape[1]
    ff = wd_ref.shape[1]
    ne = MOE_GROUPS * MOE_EXPERTS
    rb = MOE_ROW_BLOCK
    e = pl.program_id(1)
    n_tb = rows // tb

    @pl.when(e == 0)
    def _():
        lane = lax.broadcasted_iota(jnp.int32, (1, LANES), 1)
        g_lanes = jnp.logical_and(lane >= ne, lane < ne + MOE_GROUPS)
        carry = jnp.zeros((1, LANES), F32)
        ranks = []
        for i in range(n_tb):
            gind = jnp.where(g_lanes, h_ref[0, i * tb:(i + 1) * tb, d:d + LANES].astype(F32), 0.0)
            r = jnp.dot(tri_ref[...], _bf(gind), preferred_element_type=F32) + carry
            carry = r[tb - 1:tb, :]
            ranks.append((gind, r))
        padded = jnp.ceil(carry * (1.0 / rb)) * rb
        start = _dot_x3(jnp.broadcast_to(padded, (HALO, LANES)), upper_ref[...])[0:1]
        for g in range(MOE_GROUPS):
            pick = lane == ne + g
            seg_s[g] = jnp.sum(jnp.where(pick, start, 0.0)).astype(jnp.int32)
            seg_s[MOE_GROUPS + g] = jnp.sum(jnp.where(pick, padded, 0.0)).astype(jnp.int32)
        dest_rows = []
        for i, (gind, r) in enumerate(ranks):
            dest = jnp.sum(gind * (start + r - 1.0), axis=1, keepdims=True)
            dest_b = jnp.broadcast_to(dest, (tb, LANES))
            dest_s[i * tb:(i + 1) * tb, :] = dest_b
            dest_rows.append(dest_b.T[0:1, :])
        dest_row = jnp.concatenate(dest_rows, axis=1)
        hb = h_ref[0]

        def permute(sb, c_):
            base = pl.multiple_of(sb * tb, tb)
            slot = (lax.broadcasted_iota(jnp.int32, (tb, rows), 0) + base).astype(F32)
            perm = jnp.where(slot == dest_row, 1.0, 0.0)
            xs_s[pl.ds(base, tb), :] = _bf(jnp.dot(_bf(perm), hb, preferred_element_type=F32))
            acc_s[pl.ds(base, tb), :] = jnp.zeros((tb, d), F32)
            return c_
        lax.fori_loop(0, slots // tb, permute, 0)

    n_here = wgu_ref.shape[0]
    grp = (e * n_here) // MOE_EXPERTS
    start = seg_s[grp]
    n_rows = seg_s[MOE_GROUPS + grp]

    def expert_rows(firsts, size):
        sls = [pl.ds(pl.multiple_of(first, rb), size) for first in firsts]
        xbs = [xs_s[sl, :] for sl in sls]
        work = [(bi, k) for bi in range(len(sls)) for k in range(n_here)]
        gus = [jnp.dot(xbs[bi][:, 0:d], wgu_ref[k], preferred_element_type=F32) for bi, k in work]
        css = [xb[:, d:d + LANES].astype(F32) + xb[:, d + LANES:d + 2 * LANES].astype(F32) for xb in xbs]
        acts = []
        for (bi, k), gu in zip(work, gus):
            lane = lax.broadcasted_iota(jnp.int32, css[bi].shape, 1)
            ce = jnp.sum(jnp.where(lane == e * n_here + k, css[bi], 0.0), axis=1, keepdims=True)
            acts.append(_bf(_silu(gu[:, 0:ff]) * gu[:, ff:2 * ff] * ce))
        outs = [jnp.dot(act, wd_ref[k], preferred_element_type=F32) for act, (_, k) in zip(acts, work)]
        for bi, sl in enumerate(sls):
            acc_s[sl, :] += sum(out for out, (bj, _) in zip(outs, work) if bj == bi)

    big_rows = 2 * rb
    n_pairs = n_rows // (2 * big_rows)

    def pair(i, c_):
        first = start + i * (2 * big_rows)
        expert_rows([first, first + big_rows], big_rows)
        return c_
    lax.fori_loop(0, n_pairs, pair, 0)
    done = n_pairs * (2 * big_rows)
    rest = n_rows - done

    @pl.when(rest >= big_rows)
    def _():
        expert_rows([start + done], big_rows)

    @pl.when(rest % big_rows > 0)
    def _():
        expert_rows([start + done + (rest // big_rows) * big_rows], rb)

    @pl.when(e == pl.num_programs(1) - 1)
    def _():
        def narrow(sb, c_):
            sl = pl.ds(pl.multiple_of(sb * tb, tb), tb)
            xs_s[sl, 0:d] = _bf(acc_s[sl, :])
            return c_
        lax.fori_loop(0, slots // tb, narrow, 0)
        acc = xs_s[:, 0:d]
        for i in range(n_tb):
            slot = lax.broadcasted_iota(jnp.int32, (tb, slots), 1).astype(F32)
            unperm = jnp.where(slot == dest_s[i * tb:(i + 1) * tb, 0:1], 1.0, 0.0)
            m_ref[0, i * tb:(i + 1) * tb, :] = _bf(jnp.dot(_bf(unperm), acc, preferred_element_type=F32))


def _moe_experts(h, p):
    b, rows, da = h.shape
    ne, d, ff2 = p["w_gu"].shape
    ff = ff2 // 2
    tb = TOKEN_TILE
    eps = MOE_EXPERTS_PER_STEP
    assert MOE_EXPERTS % eps == 0
    slots = rows + tb * (-(-(MOE_GROUPS * MOE_ROW_BLOCK) // tb))
    tri = jnp.tril(jnp.ones((tb, tb), BF16))
    upper = jnp.triu(jnp.ones((LANES, LANES), BF16), 1)
    return pl.pallas_call(
        functools.partial(_moe_expert_kernel, tb=tb),
        out_shape=jax.ShapeDtypeStruct((b, rows, d), BF16),
        grid=(b, ne // eps),
        in_specs=[pl.BlockSpec((1, rows, da), lambda i, e: (i, 0, 0), pipeline_mode=pl.Buffered(1)),
                  _const_spec(tri.shape), _const_spec(upper.shape),
                  pl.BlockSpec((eps, d, ff2), lambda i, e: (e, 0, 0)),
                  pl.BlockSpec((eps, ff, d), lambda i, e: (e, 0, 0))],
        out_specs=pl.BlockSpec((1, rows, d), lambda i, e: (i, 0, 0)),
        scratch_shapes=[pltpu.VMEM((slots, da), BF16), pltpu.VMEM((slots, d), F32),
                        pltpu.VMEM((rows, LANES), F32), pltpu.SMEM((2 * MOE_GROUPS,), jnp.int32)],
        compiler_params=_params(2),
        name="moe_experts",
    )(h, tri, upper, p["w_gu"], p["w_down"])


def _moe_resid_kernel(x_ref, m_ref, mod_ref, fg_ref, xo_ref, *, final_norm):
    xn = x_ref[0] + mod_ref[0, 0, 5:6, :] * m_ref[0].astype(F32)
    if final_norm:
        ms = jnp.mean(xn * xn, axis=-1, keepdims=True)
        xn = xn * lax.rsqrt(ms + NORM_EPS) * fg_ref[...]
    xo_ref[0] = xn


def _moe_resid(xs, m, mod, final_g, n_ctx, final_norm):
    b, rows, d = xs.shape
    tm = TOKEN_TILE
    nct = n_ctx // tm
    tok = pl.BlockSpec((1, tm, d), lambda i, t: (i, t, 0))
    return pl.pallas_call(
        functools.partial(_moe_resid_kernel, final_norm=final_norm),
        out_shape=jax.ShapeDtypeStruct((b, rows, d), F32),
        grid=(b, rows // tm),
        in_specs=[tok, tok, pl.BlockSpec((1, 1, 6, d), lambda i, t: (i, _seg_index(t, nct), 0, 0)),
                  _const_spec(final_g.shape)],
        out_specs=tok,
        compiler_params=_params(2),
        name="moe_resid",
    )(xs, m, mod, final_g)


def _moe(xs, mod, ng, p, final_g, n_ctx, final_norm):
    h = _moe_route(xs, mod, ng, p, n_ctx)
    m = _moe_experts(h, p)
    return _moe_resid(xs, m, mod, final_g, n_ctx, final_norm)


def _gd_proj_kernel(x_ref, xp_ref, xn_ref, mod_ref, ng_ref, w_ref, conv_ref, wab_ref, alog_ref, dtb_ref,
                    o_ref, gb_ref, h_s, *, tm, n_ctx, n_tot, n_qk_blk, n_conv_blk, n_vh):
    jb = pl.program_id(2)
    pos = pl.program_id(1) * tm

    @pl.when(jb == 0)
    def _():
        sh = mod_ref[0, 0, 0:1, :]
        sc = mod_ref[0, 0, 1:2, :]
        g = ng_ref[...]
        prev_ok = jnp.logical_and(pos != 0, pos != n_ctx)
        next_ok = jnp.logical_and(pos + tm != n_ctx, pos + tm != n_tot)
        h = _norm_mod(x_ref[0], g, sh, sc)
        hp = jnp.where(prev_ok, _norm_mod(xp_ref[0], g, sh, sc), 0.0)
        hn = jnp.where(next_ok, _norm_mod(xn_ref[0], g, sh, sc), 0.0)
        h_s[...] = _bf(jnp.concatenate([hp, h, hn], axis=0))
        ab = _dot_hh(h, wab_ref[...])
        lane = lax.broadcasted_iota(jnp.int32, (1, LANES), 1)
        for z in range(2):
            abz = ab if z == 0 else pltpu.roll(ab, LANES - 2 * n_vh, axis=1)
            gdec = -jnp.exp(alog_ref[z:z + 1, :]) * _softplus(abz + dtb_ref[z:z + 1, :])
            gb_ref[z, 0] = jnp.where(lane < n_vh, gdec, jnp.where(lane < 2 * n_vh, _sigmoid(abz), 0.0))

    ext = jnp.dot(h_s[...], w_ref[...], preferred_element_type=F32)
    n_ext = tm + 2 * HALO

    @pl.when(jb >= n_conv_blk)
    def _():
        o_ref[0] = ext[HALO:HALO + tm]

    @pl.when(jb < n_conv_blk)
    def _():
        pad = (GD_CONV_W - 1) // 2
        is_qk = jb < n_qk_blk
        scale = jnp.where(jb < n_qk_blk // 2, GD_DK ** -0.5, 1.0)
        for hh in range(ext.shape[1] // GD_DK):
            cs = slice(hh * GD_DK, (hh + 1) * GD_DK)
            e = ext[:, cs]
            acc = e[HALO:HALO + tm] * conv_ref[pad:pad + 1, cs]
            for wi in range(GD_CONV_W):
                if wi == pad:
                    continue
                shifted = pltpu.roll(e, (pad - wi) % n_ext, axis=0)
                acc = acc + shifted[HALO:HALO + tm] * conv_ref[wi:wi + 1, cs]
            act = _silu(acc)
            ss = jnp.sum(act * act, axis=-1, keepdims=True)
            o_ref[0, :, cs] = act * jnp.where(is_qk, lax.rsqrt(ss + NORM_EPS) * scale, 1.0)


def _gd_proj(xs, mod, ng, w_in, conv_w, wab, alog, dtb, n_ctx, kd_total, qkv_total, n_vh):
    b, l, d = xs.shape
    n_out = w_in.shape[1]
    tm = TOKEN_TILE
    nb = 1024
    nt = l // tm
    nct = n_ctx // tm
    nj = n_out // nb
    n_conv_blk = qkv_total // nb
    kern = functools.partial(_gd_proj_kernel, tm=tm, n_ctx=n_ctx, n_tot=l, n_qk_blk=2 * kd_total // nb,
                             n_conv_blk=n_conv_blk, n_vh=n_vh)
    return pl.pallas_call(
        kern,
        out_shape=(jax.ShapeDtypeStruct((b, l, n_out), F32), jax.ShapeDtypeStruct((2, b, l, LANES), F32)),
        grid=(b, nt, nj),
        in_specs=_tile_specs(tm, d, nt)
                 + [pl.BlockSpec((1, 1, 6, d), lambda i, t, j: (i, _seg_index(t, nct), 0, 0)),
                    _const_spec(ng.shape),
                    pl.BlockSpec((d, nb), lambda i, t, j: (0, j)),
                    pl.BlockSpec((GD_CONV_W, nb), lambda i, t, j: (0, jnp.minimum(j, n_conv_blk - 1))),
                    _const_spec(wab.shape), _const_spec(alog.shape), _const_spec(dtb.shape)],
        out_specs=(pl.BlockSpec((1, tm, nb), lambda i, t, j: (i, t, j)),
                   pl.BlockSpec((2, 1, tm, LANES), lambda i, t, j: (0, i, t, 0))),
        scratch_shapes=[pltpu.VMEM((tm + 2 * HALO, d), BF16)],
        compiler_params=_params(3),
        name="gd_proj",
    )(xs, xs, xs, mod, ng, w_in, conv_w, wab, alog, dtb)


def _gd_chunk_kernel(q0_ref, k0_ref, v0_ref, gb0_ref, q1_ref, k1_ref, v1_ref, gb1_ref,
                     o0_ref, o1_ref, s_ref, *, n_vh, heads_per_group):
    c = q0_ref.shape[1]
    n = heads_per_group
    rep = n_vh // (q0_ref.shape[2] // GD_DK)
    n_groups = n_vh // n
    gw = n * c
    kh_per_group = n // rep
    assert 2 * c == LANES and n_vh % 2 == 0

    @pl.when(pl.program_id(1) == 0)
    def _():
        s_ref[...] = jnp.zeros_like(s_ref)

    bd = _block_mask(gw, gw, c, c)
    bd_k = _block_mask(gw, kh_per_group * GD_DK, c, GD_DK, rdiv=rep)
    bd_v = _block_mask(gw, n * GD_DV, c, GD_DV)
    low_half = lax.broadcasted_iota(jnp.int32, (1, LANES), 1) < c

    def bdiag_v(x):
        return jnp.where(bd_v, _tile_rows(x, n), 0.0)

    def spread_dv(x, off):
        return jnp.concatenate([jnp.broadcast_to(x[:, off + h:off + h + 1], (c, GD_DV)) for h in range(n_vh)], axis=1)

    def spread_c(x, off):
        cols = [jnp.broadcast_to(x[:, off + h:off + h + 1], (c, LANES)) for h in range(n_vh)]
        return jnp.concatenate([jnp.where(low_half, cols[h], cols[h + 1]) for h in range(0, n_vh, 2)], axis=1)

    refs = ((q0_ref, k0_ref, v0_ref, gb0_ref, o0_ref), (q1_ref, k1_ref, v1_ref, gb1_ref, o1_ref))
    n_units = 2 * q0_ref.shape[0]
    prep = []
    for unit in range(n_units):
        bi, z = divmod(unit, 2)
        q_ref, k_ref, v_ref, gb_ref, _ = refs[z]
        gb = gb_ref[0, bi]
        gc = _cumsum_rows(gb, z == 1)
        gtot = jnp.sum(gb, axis=0, keepdims=True)
        incl_all, _, _ = _order_masks(z, c, n_vh)
        gt64 = spread_c(gc, 0)
        sq = jnp.concatenate([gc, jnp.zeros((LANES - c, LANES), F32)], axis=0).T
        sq_hi = pltpu.roll(sq, c, axis=1)
        gs64 = jnp.concatenate([jnp.where(low_half, sq[h:h + 1, :], sq_hi[h + 1:h + 2, :])
                                for h in range(0, n_vh, 2)], axis=1)
        prep.append({
            "gam": jnp.where(incl_all, jnp.exp(jnp.where(incl_all, gt64 - gs64, 0.0)), 0.0),
            "bt64": spread_c(gb, n_vh),
            "e_g": spread_dv(jnp.exp(gc), 0),
            "e_end": spread_dv(jnp.exp(gtot - gc), 0),
            "beta": spread_dv(gb, n_vh),
            "gl": jnp.exp(gtot),
            "q": q_ref[bi], "k": k_ref[bi], "v": v_ref[bi], "masks": _order_masks(z, c, n)})

    chains = [(unit, gi) for unit in range(n_units) for gi in range(n_groups)]

    def ksl(gi):
        return slice(gi * kh_per_group * GD_DK, (gi + 1) * kh_per_group * GD_DK)

    def csl(gi):
        return slice(gi * gw, (gi + 1) * gw)

    def vsl(gi):
        return slice(gi * n * GD_DV, (gi + 1) * n * GD_DV)

    def per_vhead(x):
        return jnp.concatenate([x[:, (hh // rep) * GD_DK:(hh // rep + 1) * GD_DK] for hh in range(n)], axis=1)

    kg_ = [prep[z]["k"][:, ksl(gi)] for z, gi in chains]
    qg_ = [prep[z]["q"][:, ksl(gi)] for z, gi in chains]
    qkk = [_dot_nt(jnp.concatenate([kx, qx], axis=0), jnp.where(bd_k, _tile_rows(kx, n), 0.0))
           for kx, qx in zip(kg_, qg_)]
    a_mat = [jnp.where(prep[z]["masks"][1], x[:c] * prep[z]["gam"][:, csl(gi)] * prep[z]["bt64"][:, csl(gi)], 0.0)
             for x, (z, gi) in zip(qkk, chains)]
    aqk = [jnp.where(prep[z]["masks"][0], x[c:] * prep[z]["gam"][:, csl(gi)], 0.0) for x, (z, gi) in zip(qkk, chains)]
    t_inv = _tri_inv([-a for a in a_mat], bd, [prep[z]["masks"][2] for z, _ in chains],
                     [z % 2 == 1 for z, _ in chains])

    k2 = [per_vhead(x) for x in kg_]
    q2 = [per_vhead(x) for x in qg_]
    bg = [prep[z]["beta"][:, vsl(gi)] for z, gi in chains]
    eg = [prep[z]["e_g"][:, vsl(gi)] for z, gi in chains]
    u = [_dot(t, bdiag_v(prep[z]["v"][:, vsl(gi)] * b_)) for t, b_, (z, gi) in zip(t_inv, bg, chains)]
    w = [_dot(t, bdiag_v(kx * b_ * e_)) for t, kx, b_, e_ in zip(t_inv, k2, bg, eg)]
    qe = [qx * e_ for qx, e_ in zip(q2, eg)]
    ke = [kx * prep[z]["e_end"][:, vsl(gi)] for kx, (z, gi) in zip(k2, chains)]

    heads = [(ci, hh) for ci in range(len(chains)) for hh in range(n)]

    def hs(hh):
        return slice(hh * GD_DV, (hh + 1) * GD_DV)

    def state_index(ci, hh):
        z, gi = chains[ci]
        return z, gi * n + hh

    ws = {(ci, hh): _dot(jnp.concatenate([w[ci][:, hs(hh)], qe[ci][:, hs(hh)]], axis=0), s_ref[state_index(ci, hh)])
          for ci, hh in heads}
    vn = [jnp.concatenate([u[ci][:, hs(hh)] - ws[ci, hh][:c] for hh in range(n)], axis=1) for ci in range(len(chains))]
    for ci, (z, gi) in enumerate(chains):
        pre = jnp.concatenate([ws[ci, hh][c:] for hh in range(n)], axis=1)
        refs[z % 2][4][z // 2, :, vsl(gi)] = pre + _dot(aqk[ci], bdiag_v(vn[ci]))
    for ci, hh in heads:
        z, hv = state_index(ci, hh)
        gl = jnp.broadcast_to(prep[z]["gl"][:, hv:hv + 1], (1, GD_DV))
        s_ref[z, hv] = s_ref[z, hv] * gl + _dot_tn(ke[ci][:, hs(hh)], vn[ci][:, hs(hh)])


def _gd_chunk(proj, gb, n_ctx, kd_total, vd_total):
    b, l, _ = proj.shape
    c = CHUNK
    nc = l // c
    ncc = n_ctx // c
    n_vh = vd_total // GD_DV
    bpb = SCAN_BATCH if b % SCAN_BATCH == 0 else 1

    def col(z, width, blk):
        return pl.BlockSpec((bpb, c, width), lambda i, j: (i, _chunk_index(z, j, ncc, nc), blk))

    def perdir(z):
        return pl.BlockSpec((1, bpb, c, LANES), lambda i, j: (z, i, _chunk_index(z, j, ncc, nc), 0))

    def dir_specs(z):
        return [col(z, kd_total, 0), col(z, kd_total, 1), col(z, vd_total, 2 * kd_total // vd_total), perdir(z)]

    sd = jax.ShapeDtypeStruct((b, l, vd_total), F32)
    return pl.pallas_call(
        functools.partial(_gd_chunk_kernel, n_vh=n_vh, heads_per_group=4),
        out_shape=(sd, sd),
        grid=(b // bpb, nc),
        in_specs=dir_specs(0) + dir_specs(1),
        out_specs=(col(0, vd_total, 0), col(1, vd_total, 0)),
        scratch_shapes=[pltpu.VMEM((2 * bpb, n_vh, GD_DK, GD_DV), F32)],
        compiler_params=_params(2),
        name="gd_chunk",
    )(proj, proj, proj, gb, proj, proj, proj, gb)


def _gd_out_kernel(x_ref, o0_ref, o1_ref, z_ref, mod_ref, ngd_ref, wo_ref, xo_ref):
    o = o0_ref[0] + o1_ref[0]
    z = z_ref[0]
    parts = []
    for hh in range(o.shape[1] // GD_DV):
        seg = o[:, hh * GD_DV:(hh + 1) * GD_DV]
        ms = jnp.mean(seg * seg, axis=-1, keepdims=True)
        parts.append(seg * lax.rsqrt(ms + NORM_EPS) * ngd_ref[...])
    on = jnp.concatenate(parts, axis=1) * _silu(z)
    xo_ref[0] = x_ref[0] + mod_ref[0, 0, 2:3, :] * _dot(on, wo_ref[...])


def _gd_out(xs, o, proj, mod, ngd, w_o, n_ctx, z_blk):
    b, l, d = xs.shape
    vd = o[0].shape[-1]
    tm = TOKEN_TILE
    nct = n_ctx // tm
    nt = (l - n_ctx) // tm
    return pl.pallas_call(
        _gd_out_kernel,
        out_shape=jax.ShapeDtypeStruct((b, l - n_ctx, d), F32),
        grid=(b, nt),
        in_specs=[pl.BlockSpec((1, tm, d), lambda i, t: (i, t + nct, 0)),
                  pl.BlockSpec((1, tm, vd), lambda i, t: (i, t + nct, 0)),
                  pl.BlockSpec((1, tm, vd), lambda i, t: (i, t + nct, 0)),
                  pl.BlockSpec((1, tm, vd), lambda i, t: (i, t + nct, z_blk)),
                  pl.BlockSpec((1, 1, 6, d), lambda i, t: (i, 1, 0, 0)),
                  _const_spec(ngd.shape), _const_spec(w_o.shape)],
        out_specs=pl.BlockSpec((1, tm, d), lambda i, t: (i, t, 0)),
        compiler_params=_params(2),
        name="gd_out",
    )(xs, o[0], o[1], proj, mod, ngd, w_o)


def _pos_embed_2d(rows, d):
    quarter = d // 4
    omega = 1.0 / (POS_BASE ** (jnp.arange(quarter, dtype=F32) / quarter))

    def axis_emb(n):
        ang = jnp.arange(n, dtype=F32)[:, None] * omega[None, :]
        return jnp.concatenate([jnp.sin(ang), jnp.cos(ang)], axis=-1)

    e_row = jnp.broadcast_to(axis_emb(rows)[:, None, :], (rows, GRID_W, d // 2))
    e_col = jnp.broadcast_to(axis_emb(GRID_W)[None, :, :], (rows, GRID_W, d // 2))
    return jnp.concatenate([e_row, e_col], axis=-1).reshape(rows * GRID_W, d)


def _head_indicator(d, head):
    ch = jnp.arange(d)[:, None] // head
    ind = (ch == jnp.arange(LANES)[None, :]).astype(BF16)
    return ind, ind.T


def _block_diag2(m):
    z = jnp.zeros_like(m[0])
    return jnp.concatenate([jnp.concatenate([m[0], z], axis=1), jnp.concatenate([z, m[1]], axis=1)], axis=0)


def _moe_params(i, moe_w_rg, moe_b_rg, moe_w_re, moe_b_re, moe_w_gate, moe_w_up, moe_w_down):
    d = moe_w_rg.shape[1]
    ne = MOE_GROUPS * MOE_EXPERTS
    pad = LANES - ne - MOE_GROUPS
    w_r = jnp.concatenate([moe_w_re[i], moe_w_rg[i], jnp.zeros((d, pad), F32)], axis=1)
    b_r = jnp.concatenate([moe_b_re[i], moe_b_rg[i], jnp.zeros((pad,), F32)])[None, :]
    ff = moe_w_gate.shape[-1]
    return {"w_r": w_r, "b_r": b_r,
            "w_gu": jnp.concatenate([_bf(moe_w_gate[i]), _bf(moe_w_up[i])], axis=-1).reshape(ne, d, 2 * ff),
            "w_down": _bf(moe_w_down[i]).reshape(ne, ff, d)}


def kernel(x, c, ctx, c_ctx, ada_w, ada_b, norm1_g, norm2_g, rw_mix, rw_w_rkv, rw_w0, rw_w1, rw_w2, rw_a0, rw_a1, rw_a2, rw_g1, rw_g2, rw_k_k, rw_k_a, rw_r_k, rw_ln_g, rw_ln_b, rw_w_o, gd_w_in, gd_conv, gd_w_ab, gd_a_log, gd_dt_bias, gd_norm_g, gd_w_o, moe_w_rg, moe_b_rg, moe_w_re, moe_b_re, moe_w_gate, moe_w_up, moe_w_down, final_g):
    bsz, n_lat, d = x.shape
    n_ctx = ctx.shape[1]
    assert n_ctx % TOKEN_TILE == 0 and n_lat % TOKEN_TILE == 0 and n_ctx % CHUNK == 0
    assert ada_w.shape[0] == 2 and d % (4 * RW_HEAD) == 0

    rows = -(-(bsz + 1) // HALO) * HALO
    cs = jnp.zeros((rows, d), F32).at[:bsz].set(c).at[bsz].set(c_ctx)
    mod_all = _ada(cs, ada_w, ada_b)

    def mod_of(i):
        lat = mod_all[i, :bsz].reshape(bsz, 1, 6, d)
        cx = jnp.broadcast_to(mod_all[i, bsz].reshape(1, 1, 6, d), (bsz, 1, 6, d))
        return jnp.concatenate([cx, lat], axis=1)

    xs = jnp.concatenate([ctx, x], axis=1)
    pos = jnp.concatenate([jnp.zeros((n_ctx, d), F32), _pos_embed_2d(n_lat // GRID_W, d)], axis=0)
    gsum, gbc = _head_indicator(d, RW_HEAD)

    mod0 = mod_of(0)
    ng1 = norm1_g[0][None, :]
    rw = {"mix": rw_mix[0], "w_rkv": _bf(rw_w_rkv[0]), "w0": rw_w0[0],
          "w1": _bf(jnp.concatenate([rw_w1[0, 0], rw_w1[0, 1]], axis=1)), "w2": _bf(_block_diag2(rw_w2[0])),
          "a0": rw_a0[0],
          "a1": _bf(jnp.concatenate([rw_a1[0, 0], rw_a1[0, 1]], axis=1)), "a2": _bf(_block_diag2(rw_a2[0])),
          "g1": _bf(rw_g1[0]), "g2": _bf(rw_g2[0]), "k_k": rw_k_k[0][None, :], "k_a": rw_k_a[0][None, :],
          "r_k": rw_r_k[0].reshape(1, d), "ln_g": rw_ln_g[0][None, :], "ln_b": rw_ln_b[0][None, :],
          "w_o": _bf(rw_w_o[0])}
    r, v, kk, gate, lw, kd, ar = _rw_feat(xs, pos, mod0, ng1, rw, gsum, gbc, n_ctx)
    y = _rw_scan(r, v, kk, lw, kd, ar, n_ctx)
    xs = _rw_out(xs, pos, y, r, v, kd, gate, mod0, rw, gsum, gbc, n_ctx)
    moe_args = (moe_w_rg, moe_b_rg, moe_w_re, moe_b_re, moe_w_gate, moe_w_up, moe_w_down)
    fg = final_g[None, :]
    xs = _moe(xs, mod0, norm2_g[0][None, :], _moe_params(0, *moe_args), fg, n_ctx, False)

    mod1 = mod_of(1)
    n_vh = gd_a_log.shape[-1]
    vd_total = n_vh * GD_DV
    kd_total = (gd_w_in.shape[-1] - 2 * vd_total) // 2
    qkv_total = 2 * kd_total + vd_total
    ng1 = norm1_g[1][None, :]
    wab = jnp.concatenate([gd_w_ab[0, 0], gd_w_ab[0, 1], jnp.zeros((d, LANES - 4 * n_vh), F32)], axis=1)
    lane_pad = jnp.zeros((2, LANES - n_vh), F32)
    alog = jnp.concatenate([gd_a_log[0], lane_pad], axis=1)
    dtb = jnp.concatenate([gd_dt_bias[0], lane_pad], axis=1)
    proj, gb = _gd_proj(xs, mod1, ng1, _bf(gd_w_in[0]), gd_conv[0], wab, alog, dtb, n_ctx, kd_total, qkv_total, n_vh)
    o = _gd_chunk(proj, gb, n_ctx, kd_total, vd_total)
    ngd = gd_norm_g[0][None, :]
    x_lat = _gd_out(xs, o, proj, mod1, ngd, _bf(gd_w_o[0]), n_ctx, qkv_total // vd_total)
    return _moe(x_lat, mod1, norm2_g[1][None, :], _moe_params(1, *moe_args), fg, 0, True)
```

```python
import functools
import math

import jax
import jax.numpy as jnp
from jax import lax
from jax.experimental import pallas as pl
from jax.experimental.pallas import tpu as pltpu

F32 = jnp.float32
BF16 = jnp.bfloat16

NORM_EPS = 1e-6
RW_GN_EPS = 64e-5
POS_BASE = 10000.0
GRID_W = 64
RW_HEAD = 64
RW_LORA = 64
GD_DK = 128
GD_DV = 128
GD_CONV_W = 5
MOE_GROUPS = 4
MOE_EXPERTS = 8
CHUNK = 64
LANES = 128
HALO = 8
TOKEN_TILE = 256
VMEM_LIMIT = 56 * 1024 * 1024
NEG_BIG = -1e30


def _bf(x):
    return x.astype(BF16)


def _dot(a, b):
    return jnp.dot(_bf(a), _bf(b), preferred_element_type=F32)


def _dot_nt(a, b):
    return lax.dot_general(_bf(a), _bf(b), (((1,), (1,)), ((), ())), preferred_element_type=F32)


def _dot_tn(a, b):
    return lax.dot_general(_bf(a), _bf(b), (((0,), (0,)), ((), ())), preferred_element_type=F32)


def _split2(x):
    hi = x.astype(BF16)
    lo = (x - hi.astype(F32)).astype(BF16)
    return hi, lo


def _split3(x):
    hi = x.astype(BF16)
    r1 = x - hi.astype(F32)
    mid = r1.astype(BF16)
    lo = (r1 - mid.astype(F32)).astype(BF16)
    return hi, mid, lo


def _dot_x2(a, b_exact):
    hi, lo = _split2(a)
    b = _bf(b_exact)
    return (jnp.dot(hi, b, preferred_element_type=F32) + jnp.dot(lo, b, preferred_element_type=F32))


def _dot_x3(a, b_exact):
    hi, mid, lo = _split3(a)
    b = _bf(b_exact)
    return (jnp.dot(hi, b, preferred_element_type=F32) + jnp.dot(mid, b, preferred_element_type=F32)
            + jnp.dot(lo, b, preferred_element_type=F32))


def _dot_ex3(a_exact, b):
    hi, mid, lo = _split3(b)
    a = _bf(a_exact)
    return (jnp.dot(a, hi, preferred_element_type=F32) + jnp.dot(a, mid, preferred_element_type=F32)
            + jnp.dot(a, lo, preferred_element_type=F32))


def _dot_hh(a, b):
    ah, al = _split2(a)
    bh, bl = _split2(b)
    return (jnp.dot(ah, bh, preferred_element_type=F32) + jnp.dot(al, bh, preferred_element_type=F32)
            + jnp.dot(ah, bl, preferred_element_type=F32))


def _sigmoid(x):
    return 1.0 / (1.0 + jnp.exp(-x))


def _silu(x):
    return x * _sigmoid(x)


def _softplus(x):
    return jnp.maximum(x, 0.0) + jnp.log(1.0 + jnp.exp(-jnp.abs(x)))


def _norm_mod(x, g, shift, scale):
    ms = jnp.mean(x * x, axis=-1, keepdims=True)
    return (x * lax.rsqrt(ms + NORM_EPS) * g) * (1.0 + scale) + shift


def _tile_rows(x, n):
    return jnp.concatenate([x] * n, axis=0)


def _tile_lanes(x, n):
    return jnp.concatenate([x] * n, axis=1)


def _order_masks(direction, c, n=1):
    row = lax.broadcasted_iota(jnp.int32, (c, n * c), 0)
    col = lax.broadcasted_iota(jnp.int32, (c, n * c), 1) % c
    diff = (col - row) * (1 - 2 * direction)
    return diff <= 0, diff < 0, jnp.where(diff == 0, 1.0, 0.0)


def _cumsum_rows(x, reverse):
    n = x.shape[0]
    row = lax.broadcasted_iota(jnp.int32, (n, 1), 0)
    s = 1
    while s < n:
        if reverse:
            shifted = jnp.where(row < n - s, pltpu.roll(x, n - s, axis=0), 0.0)
        else:
            shifted = jnp.where(row >= s, pltpu.roll(x, s, axis=0), 0.0)
        x = x + shifted
        s *= 2
    return x


def _block_mask(rows, cols, rblk, cblk, rdiv=1):
    r = lax.broadcasted_iota(jnp.int32, (rows, cols), 0) // rblk
    c = lax.broadcasted_iota(jnp.int32, (rows, cols), 1) // cblk
    return (r // rdiv) == c if rdiv != 1 else r == c


TRI_BASE = 4
SCAN_BATCH = 4


def _tri_inv(a_list, bd, eye_list, rev_list):
    c = a_list[0].shape[0]
    n = a_list[0].shape[1] // c
    row = lax.broadcasted_iota(jnp.int32, a_list[0].shape, 0)
    col = lax.broadcasted_iota(jnp.int32, a_list[0].shape, 1) % c

    def same_block(m):
        return (row // m) == (col // m)

    def mm(x, y):
        return jnp.dot(_bf(x), _bf(jnp.where(bd, _tile_rows(y, n), 0.0)), preferred_element_type=F32)

    base = same_block(TRI_BASE)
    a0 = [jnp.where(base, a, 0.0) for a in a_list]
    x = [e + a for e, a in zip(eye_list, a0)]
    p = [mm(a, a) for a in a0]
    base_levels = int(math.log2(TRI_BASE)) - 1
    for lvl in range(base_levels):
        x = [xi + mm(pi, xi) for xi, pi in zip(x, p)]
        if lvl + 1 < base_levels:
            p = [mm(pi, pi) for pi in p]
    m = TRI_BASE
    while m < c:
        off = jnp.logical_and(same_block(2 * m), jnp.logical_not(same_block(m)))
        a_off = [jnp.where(off, a, 0.0) for a in a_list]
        if m % HALO:
            t = [mm(ao, xi) for ao, xi in zip(a_off, x)]
            x = [xi + mm(xi, ti) for xi, ti in zip(x, t)]
        else:
            zeros = jnp.zeros((m, a_list[0].shape[1]), F32)

            def pick(v, rev):
                return jnp.concatenate([v[(2 * i + (0 if rev else 1)) * m:(2 * i + (0 if rev else 1) + 1) * m]
                                        for i in range(c // (2 * m))], axis=0)

            def place(v, rev):
                parts = []
                for i in range(c // (2 * m)):
                    blk = v[i * m:(i + 1) * m]
                    parts += [blk, zeros] if rev else [zeros, blk]
                return jnp.concatenate(parts, axis=0)

            t = [place(mm(pick(ao, rev), xi), rev) for ao, xi, rev in zip(a_off, x, rev_list)]
            x = [xi + place(mm(pick(xi, rev), ti), rev) for xi, ti, rev in zip(x, t, rev_list)]
        m *= 2
    return x


def _chunk_index(direction, j, n_ctx_chunks, n_chunks):
    bwd = jnp.where(j < n_ctx_chunks, n_ctx_chunks - 1 - j, n_chunks + n_ctx_chunks - 1 - j)
    return jnp.where(direction == 0, j, bwd)


def _const_spec(shape):
    nd = len(shape)
    return pl.BlockSpec(shape, lambda *_: (0,) * nd, pipeline_mode=pl.Buffered(1))


def _params(n_axes):
    return pltpu.CompilerParams(dimension_semantics=("arbitrary",) * n_axes, vmem_limit_bytes=VMEM_LIMIT)


def _ada_kernel(cs_ref, w_ref, b_ref, o_ref):
    s = _silu(cs_ref[...])
    o_ref[0] = _dot_hh(s, w_ref[0]) + b_ref[0]


def _ada(cs, ada_w, ada_b):
    n_layers, d, n6 = ada_w.shape
    rows = cs.shape[0]
    tn = 1536
    return pl.pallas_call(
        _ada_kernel,
        out_shape=jax.ShapeDtypeStruct((n_layers, rows, n6), F32),
        grid=(n_layers, n6 // tn),
        in_specs=[pl.BlockSpec((rows, d), lambda i, j: (0, 0)),
                  pl.BlockSpec((1, d, tn), lambda i, j: (i, 0, j)),
                  pl.BlockSpec((1, 1, tn), lambda i, j: (i, 0, j))],
        out_specs=pl.BlockSpec((1, rows, tn), lambda i, j: (i, 0, j)),
        compiler_params=_params(2),
        name="ada",
    )(cs, ada_w, ada_b.reshape(n_layers, 1, n6))


def _shifted_neighbours(h, h_prev, h_next, pos, tm, n_ctx, n_tot):
    prev_ok = jnp.logical_and(pos != 0, pos != n_ctx)
    next_ok = jnp.logical_and(pos + tm != n_ctx, pos + tm != n_tot)
    h_prev = jnp.where(prev_ok, h_prev, 0.0)
    h_next = jnp.where(next_ok, h_next, 0.0)
    row = lax.broadcasted_iota(jnp.int32, (tm, 1), 0)
    prev = jnp.where(row == 0, h_prev, pltpu.roll(h, 1, axis=0))
    nxt = jnp.where(row == tm - 1, h_next, pltpu.roll(h, tm - 1, axis=0))
    return prev, nxt


def _tile_specs(tm, d, n_tiles, lead=()):
    nl = len(lead)
    hb = tm // HALO
    last = n_tiles * hb - 1

    def cur(*g):
        return (g[nl], g[nl + 1], 0)

    def prv(*g):
        return (g[nl], jnp.maximum(g[nl + 1] * hb - 1, 0), 0)

    def nxt(*g):
        return (g[nl], jnp.minimum((g[nl + 1] + 1) * hb, last), 0)

    return [pl.BlockSpec((1, tm, d), cur), pl.BlockSpec((1, HALO, d), prv), pl.BlockSpec((1, HALO, d), nxt)]


def _rw_feat_kernel(x_ref, xp_ref, xn_ref, pos_ref, posp_ref, posn_ref, mod_ref, ng_ref, mix_ref,
                    wrkv_ref, w0_ref, w1_ref, w2_ref, a0_ref, a1_ref, a2_ref, g1_ref, g2_ref,
                    kkp_ref, kap_ref, gsum_ref, gbc_ref,
                    r_o, v_o, kk_o, gate_o, lw_o, kd_o, a_o, *, tm, n_ctx, n_tot):
    d = x_ref.shape[-1]
    pos = pl.program_id(1) * tm
    sh = mod_ref[0, 0, 0:1, :]
    sc = mod_ref[0, 0, 1:2, :]
    g = ng_ref[...]
    h = _norm_mod(x_ref[0] + pos_ref[...], g, sh, sc)
    hp = _norm_mod(xp_ref[0, HALO - 1:HALO, :] + posp_ref[HALO - 1:HALO, :], g, sh, sc)
    hn = _norm_mod(xn_ref[0, 0:1, :] + posn_ref[0:1, :], g, sh, sc)
    prev, nxt = _shifted_neighbours(h, hp, hn, pos, tm, n_ctx, n_tot)
    xx = 0.5 * (prev + nxt) - h
    mix = mix_ref[...]
    xr = h + xx * mix[0:1]
    xw = h + xx * mix[1:2]
    xk = h + xx * mix[2:3]
    xv = h + xx * mix[3:4]
    xa = h + xx * mix[4:5]
    xg = h + xx * mix[5:6]

    r_o[0] = _bf(_dot(xr, wrkv_ref[0]))
    v_o[0] = _bf(_dot(xv, wrkv_ref[2]))
    gate_o[0] = _bf(_dot(_sigmoid(_dot(xg, g1_ref[...])), g2_ref[...]))
    k = _dot(xk, wrkv_ref[1])

    kq = k * kkp_ref[...]
    ssq = _dot_x2(kq * kq, gsum_ref[...])
    kk_o[0] = _bf(kq * _dot_x2(lax.rsqrt(ssq + NORM_EPS), gbc_ref[...]))

    wl = _dot(jnp.tanh(_dot(xw, w1_ref[...])), w2_ref[...])
    al = _dot(_dot(xa, a1_ref[...]), a2_ref[...])
    ka = kap_ref[...]
    for z in range(2):
        zw = w0_ref[z:z + 1, :] + wl[:, z * d:(z + 1) * d]
        lw_o[z, 0] = (-math.exp(-0.5)) * _sigmoid(zw)
        rate = _sigmoid(a0_ref[z:z + 1, :] + al[:, z * d:(z + 1) * d])
        a_o[z, 0] = _bf(rate)
        kd_o[z, 0] = _bf(k * (1.0 + (rate - 1.0) * ka))


def _rw_feat(xs, pos, mod, ng, p, gsum, gbc, n_ctx):
    b, l, d = xs.shape
    tm = TOKEN_TILE
    nt = l // tm
    nct = n_ctx // tm
    hb = tm // HALO
    kern = functools.partial(_rw_feat_kernel, tm=tm, n_ctx=n_ctx, n_tot=l)
    tok = pl.BlockSpec((1, tm, d), lambda i, t: (i, t, 0))
    tok2 = pl.BlockSpec((2, 1, tm, d), lambda i, t: (0, i, t, 0))
    pos_specs = [pl.BlockSpec((tm, d), lambda i, t: (t, 0)),
                 pl.BlockSpec((HALO, d), lambda i, t: (jnp.maximum(t * hb - 1, 0), 0)),
                 pl.BlockSpec((HALO, d), lambda i, t: (jnp.minimum((t + 1) * hb, nt * hb - 1), 0))]
    in_specs = (_tile_specs(tm, d, nt) + pos_specs
                + [pl.BlockSpec((1, 1, 6, d), lambda i, t: (i, _seg_index(t, nct), 0, 0))]
                + [_const_spec(a.shape) for a in (ng, p["mix"], p["w_rkv"], p["w0"], p["w1"], p["w2"], p["a0"],
                                                  p["a1"], p["a2"], p["g1"], p["g2"], p["k_k"], p["k_a"], gsum, gbc)])
    sd = jax.ShapeDtypeStruct((b, l, d), BF16)
    sd2 = jax.ShapeDtypeStruct((2, b, l, d), BF16)
    return pl.pallas_call(
        kern,
        out_shape=(sd, sd, sd, sd, jax.ShapeDtypeStruct((2, b, l, d), F32), sd2, sd2),
        grid=(b, nt),
        in_specs=in_specs,
        out_specs=(tok, tok, tok, tok, tok2, tok2, tok2),
        compiler_params=_params(2),
        name="rw_feat",
    )(xs, xs, xs, pos, pos, pos, mod, ng, p["mix"], p["w_rkv"], p["w0"], p["w1"], p["w2"], p["a0"], p["a1"], p["a2"],
      p["g1"], p["g2"], p["k_k"], p["k_a"], gsum, gbc)


def _rw_scan_kernel(r0_ref, v0_ref, kk0_ref, lw0_ref, kd0_ref, a0_ref,
                    r1_ref, v1_ref, kk1_ref, lw1_ref, kd1_ref, a1_ref,
                    y0_ref, y1_ref, s_ref, *, heads_per_group):
    c = r0_ref.shape[1]
    d = r0_ref.shape[2]
    n = heads_per_group
    gw = n * RW_HEAD
    n_groups = d // gw
    nt = (((1,), (1,)), ((), ()))

    @pl.when(pl.program_id(1) == 0)
    def _():
        s_ref[...] = jnp.zeros_like(s_ref)

    bd = _block_mask(n * c, gw, c, RW_HEAD)

    def bdiag(x):
        return jnp.where(bd, _tile_rows(x, n), 0.0)

    refs = ((r0_ref, v0_ref, kk0_ref, lw0_ref, kd0_ref, a0_ref, y0_ref),
            (r1_ref, v1_ref, kk1_ref, lw1_ref, kd1_ref, a1_ref, y1_ref))
    masks = [_order_masks(z, c, n) for z in range(2)]
    prep = {}
    for bi in range(r0_ref.shape[0]):
        for z, (r_ref, v_ref, kk_ref, lw_ref, kd_ref, a_ref, _) in enumerate(refs):
            lw = lw_ref[0, bi]
            lam = _cumsum_rows(lw, z == 1)
            tot = jnp.sum(lw, axis=0, keepdims=True)
            w_inv = jnp.exp(-lam)
            w_end = jnp.exp(tot - lam)
            kk = kk_ref[bi].astype(F32)
            kd = kd_ref[0, bi].astype(F32)
            bb = kk * a_ref[0, bi].astype(F32)
            prep[bi, z] = {"a_t": -(kk * jnp.exp(lam - lw)), "r_t": r_ref[bi].astype(F32) * jnp.exp(lam),
                           "b_t": bb * w_inv, "k_t": kd * w_inv, "b_e": bb * w_end, "k_e": kd * w_end,
                           "v": v_ref[bi].astype(F32),
                           "w_tot": jnp.exp(tot)}

    chains = [(bi, z, gi) for bi in range(r0_ref.shape[0]) for z in range(2) for gi in range(n_groups)]

    def sl(gi):
        return slice(gi * gw, (gi + 1) * gw)

    def part(name, ch):
        bi, z, gi = ch
        return prep[bi, z][name][:, sl(gi)]

    lhs = [_bf(jnp.concatenate([part("a_t", ch), part("r_t", ch)], axis=0)) for ch in chains]
    ab = [lax.dot_general(lh, _bf(bdiag(part("b_t", ch))), nt, preferred_element_type=F32)
          for lh, ch in zip(lhs, chains)]
    ak = [lax.dot_general(lh, _bf(bdiag(part("k_t", ch))), nt, preferred_element_type=F32)
          for lh, ch in zip(lhs, chains)]
    a_ab = [jnp.where(masks[z][1], x[:c], 0.0) for x, (_, z, _) in zip(ab, chains)]
    a_rb = [jnp.where(masks[z][0], x[c:], 0.0) for x, (_, z, _) in zip(ab, chains)]
    a_kk = [jnp.concatenate([jnp.where(masks[z][1], x[:c], 0.0), jnp.where(masks[z][0], x[c:], 0.0)], axis=0)
            for x, (_, z, _) in zip(ak, chains)]
    st = [s_ref[ch] for ch in chains]
    pre = [lax.dot_general(lh, _bf(s), nt, preferred_element_type=F32) for lh, s in zip(lhs, st)]
    vg = [part("v", ch) for ch in chains]
    av = [_dot(x, bdiag(v)) for x, v in zip(a_kk, vg)]
    t_inv = _tri_inv(a_ab, bd, [masks[z][2] for _, z, _ in chains], [z == 1 for _, z, _ in chains])
    u = [_dot(t, bdiag(p[:c] + q[:c])) for t, p, q in zip(t_inv, pre, av)]
    yv = [p[c:] + q[c:] + _dot(x, bdiag(ui)) for p, q, x, ui in zip(pre, av, a_rb, u)]
    upd = [_dot_tn(jnp.concatenate([ui, v], axis=0), jnp.concatenate([part("b_e", ch), part("k_e", ch)], axis=0))
           for ui, v, ch in zip(u, vg, chains)]
    for ch, y_val, s, up in zip(chains, yv, st, upd):
        bi, z, gi = ch
        refs[z][6][bi, :, sl(gi)] = y_val
        s_ref[ch] = s * part("w_tot", ch) + jnp.where(bd, up, 0.0)


def _rw_scan(r, v, kk, lw, kd, a, n_ctx):
    b, l, d = r.shape
    c = CHUNK
    nc = l // c
    ncc = n_ctx // c
    hpg = 4
    gw = hpg * RW_HEAD
    bpb = SCAN_BATCH if b % SCAN_BATCH == 0 else 1

    def shared(z):
        return pl.BlockSpec((bpb, c, d), lambda i, j: (i, _chunk_index(z, j, ncc, nc), 0))

    def perdir(z):
        return pl.BlockSpec((1, bpb, c, d), lambda i, j: (z, i, _chunk_index(z, j, ncc, nc), 0))

    sd = jax.ShapeDtypeStruct((b, l, d), F32)
    return pl.pallas_call(
        functools.partial(_rw_scan_kernel, heads_per_group=hpg),
        out_shape=(sd, sd),
        grid=(b // bpb, nc),
        in_specs=[shared(0)] * 3 + [perdir(0)] * 3 + [shared(1)] * 3 + [perdir(1)] * 3,
        out_specs=(shared(0), shared(1)),
        scratch_shapes=[pltpu.VMEM((bpb, 2, d // gw, gw, gw), F32)],
        compiler_params=_params(2),
        name="rw_scan",
    )(r, v, kk, lw, kd, a, r, v, kk, lw, kd, a)


def _rw_out_kernel(x_ref, pos_ref, y0_ref, y1_ref, r_ref, v_ref, kd_ref, gate_ref, mod_ref, rk_ref, lng_ref, lnb_ref,
                   wo_ref, gsum_ref, gbc_ref, xo_ref):
    inv_n = 1.0 / RW_HEAD
    gsum = gsum_ref[...]
    gbc = gbc_ref[...]
    y = y0_ref[0] + y1_ref[0]
    mu = _dot_x2(_dot_x2(y, gsum) * inv_n, gbc)
    yc = y - mu
    var = _dot_x2(yc * yc, gsum) * inv_n
    yn = yc * _dot_x2(lax.rsqrt(var + RW_GN_EPS), gbc) * lng_ref[...] + lnb_ref[...]
    k_bonus = 0.5 * (kd_ref[0, 0].astype(F32) + kd_ref[1, 0].astype(F32))
    bonus = _dot_x2(_dot_x2(r_ref[0].astype(F32) * k_bonus * rk_ref[...], gsum), gbc) * v_ref[0].astype(F32)
    out = (yn + bonus) * gate_ref[0].astype(F32)
    o = _dot(out, wo_ref[...])
    xo_ref[0] = x_ref[0] + pos_ref[...] + mod_ref[0, 0, 2:3, :] * o


def _rw_out(xs, pos, y, r, v, kd, gate, mod, p, gsum, gbc, n_ctx):
    b, l, d = xs.shape
    tm = TOKEN_TILE
    nt = l // tm
    nct = n_ctx // tm
    tok = pl.BlockSpec((1, tm, d), lambda i, t: (i, t, 0))
    tok2 = pl.BlockSpec((2, 1, tm, d), lambda i, t: (0, i, t, 0))
    consts = (p["r_k"], p["ln_g"], p["ln_b"], p["w_o"], gsum, gbc)
    return pl.pallas_call(
        _rw_out_kernel,
        out_shape=jax.ShapeDtypeStruct((b, l, d), F32),
        grid=(b, nt),
        in_specs=[tok, pl.BlockSpec((tm, d), lambda i, t: (t, 0)), tok, tok, tok, tok, tok2, tok,
                  pl.BlockSpec((1, 1, 6, d), lambda i, t: (i, _seg_index(t, nct), 0, 0))]
                 + [_const_spec(a.shape) for a in consts],
        out_specs=tok,
        compiler_params=_params(2),
        name="rw_out",
    )(xs, pos, y[0], y[1], r, v, kd, gate, mod, *consts)


def _route(logits):
    ne = MOE_GROUPS * MOE_EXPERTS
    lane = lax.broadcasted_iota(jnp.int32, logits.shape, 1).astype(F32)
    far = float(4 * LANES)
    g_mask = jnp.logical_and(lane >= ne, lane < ne + MOE_GROUPS)
    gl = jnp.where(g_mask, logits, NEG_BIG)
    gmax = jnp.max(gl, axis=1, keepdims=True)
    gsum = jnp.sum(jnp.where(g_mask, jnp.exp(gl - gmax), 0.0), axis=1, keepdims=True)
    gp = 1.0 / gsum
    g_first = jnp.min(jnp.where(gl == gmax, lane, far), axis=1, keepdims=True) - ne
    in_grp = jnp.logical_and(lane >= g_first * MOE_EXPERTS, lane < (g_first + 1.0) * MOE_EXPERTS)
    el = jnp.where(in_grp, logits, NEG_BIG)
    m1 = jnp.max(el, axis=1, keepdims=True)
    i1 = jnp.min(jnp.where(el == m1, lane, far), axis=1, keepdims=True)
    el2 = jnp.where(lane == i1, NEG_BIG, el)
    m2 = jnp.max(el2, axis=1, keepdims=True)
    i2 = jnp.min(jnp.where(el2 == m2, lane, far), axis=1, keepdims=True)
    e2 = jnp.exp(m2 - m1)
    p1 = gp / (1.0 + e2)
    p2 = p1 * e2
    comb = jnp.where(lane == i1, p1, jnp.where(lane == i2, p2, 0.0))
    return jnp.where(lane == g_first + ne, 1.0, comb)


def _seg_index(t, n_ctx_tiles):
    return jnp.where(t >= n_ctx_tiles, 1, 0)


def _moe_route_kernel(x_ref, mod_ref, ng_ref, wr_ref, br_ref, h_ref):
    d = x_ref.shape[2]
    h = _norm_mod(x_ref[0], ng_ref[...], mod_ref[0, 0, 3:4, :], mod_ref[0, 0, 4:5, :])
    h_ref[0, :, 0:d] = _bf(h)
    hi, lo = _split2(_route(_dot_hh(h, wr_ref[...]) + br_ref[...]))
    h_ref[0, :, d:d + LANES] = hi
    h_ref[0, :, d + LANES:d + 2 * LANES] = lo


def _moe_route(xs, mod, ng, p, n_ctx):
    b, rows, d = xs.shape
    tm = TOKEN_TILE
    nct = n_ctx // tm
    return pl.pallas_call(
        _moe_route_kernel,
        out_shape=jax.ShapeDtypeStruct((b, rows, d + 2 * LANES), BF16),
        grid=(b, rows // tm),
        in_specs=[pl.BlockSpec((1, tm, d), lambda i, t: (i, t, 0)),
                  pl.BlockSpec((1, 1, 6, d), lambda i, t: (i, _seg_index(t, nct), 0, 0)),
                  _const_spec(ng.shape), _const_spec(p["w_r"].shape), _const_spec(p["b_r"].shape)],
        out_specs=pl.BlockSpec((1, tm, d + 2 * LANES), lambda i, t: (i, t, 0)),
        compiler_params=_params(2),
        name="moe_route",
    )(xs, mod, ng, p["w_r"], p["b_r"])


MOE_ROW_BLOCK = 128
MOE_EXPERTS_PER_STEP = 2


def _moe_expert_kernel(h_ref, tri_ref, upper_ref, wgu_ref, wd_ref, m_ref, xs_s, acc_s, dest_s, seg_s, *, tb):
    rows = h_ref.shape[1]
    slots = xs_s.shape[0]
    d = acc_s.shape[1]
    ff = wd_ref.shape[1]
    ne = MOE_GROUPS * MOE_EXPERTS
    rb = MOE_ROW_BLOCK
    e = pl.program_id(1)
    n_tb = rows // tb

    @pl.when(e == 0)
    def _():
        lane = lax.broadcasted_iota(jnp.int32, (1, LANES), 1)
        g_lanes = jnp.logical_and(lane >= ne, lane < ne + MOE_GROUPS)
        carry = jnp.zeros((1, LANES), F32)
        ranks = []
        for i in range(n_tb):
            gind = jnp.where(g_lanes, h_ref[0, i * tb:(i + 1) * tb, d:d + LANES].astype(F32), 0.0)
            r = jnp.dot(tri_ref[...], _bf(gind), preferred_element_type=F32) + carry
            carry = r[tb - 1:tb, :]
            ranks.append((gind, r))
        padded = jnp.ceil(carry * (1.0 / rb)) * rb
        start = _dot_x3(jnp.broadcast_to(padded, (HALO, LANES)), upper_ref[...])[0:1]
        for g in range(MOE_GROUPS):
            pick = lane == ne + g
            seg_s[g] = jnp.sum(jnp.where(pick, start, 0.0)).astype(jnp.int32)
            seg_s[MOE_GROUPS + g] = jnp.sum(jnp.where(pick, padded, 0.0)).astype(jnp.int32)
        dest_rows = []
        for i, (gind, r) in enumerate(ranks):
            dest = jnp.sum(gind * (start + r - 1.0), axis=1, keepdims=True)
            dest_b = jnp.broadcast_to(dest, (tb, LANES))
            dest_s[i * tb:(i + 1) * tb, :] = dest_b
            dest_rows.append(dest_b.T[0:1, :])
        dest_row = jnp.concatenate(dest_rows, axis=1)
        hb = h_ref[0]

        def permute(sb, c_):
            base = pl.multiple_of(sb * tb, tb)
            slot = (lax.broadcasted_iota(jnp.int32, (tb, rows), 0) + base).astype(F32)
            perm = jnp.where(slot == dest_row, 1.0, 0.0)
            xs_s[pl.ds(base, tb), :] = _bf(jnp.dot(_bf(perm), hb, preferred_element_type=F32))
            acc_s[pl.ds(base, tb), :] = jnp.zeros((tb, d), F32)
            return c_
        lax.fori_loop(0, slots // tb, permute, 0)

    n_here = wgu_ref.shape[0]
    grp = (e * n_here) // MOE_EXPERTS
    start = seg_s[grp]
    n_rows = seg_s[MOE_GROUPS + grp]

    def expert_rows(firsts, size):
        sls = [pl.ds(pl.multiple_of(first, rb), size) for first in firsts]
        xbs = [xs_s[sl, :] for sl in sls]
        work = [(bi, k) for bi in range(len(sls)) for k in range(n_here)]
        gus = [jnp.dot(xbs[bi][:, 0:d], wgu_ref[k], preferred_element_type=F32) for bi, k in work]
        css = [xb[:, d:d + LANES].astype(F32) + xb[:, d + LANES:d + 2 * LANES].astype(F32) for xb in xbs]
        acts = []
        for (bi, k), gu in zip(work, gus):
            lane = lax.broadcasted_iota(jnp.int32, css[bi].shape, 1)
            ce = jnp.sum(jnp.where(lane == e * n_here + k, css[bi], 0.0), axis=1, keepdims=True)
            acts.append(_bf(_silu(gu[:, 0:ff]) * gu[:, ff:2 * ff] * ce))
        outs = [jnp.dot(act, wd_ref[k], preferred_element_type=F32) for act, (_, k) in zip(acts, work)]
        for bi, sl in enumerate(sls):
            acc_s[sl, :] += sum(out for out, (bj, _) in zip(outs, work) if bj == bi)

    big_rows = 2 * rb
    n_pairs = n_rows // (2 * big_rows)

    def pair(i, c_):
        first = start + i * (2 * big_rows)
        expert_rows([first, first + big_rows], big_rows)
        return c_
    lax.fori_loop(0, n_pairs, pair, 0)
    done = n_pairs * (2 * big_rows)
    rest = n_rows - done

    @pl.when(rest >= big_rows)
    def _():
        expert_rows([start + done], big_rows)

    @pl.when(rest % big_rows > 0)
    def _():
        expert_rows([start + done + (rest // big_rows) * big_rows], rb)

    @pl.when(e == pl.num_programs(1) - 1)
    def _():
        def narrow(sb, c_):
            sl = pl.ds(pl.multiple_of(sb * tb, tb), tb)
            xs_s[sl, 0:d] = _bf(acc_s[sl, :])
            return c_
        lax.fori_loop(0, slots // tb, narrow, 0)
        acc = xs_s[:, 0:d]
        for i in range(n_tb):
            slot = lax.broadcasted_iota(jnp.int32, (tb, slots), 1).astype(F32)
            unperm = jnp.where(slot == dest_s[i * tb:(i + 1) * tb, 0:1], 1.0, 0.0)
            m_ref[0, i * tb:(i + 1) * tb, :] = _bf(jnp.dot(_bf(unperm), acc, preferred_element_type=F32))


def _moe_experts(h, p):
    b, rows, da = h.shape
    ne, d, ff2 = p["w_gu"].shape
    ff = ff2 // 2
    tb = TOKEN_TILE
    eps = MOE_EXPERTS_PER_STEP
    assert MOE_EXPERTS % eps == 0
    slots = rows + tb * (-(-(MOE_GROUPS * MOE_ROW_BLOCK) // tb))
    tri = jnp.tril(jnp.ones((tb, tb), BF16))
    upper = jnp.triu(jnp.ones((LANES, LANES), BF16), 1)
    return pl.pallas_call(
        functools.partial(_moe_expert_kernel, tb=tb),
        out_shape=jax.ShapeDtypeStruct((b, rows, d), BF16),
        grid=(b, ne // eps),
        in_specs=[pl.BlockSpec((1, rows, da), lambda i, e: (i, 0, 0), pipeline_mode=pl.Buffered(1)),
                  _const_spec(tri.shape), _const_spec(upper.shape),
                  pl.BlockSpec((eps, d, ff2), lambda i, e: (e, 0, 0)),
                  pl.BlockSpec((eps, ff, d), lambda i, e: (e, 0, 0))],
        out_specs=pl.BlockSpec((1, rows, d), lambda i, e: (i, 0, 0)),
        scratch_shapes=[pltpu.VMEM((slots, da), BF16), pltpu.VMEM((slots, d), F32),
                        pltpu.VMEM((rows, LANES), F32), pltpu.SMEM((2 * MOE_GROUPS,), jnp.int32)],
        compiler_params=_params(2),
        name="moe_experts",
    )(h, tri, upper, p["w_gu"], p["w_down"])


def _moe_resid_kernel(x_ref, m_ref, mod_ref, fg_ref, xo_ref, *, final_norm):
    xn = x_ref[0] + mod_ref[0, 0, 5:6, :] * m_ref[0].astype(F32)
    if final_norm:
        ms = jnp.mean(xn * xn, axis=-1, keepdims=True)
        xn = xn * lax.rsqrt(ms + NORM_EPS) * fg_ref[...]
    xo_ref[0] = xn


def _moe_resid(xs, m, mod, final_g, n_ctx, final_norm):
    b, rows, d = xs.shape
    tm = TOKEN_TILE
    nct = n_ctx // tm
    tok = pl.BlockSpec((1, tm, d), lambda i, t: (i, t, 0))
    return pl.pallas_call(
        functools.partial(_moe_resid_kernel, final_norm=final_norm),
        out_shape=jax.ShapeDtypeStruct((b, rows, d), F32),
        grid=(b, rows // tm),
        in_specs=[tok, tok, pl.BlockSpec((1, 1, 6, d), lambda i, t: (i, _seg_index(t, nct), 0, 0)),
                  _const_spec(final_g.shape)],
        out_specs=tok,
        compiler_params=_params(2),
        name="moe_resid",
    )(xs, m, mod, final_g)


def _moe(xs, mod, ng, p, final_g, n_ctx, final_norm):
    h = _moe_route(xs, mod, ng, p, n_ctx)
    m = _moe_experts(h, p)
    return _moe_resid(xs, m, mod, final_g, n_ctx, final_norm)


def _gd_proj_kernel(x_ref, xp_ref, xn_ref, mod_ref, ng_ref, w_ref, conv_ref, wab_ref, alog_ref, dtb_ref,
                    o_ref, gb_ref, h_s, *, tm, n_ctx, n_tot, n_qk_blk, n_conv_blk, n_vh):
    jb = pl.program_id(2)
    pos = pl.program_id(1) * tm

    @pl.when(jb == 0)
    def _():
        sh = mod_ref[0, 0, 0:1, :]
        sc = mod_ref[0, 0, 1:2, :]
        g = ng_ref[...]
        prev_ok = jnp.logical_and(pos != 0, pos != n_ctx)
        next_ok = jnp.logical_and(pos + tm != n_ctx, pos + tm != n_tot)
        h = _norm_mod(x_ref[0], g, sh, sc)
        hp = jnp.where(prev_ok, _norm_mod(xp_ref[0], g, sh, sc), 0.0)
        hn = jnp.where(next_ok, _norm_mod(xn_ref[0], g, sh, sc), 0.0)
        h_s[...] = _bf(jnp.concatenate([hp, h, hn], axis=0))
        ab = _dot_hh(h, wab_ref[...])
        lane = lax.broadcasted_iota(jnp.int32, (1, LANES), 1)
        for z in range(2):
            abz = ab if z == 0 else pltpu.roll(ab, LANES - 2 * n_vh, axis=1)
            gdec = -jnp.exp(alog_ref[z:z + 1, :]) * _softplus(abz + dtb_ref[z:z + 1, :])
            gb_ref[z, 0] = jnp.where(lane < n_vh, gdec, jnp.where(lane < 2 * n_vh, _sigmoid(abz), 0.0))

    ext = jnp.dot(h_s[...], w_ref[...], preferred_element_type=F32)
    n_ext = tm + 2 * HALO

    @pl.when(jb >= n_conv_blk)
    def _():
        o_ref[0] = ext[HALO:HALO + tm]

    @pl.when(jb < n_conv_blk)
    def _():
        pad = (GD_CONV_W - 1) // 2
        is_qk = jb < n_qk_blk
        scale = jnp.where(jb < n_qk_blk // 2, GD_DK ** -0.5, 1.0)
        for hh in range(ext.shape[1] // GD_DK):
            cs = slice(hh * GD_DK, (hh + 1) * GD_DK)
            e = ext[:, cs]
            acc = e[HALO:HALO + tm] * conv_ref[pad:pad + 1, cs]
            for wi in range(GD_CONV_W):
                if wi == pad:
                    continue
                shifted = pltpu.roll(e, (pad - wi) % n_ext, axis=0)
                acc = acc + shifted[HALO:HALO + tm] * conv_ref[wi:wi + 1, cs]
            act = _silu(acc)
            ss = jnp.sum(act * act, axis=-1, keepdims=True)
            o_ref[0, :, cs] = act * jnp.where(is_qk, lax.rsqrt(ss + NORM_EPS) * scale, 1.0)


def _gd_proj(xs, mod, ng, w_in, conv_w, wab, alog, dtb, n_ctx, kd_total, qkv_total, n_vh):
    b, l, d = xs.shape
    n_out = w_in.shape[1]
    tm = TOKEN_TILE
    nb = 1024
    nt = l // tm
    nct = n_ctx // tm
    nj = n_out // nb
    n_conv_blk = qkv_total // nb
    kern = functools.partial(_gd_proj_kernel, tm=tm, n_ctx=n_ctx, n_tot=l, n_qk_blk=2 * kd_total // nb,
                             n_conv_blk=n_conv_blk, n_vh=n_vh)
    return pl.pallas_call(
        kern,
        out_shape=(jax.ShapeDtypeStruct((b, l, n_out), F32), jax.ShapeDtypeStruct((2, b, l, LANES), F32)),
        grid=(b, nt, nj),
        in_specs=_tile_specs(tm, d, nt)
                 + [pl.BlockSpec((1, 1, 6, d), lambda i, t, j: (i, _seg_index(t, nct), 0, 0)),
                    _const_spec(ng.shape),
                    pl.BlockSpec((d, nb), lambda i, t, j: (0, j)),
                    pl.BlockSpec((GD_CONV_W, nb), lambda i, t, j: (0, jnp.minimum(j, n_conv_blk - 1))),
                    _const_spec(wab.shape), _const_spec(alog.shape), _const_spec(dtb.shape)],
        out_specs=(pl.BlockSpec((1, tm, nb), lambda i, t, j: (i, t, j)),
                   pl.BlockSpec((2, 1, tm, LANES), lambda i, t, j: (0, i, t, 0))),
        scratch_shapes=[pltpu.VMEM((tm + 2 * HALO, d), BF16)],
        compiler_params=_params(3),
        name="gd_proj",
    )(xs, xs, xs, mod, ng, w_in, conv_w, wab, alog, dtb)


def _gd_chunk_kernel(q0_ref, k0_ref, v0_ref, gb0_ref, q1_ref, k1_ref, v1_ref, gb1_ref,
                     o0_ref, o1_ref, s_ref, *, n_vh, heads_per_group):
    c = q0_ref.shape[1]
    n = heads_per_group
    rep = n_vh // (q0_ref.shape[2] // GD_DK)
    n_groups = n_vh // n
    gw = n * c
    kh_per_group = n // rep
    assert 2 * c == LANES and n_vh % 2 == 0

    @pl.when(pl.program_id(1) == 0)
    def _():
        s_ref[...] = jnp.zeros_like(s_ref)

    bd = _block_mask(gw, gw, c, c)
    bd_k = _block_mask(gw, kh_per_group * GD_DK, c, GD_DK, rdiv=rep)
    bd_v = _block_mask(gw, n * GD_DV, c, GD_DV)
    low_half = lax.broadcasted_iota(jnp.int32, (1, LANES), 1) < c

    def bdiag_v(x):
        return jnp.where(bd_v, _tile_rows(x, n), 0.0)

    def spread_dv(x, off):
        return jnp.concatenate([jnp.broadcast_to(x[:, off + h:off + h + 1], (c, GD_DV)) for h in range(n_vh)], axis=1)

    def spread_c(x, off):
        cols = [jnp.broadcast_to(x[:, off + h:off + h + 1], (c, LANES)) for h in range(n_vh)]
        return jnp.concatenate([jnp.where(low_half, cols[h], cols[h + 1]) for h in range(0, n_vh, 2)], axis=1)

    refs = ((q0_ref, k0_ref, v0_ref, gb0_ref, o0_ref), (q1_ref, k1_ref, v1_ref, gb1_ref, o1_ref))
    n_units = 2 * q0_ref.shape[0]
    prep = []
    for unit in range(n_units):
        bi, z = divmod(unit, 2)
        q_ref, k_ref, v_ref, gb_ref, _ = refs[z]
        gb = gb_ref[0, bi]
        gc = _cumsum_rows(gb, z == 1)
        gtot = jnp.sum(gb, axis=0, keepdims=True)
        incl_all, _, _ = _order_masks(z, c, n_vh)
        gt64 = spread_c(gc, 0)
        sq = jnp.concatenate([gc, jnp.zeros((LANES - c, LANES), F32)], axis=0).T
        sq_hi = pltpu.roll(sq, c, axis=1)
        gs64 = jnp.concatenate([jnp.where(low_half, sq[h:h + 1, :], sq_hi[h + 1:h + 2, :])
                                for h in range(0, n_vh, 2)], axis=1)
        prep.append({
            "gam": jnp.where(incl_all, jnp.exp(jnp.where(incl_all, gt64 - gs64, 0.0)), 0.0),
            "bt64": spread_c(gb, n_vh),
            "e_g": spread_dv(jnp.exp(gc), 0),
            "e_end": spread_dv(jnp.exp(gtot - gc), 0),
            "beta": spread_dv(gb, n_vh),
            "gl": jnp.exp(gtot),
            "q": q_ref[bi], "k": k_ref[bi], "v": v_ref[bi], "masks": _order_masks(z, c, n)})

    chains = [(unit, gi) for unit in range(n_units) for gi in range(n_groups)]

    def ksl(gi):
        return slice(gi * kh_per_group * GD_DK, (gi + 1) * kh_per_group * GD_DK)

    def csl(gi):
        return slice(gi * gw, (gi + 1) * gw)

    def vsl(gi):
        return slice(gi * n * GD_DV, (gi + 1) * n * GD_DV)

    def per_vhead(x):
        return jnp.concatenate([x[:, (hh // rep) * GD_DK:(hh // rep + 1) * GD_DK] for hh in range(n)], axis=1)

    kg_ = [prep[z]["k"][:, ksl(gi)] for z, gi in chains]
    qg_ = [prep[z]["q"][:, ksl(gi)] for z, gi in chains]
    qkk = [_dot_nt(jnp.concatenate([kx, qx], axis=0), jnp.where(bd_k, _tile_rows(kx, n), 0.0))
           for kx, qx in zip(kg_, qg_)]
    a_mat = [jnp.where(prep[z]["masks"][1], x[:c] * prep[z]["gam"][:, csl(gi)] * prep[z]["bt64"][:, csl(gi)], 0.0)
             for x, (z, gi) in zip(qkk, chains)]
    aqk = [jnp.where(prep[z]["masks"][0], x[c:] * prep[z]["gam"][:, csl(gi)], 0.0) for x, (z, gi) in zip(qkk, chains)]
    t_inv = _tri_inv([-a for a in a_mat], bd, [prep[z]["masks"][2] for z, _ in chains],
                     [z % 2 == 1 for z, _ in chains])

    k2 = [per_vhead(x) for x in kg_]
    q2 = [per_vhead(x) for x in qg_]
    bg = [prep[z]["beta"][:, vsl(gi)] for z, gi in chains]
    eg = [prep[z]["e_g"][:, vsl(gi)] for z, gi in chains]
    u = [_dot(t, bdiag_v(prep[z]["v"][:, vsl(gi)] * b_)) for t, b_, (z, gi) in zip(t_inv, bg, chains)]
    w = [_dot(t, bdiag_v(kx * b_ * e_)) for t, kx, b_, e_ in zip(t_inv, k2, bg, eg)]
    qe = [qx * e_ for qx, e_ in zip(q2, eg)]
    ke = [kx * prep[z]["e_end"][:, vsl(gi)] for kx, (z, gi) in zip(k2, chains)]

    heads = [(ci, hh) for ci in range(len(chains)) for hh in range(n)]

    def hs(hh):
        return slice(hh * GD_DV, (hh + 1) * GD_DV)

    def state_index(ci, hh):
        z, gi = chains[ci]
        return z, gi * n + hh

    ws = {(ci, hh): _dot(jnp.concatenate([w[ci][:, hs(hh)], qe[ci][:, hs(hh)]], axis=0), s_ref[state_index(ci, hh)])
          for ci, hh in heads}
    vn = [jnp.concatenate([u[ci][:, hs(hh)] - ws[ci, hh][:c] for hh in range(n)], axis=1) for ci in range(len(chains))]
    for ci, (z, gi) in enumerate(chains):
        pre = jnp.concatenate([ws[ci, hh][c:] for hh in range(n)], axis=1)
        refs[z % 2][4][z // 2, :, vsl(gi)] = pre + _dot(aqk[ci], bdiag_v(vn[ci]))
    for ci, hh in heads:
        z, hv = state_index(ci, hh)
        gl = jnp.broadcast_to(prep[z]["gl"][:, hv:hv + 1], (1, GD_DV))
        s_ref[z, hv] = s_ref[z, hv] * gl + _dot_tn(ke[ci][:, hs(hh)], vn[ci][:, hs(hh)])


def _gd_chunk(proj, gb, n_ctx, kd_total, vd_total):
    b, l, _ = proj.shape
    c = CHUNK
    nc = l // c
    ncc = n_ctx // c
    n_vh = vd_total // GD_DV
    bpb = SCAN_BATCH if b % SCAN_BATCH == 0 else 1

    def col(z, width, blk):
        return pl.BlockSpec((bpb, c, width), lambda i, j: (i, _chunk_index(z, j, ncc, nc), blk))

    def perdir(z):
        return pl.BlockSpec((1, bpb, c, LANES), lambda i, j: (z, i, _chunk_index(z, j, ncc, nc), 0))

    def dir_specs(z):
        return [col(z, kd_total, 0), col(z, kd_total, 1), col(z, vd_total, 2 * kd_total // vd_total), perdir(z)]

    sd = jax.ShapeDtypeStruct((b, l, vd_total), F32)
    return pl.pallas_call(
        functools.partial(_gd_chunk_kernel, n_vh=n_vh, heads_per_group=4),
        out_shape=(sd, sd),
        grid=(b // bpb, nc),
        in_specs=dir_specs(0) + dir_specs(1),
        out_specs=(col(0, vd_total, 0), col(1, vd_total, 0)),
        scratch_shapes=[pltpu.VMEM((2 * bpb, n_vh, GD_DK, GD_DV), F32)],
        compiler_params=_params(2),
        name="gd_chunk",
    )(proj, proj, proj, gb, proj, proj, proj, gb)


def _gd_out_kernel(x_ref, o0_ref, o1_ref, z_ref, mod_ref, ngd_ref, wo_ref, xo_ref):
    o = o0_ref[0] + o1_ref[0]
    z = z_ref[0]
    parts = []
    for hh in range(o.shape[1] // GD_DV):
        seg = o[:, hh * GD_DV:(hh + 1) * GD_DV]
        ms = jnp.mean(seg * seg, axis=-1, keepdims=True)
        parts.append(seg * lax.rsqrt(ms + NORM_EPS) * ngd_ref[...])
    on = jnp.concatenate(parts, axis=1) * _silu(z)
    xo_ref[0] = x_ref[0] + mod_ref[0, 0, 2:3, :] * _dot(on, wo_ref[...])


def _gd_out(xs, o, proj, mod, ngd, w_o, n_ctx, z_blk):
    b, l, d = xs.shape
    vd = o[0].shape[-1]
    tm = TOKEN_TILE
    nct = n_ctx // tm
    nt = (l - n_ctx) // tm
    return pl.pallas_call(
        _gd_out_kernel,
        out_shape=jax.ShapeDtypeStruct((b, l - n_ctx, d), F32),
        grid=(b, nt),
        in_specs=[pl.BlockSpec((1, tm, d), lambda i, t: (i, t + nct, 0)),
                  pl.BlockSpec((1, tm, vd), lambda i, t: (i, t + nct, 0)),
                  pl.BlockSpec((1, tm, vd), lambda i, t: (i, t + nct, 0)),
                  pl.BlockSpec((1, tm, vd), lambda i, t: (i, t + nct, z_blk)),
                  pl.BlockSpec((1, 1, 6, d), lambda i, t: (i, 1, 0, 0)),
                  _const_spec(ngd.shape), _const_spec(w_o.shape)],
        out_specs=pl.BlockSpec((1, tm, d), lambda i, t: (i, t, 0)),
        compiler_params=_params(2),
        name="gd_out",
    )(xs, o[0], o[1], proj, mod, ngd, w_o)


def _pos_embed_2d(rows, d):
    quarter = d // 4
    omega = 1.0 / (POS_BASE ** (jnp.arange(quarter, dtype=F32) / quarter))

    def axis_emb(n):
        ang = jnp.arange(n, dtype=F32)[:, None] * omega[None, :]
        return jnp.concatenate([jnp.sin(ang), jnp.cos(ang)], axis=-1)

    e_row = jnp.broadcast_to(axis_emb(rows)[:, None, :], (rows, GRID_W, d // 2))
    e_col = jnp.broadcast_to(axis_emb(GRID_W)[None, :, :], (rows, GRID_W, d // 2))
    return jnp.concatenate([e_row, e_col], axis=-1).reshape(rows * GRID_W, d)


def _head_indicator(d, head):
    ch = jnp.arange(d)[:, None] // head
    ind = (ch == jnp.arange(LANES)[None, :]).astype(BF16)
    return ind, ind.T


def _block_diag2(m):
    z = jnp.zeros_like(m[0])
    return jnp.concatenate([jnp.concatenate([m[0], z], axis=1), jnp.concatenate([z, m[1]], axis=1)], axis=0)


def _moe_params(i, moe_w_rg, moe_b_rg, moe_w_re, moe_b_re, moe_w_gate, moe_w_up, moe_w_down):
    d = moe_w_rg.shape[1]
    ne = MOE_GROUPS * MOE_EXPERTS
    pad = LANES - ne - MOE_GROUPS
    w_r = jnp.concatenate([moe_w_re[i], moe_w_rg[i], jnp.zeros((d, pad), F32)], axis=1)
    b_r = jnp.concatenate([moe_b_re[i], moe_b_rg[i], jnp.zeros((pad,), F32)])[None, :]
    ff = moe_w_gate.shape[-1]
    return {"w_r": w_r, "b_r": b_r,
            "w_gu": jnp.concatenate([_bf(moe_w_gate[i]), _bf(moe_w_up[i])], axis=-1).reshape(ne, d, 2 * ff),
            "w_down": _bf(moe_w_down[i]).reshape(ne, ff, d)}


def kernel(x, c, ctx, c_ctx, ada_w, ada_b, norm1_g, norm2_g, rw_mix, rw_w_rkv, rw_w0, rw_w1, rw_w2, rw_a0, rw_a1, rw_a2, rw_g1, rw_g2, rw_k_k, rw_k_a, rw_r_k, rw_ln_g, rw_ln_b, rw_w_o, gd_w_in, gd_conv, gd_w_ab, gd_a_log, gd_dt_bias, gd_norm_g, gd_w_o, moe_w_rg, moe_b_rg, moe_w_re, moe_b_re, moe_w_gate, moe_w_up, moe_w_down, final_g):
    bsz, n_lat, d = x.shape
    n_ctx = ctx.shape[1]
    assert n_ctx % TOKEN_TILE == 0 and n_lat % TOKEN_TILE == 0 and n_ctx % CHUNK == 0
    assert ada_w.shape[0] == 2 and d % (4 * RW_HEAD) == 0

    rows = -(-(bsz + 1) // HALO) * HALO
    cs = jnp.zeros((rows, d), F32).at[:bsz].set(c).at[bsz].set(c_ctx)
    mod_all = _ada(cs, ada_w, ada_b)

    def mod_of(i):
        lat = mod_all[i, :bsz].reshape(bsz, 1, 6, d)
        cx = jnp.broadcast_to(mod_all[i, bsz].reshape(1, 1, 6, d), (bsz, 1, 6, d))
        return jnp.concatenate([cx, lat], axis=1)

    xs = jnp.concatenate([ctx, x], axis=1)
    pos = jnp.concatenate([jnp.zeros((n_ctx, d), F32), _pos_embed_2d(n_lat // GRID_W, d)], axis=0)
    gsum, gbc = _head_indicator(d, RW_HEAD)

    mod0 = mod_of(0)
    ng1 = norm1_g[0][None, :]
    rw = {"mix": rw_mix[0], "w_rkv": _bf(rw_w_rkv[0]), "w0": rw_w0[0],
          "w1": _bf(jnp.concatenate([rw_w1[0, 0], rw_w1[0, 1]], axis=1)), "w2": _bf(_block_diag2(rw_w2[0])),
          "a0": rw_a0[0],
          "a1": _bf(jnp.concatenate([rw_a1[0, 0], rw_a1[0, 1]], axis=1)), "a2": _bf(_block_diag2(rw_a2[0])),
          "g1": _bf(rw_g1[0]), "g2": _bf(rw_g2[0]), "k_k": rw_k_k[0][None, :], "k_a": rw_k_a[0][None, :],
          "r_k": rw_r_k[0].reshape(1, d), "ln_g": rw_ln_g[0][None, :], "ln_b": rw_ln_b[0][None, :],
          "w_o": _bf(rw_w_o[0])}
    r, v, kk, gate, lw, kd, ar = _rw_feat(xs, pos, mod0, ng1, rw, gsum, gbc, n_ctx)
    y = _rw_scan(r, v, kk, lw, kd, ar, n_ctx)
    xs = _rw_out(xs, pos, y, r, v, kd, gate, mod0, rw, gsum, gbc, n_ctx)
    moe_args = (moe_w_rg, moe_b_rg, moe_w_re, moe_b_re, moe_w_gate, moe_w_up, moe_w_down)
    fg = final_g[None, :]
    xs = _moe(xs, mod0, norm2_g[0][None, :], _moe_params(0, *moe_args), fg, n_ctx, False)

    mod1 = mod_of(1)
    n_vh = gd_a_log.shape[-1]
    vd_total = n_vh * GD_DV
    kd_total = (gd_w_in.shape[-1] - 2 * vd_total) // 2
    qkv_total = 2 * kd_total + vd_total
    ng1 = norm1_g[1][None, :]
    wab = jnp.concatenate([gd_w_ab[0, 0], gd_w_ab[0, 1], jnp.zeros((d, LANES - 4 * n_vh), F32)], axis=1)
    lane_pad = jnp.zeros((2, LANES - n_vh), F32)
    alog = jnp.concatenate([gd_a_log[0], lane_pad], axis=1)
    dtb = jnp.concatenate([gd_dt_bias[0], lane_pad], axis=1)
    proj, gb = _gd_proj(xs, mod1, ng1, _bf(gd_w_in[0]), gd_conv[0], wab, alog, dtb, n_ctx, kd_total, qkv_total, n_vh)
    o = _gd_chunk(proj, gb, n_ctx, kd_total, vd_total)
    ngd = gd_norm_g[0][None, :]
    x_lat = _gd_out(xs, o, proj, mod1, ngd, _bf(gd_w_o[0]), n_ctx, qkv_total // vd_total)
    return _moe(x_lat, mod1, norm2_g[1][None, :], _moe_params(1, *moe_args), fg, 0, True)
```
